```python
import jax, jax.numpy as jnp
from jax import lax
import numpy as np

D_MODEL = 1024
BATCH = 2
SEQ = 8192
DEPTH = 1
DEC_BATCH = 32
DEC_SEQ = 32
PAST_LEN = 1024

CHUNK = 64
HEAD_DIM = 64
CONV_DIM = 256
CONV_K = 3
FOX_HEADS = 8
FOX_DIM = FOX_HEADS * HEAD_DIM
MEM_HEADS = 4
MEM_DIM = MEM_HEADS * HEAD_DIM
N_MEM = 256
MIX_DIM = CONV_DIM + FOX_DIM + MEM_DIM
IN_COLS = 3 * CONV_DIM + 3 * FOX_DIM + FOX_HEADS + MEM_DIM
D_FF = 2816
Q_BLOCK = 128
EPS = 1e-6
F32 = jnp.float32

kernel_name = 'hybrid_streaming_conv_fox_memory_step'


def _rmsnorm(x, g):
    xf = x.astype(F32)
    y = xf * lax.rsqrt(jnp.mean(xf * xf, axis=-1, keepdims=True) + EPS)
    return (y * g.astype(F32)).astype(x.dtype)


def _head_rms(x, g):
    return _rmsnorm(x, g)


def _swiglu(x, w_gate, w_up, w_down):
    return (jax.nn.silu(x @ w_gate) * (x @ w_up)) @ w_down


def _fox_attention(q, k, v, cum_q, cum_k, q_offset):
    bsz, tq, nh, dh = q.shape
    tk = k.shape[1]
    kpos = jnp.arange(tk)
    cum_k_t = jnp.swapaxes(cum_k, 1, 2)
    scale = dh ** -0.5

    def block(args):
        qb, cqb, qpos = args
        s = jnp.einsum('bqhd,bkhd->bhqk', qb, k).astype(F32) * scale
        s = s + jnp.swapaxes(cqb, 1, 2)[..., :, None] - cum_k_t[..., None, :]
        s = jnp.where(kpos[None, :] <= qpos[:, None], s, -jnp.inf)
        p = jax.nn.softmax(s, axis=-1)
        return jnp.einsum('bhqk,bkhd->bqhd', p.astype(v.dtype), v)

    qpos = q_offset + jnp.arange(tq)
    if tq <= Q_BLOCK:
        return block((q, cum_q, qpos))
    nb = tq // Q_BLOCK
    qb = q.reshape(bsz, nb, Q_BLOCK, nh, dh).transpose(1, 0, 2, 3, 4)
    cqb = cum_q.reshape(bsz, nb, Q_BLOCK, nh).transpose(1, 0, 2, 3)
    pb = qpos.reshape(nb, Q_BLOCK)
    out = lax.map(block, (qb, cqb, pb))
    return out.transpose(1, 0, 2, 3, 4).reshape(bsz, tq, nh, dh)


def _mem_attention(q, k, v):
    s = jnp.einsum('bqhd,bkhd->bhqk', q, k).astype(F32) * (HEAD_DIM ** -0.5)
    p = jax.nn.softmax(s, axis=-1)
    return jnp.einsum('bhqk,bkhd->bqhd', p.astype(v.dtype), v)


def _mem_kv(mem, g_mem, w_mem_kv, gk_mem):
    bsz, n, _ = mem.shape
    kv = _rmsnorm(mem, g_mem) @ w_mem_kv
    mk, mv = jnp.split(kv, [MEM_DIM], axis=-1)
    mk = _head_rms(mk.reshape(bsz, n, MEM_HEADS, HEAD_DIM), gk_mem)
    return mk, mv.reshape(bsz, n, MEM_HEADS, HEAD_DIM)


def _layer(x, conv_prev, past_k, past_v, past_logf, mem_k, mem_v, lp):
    (g_f1, w_g1, w_u1, w_d1, g_mix, w_in, b_f, conv_w, conv_b,
     gq_fox, gk_fox, gq_mem, w_out, g_f2, w_g2, w_u2, w_d2, g_fin) = lp
    bsz, t, _ = x.shape
    x = x + 0.5 * _swiglu(_rmsnorm(x, g_f1), w_g1, w_u1, w_d1)
    h = _rmsnorm(x, g_mix)
    proj = h @ w_in
    bounds = np.cumsum([CONV_DIM, CONV_DIM, CONV_DIM, FOX_DIM, FOX_DIM, FOX_DIM, FOX_HEADS]).tolist()
    u, c_gate, b_gate, q, k, v, f_logit, q_mem = jnp.split(proj, bounds, axis=-1)
    conv_in = c_gate * u
    padded = jnp.concatenate([conv_prev.astype(conv_in.dtype), conv_in], axis=1)
    conv = conv_b + conv_w[CONV_K - 1] * padded[:, CONV_K - 1:CONV_K - 1 + t]
    for i in range(CONV_K - 1):
        conv = conv + conv_w[i] * padded[:, i:i + t]
    y_conv = b_gate * conv
    new_conv = padded[:, t:]
    q = _head_rms(q.reshape(bsz, t, FOX_HEADS, HEAD_DIM), gq_fox)
    k = _head_rms(k.reshape(bsz, t, FOX_HEADS, HEAD_DIM), gk_fox)
    v = v.reshape(bsz, t, FOX_HEADS, HEAD_DIM)
    logf = jax.nn.log_sigmoid(f_logit.astype(F32) + b_f.astype(F32))
    p_len = past_k.shape[1]
    k_all = jnp.concatenate([past_k.astype(k.dtype), k], axis=1)
    v_all = jnp.concatenate([past_v.astype(v.dtype), v], axis=1)
    cum = jnp.cumsum(jnp.concatenate([past_logf.astype(F32), logf], axis=1), axis=1)
    y_fox = _fox_attention(q, k_all, v_all, cum[:, p_len:], cum, p_len)
    q_mem = _head_rms(q_mem.reshape(bsz, t, MEM_HEADS, HEAD_DIM), gq_mem)
    y_mem = _mem_attention(q_mem, mem_k.astype(q_mem.dtype), mem_v.astype(q_mem.dtype))
    mix = jnp.concatenate([y_conv, y_fox.reshape(bsz, t, FOX_DIM), y_mem.reshape(bsz, t, MEM_DIM)], axis=-1)
    x = x + mix @ w_out
    x = x + 0.5 * _swiglu(_rmsnorm(x, g_f2), w_g2, w_u2, w_d2)
    x = _rmsnorm(x, g_fin)
    return x, k, v, logf.astype(x.dtype), new_conv


def setup_inputs(seed: int = 0) -> dict:
    key = jax.random.key(seed)
    ks = jax.random.split(key, 32)

    def nrm(k, shape, scale=1.0):
        return jax.random.normal(k, shape, F32) * scale

    def gain(k, n):
        return 1.0 + 0.02 * jax.random.normal(k, (DEPTH, n), F32)

    return {
        'x_prompt': nrm(ks[0], (BATCH, SEQ, D_MODEL)),
        'x_sample': nrm(ks[1], (DEC_BATCH, DEC_SEQ, D_MODEL)),
        'cache_fox_k': nrm(ks[2], (DEPTH, DEC_BATCH, PAST_LEN, FOX_HEADS, HEAD_DIM)),
        'cache_fox_v': nrm(ks[3], (DEPTH, DEC_BATCH, PAST_LEN, FOX_HEADS, HEAD_DIM)),
        'cache_fox_logf': jax.nn.log_sigmoid(2.0 + nrm(ks[4], (DEPTH, DEC_BATCH, PAST_LEN, FOX_HEADS), 0.5)),
        'state_conv': nrm(ks[5], (DEPTH, DEC_BATCH, CONV_K - 1, CONV_DIM)),
        'cache_mem_k': nrm(ks[6], (DEPTH, DEC_BATCH, N_MEM, MEM_HEADS, HEAD_DIM)),
        'cache_mem_v': nrm(ks[7], (DEPTH, DEC_BATCH, N_MEM, MEM_HEADS, HEAD_DIM)),
        'mem_prompt': nrm(ks[8], (BATCH, N_MEM, D_MODEL)),
        'norm_ffn1': gain(ks[9], D_MODEL),
        'w_ffn1_gate': nrm(ks[10], (DEPTH, D_MODEL, D_FF), D_MODEL ** -0.5),
        'w_ffn1_up': nrm(ks[11], (DEPTH, D_MODEL, D_FF), D_MODEL ** -0.5),
        'w_ffn1_down': nrm(ks[12], (DEPTH, D_FF, D_MODEL), D_FF ** -0.5),
        'norm_mix': gain(ks[13], D_MODEL),
        'w_in': nrm(ks[14], (DEPTH, D_MODEL, IN_COLS), D_MODEL ** -0.5),
        'b_forget': 2.0 + nrm(ks[15], (DEPTH, FOX_HEADS), 0.5),
        'conv_w': nrm(ks[16], (DEPTH, CONV_K, CONV_DIM), CONV_K ** -0.5),
        'conv_b': nrm(ks[17], (DEPTH, CONV_DIM), 0.02),
        'q_norm_fox': gain(ks[18], HEAD_DIM),
        'k_norm_fox': gain(ks[19], HEAD_DIM),
        'q_norm_mem': gain(ks[20], HEAD_DIM),
        'k_norm_mem': gain(ks[21], HEAD_DIM),
        'norm_mem': gain(ks[22], D_MODEL),
        'w_mem_kv': nrm(ks[23], (DEPTH, D_MODEL, 2 * MEM_DIM), D_MODEL ** -0.5),
        'w_out': nrm(ks[24], (DEPTH, MIX_DIM, D_MODEL), MIX_DIM ** -0.5),
        'norm_ffn2': gain(ks[25], D_MODEL),
        'w_ffn2_gate': nrm(ks[26], (DEPTH, D_MODEL, D_FF), D_MODEL ** -0.5),
        'w_ffn2_up': nrm(ks[27], (DEPTH, D_MODEL, D_FF), D_MODEL ** -0.5),
        'w_ffn2_down': nrm(ks[28], (DEPTH, D_FF, D_MODEL), D_FF ** -0.5),
        'norm_final': gain(ks[29], D_MODEL),
    }


def reference(x_prompt, x_sample, cache_fox_k, cache_fox_v, cache_fox_logf, state_conv,
              cache_mem_k, cache_mem_v, mem_prompt,
              norm_ffn1, w_ffn1_gate, w_ffn1_up, w_ffn1_down, norm_mix, w_in, b_forget,
              conv_w, conv_b, q_norm_fox, k_norm_fox, q_norm_mem, k_norm_mem, norm_mem,
              w_mem_kv, w_out, norm_ffn2, w_ffn2_gate, w_ffn2_up, w_ffn2_down, norm_final):
    yp, ys = x_prompt, x_sample
    bp = x_prompt.shape[0]
    kp_l, vp_l, fp_l, cp_l, mkp_l, mvp_l = [], [], [], [], [], []
    ks_l, vs_l, fs_l, cs_l = [], [], [], []
    for l in range(DEPTH):
        lp = (norm_ffn1[l], w_ffn1_gate[l], w_ffn1_up[l], w_ffn1_down[l], norm_mix[l], w_in[l],
              b_forget[l], conv_w[l], conv_b[l], q_norm_fox[l], k_norm_fox[l], q_norm_mem[l],
              w_out[l], norm_ffn2[l], w_ffn2_gate[l], w_ffn2_up[l], w_ffn2_down[l], norm_final[l])
        mk_p, mv_p = _mem_kv(mem_prompt, norm_mem[l], w_mem_kv[l], k_norm_mem[l])
        zero_conv = jnp.zeros((bp, CONV_K - 1, CONV_DIM), yp.dtype)
        zero_kv = jnp.zeros((bp, 0, FOX_HEADS, HEAD_DIM), yp.dtype)
        zero_f = jnp.zeros((bp, 0, FOX_HEADS), F32)
        yp, k_p, v_p, f_p, c_p = _layer(yp, zero_conv, zero_kv, zero_kv, zero_f, mk_p, mv_p, lp)
        ys, k_s, v_s, f_s, c_s = _layer(ys, state_conv[l], cache_fox_k[l], cache_fox_v[l],
                                         cache_fox_logf[l], cache_mem_k[l], cache_mem_v[l], lp)
        kp_l.append(k_p); vp_l.append(v_p); fp_l.append(f_p); cp_l.append(c_p)
        mkp_l.append(mk_p); mvp_l.append(mv_p)
        ks_l.append(k_s); vs_l.append(v_s); fs_l.append(f_s); cs_l.append(c_s)
    return (yp, ys,
            jnp.stack(kp_l), jnp.stack(vp_l), jnp.stack(fp_l), jnp.stack(cp_l),
            jnp.stack(mkp_l), jnp.stack(mvp_l),
            jnp.stack(ks_l), jnp.stack(vs_l), jnp.stack(fs_l), jnp.stack(cs_l))
```

```python
import functools

import jax
import jax.numpy as jnp
from jax import lax
from jax.experimental import pallas as pl
from jax.experimental.pallas import tpu as pltpu

F32 = jnp.float32
BF16 = jnp.bfloat16

D_MODEL = 1024
HEAD_DIM = 64
CONV_DIM = 256
CONV_K = 3
FOX_HEADS = 8
FOX_DIM = FOX_HEADS * HEAD_DIM
MEM_HEADS = 4
MEM_DIM = MEM_HEADS * HEAD_DIM
N_MEM = 256
D_FF = 2816
EPS = 1e-6

LANES = 128
PAIR = 2 * HEAD_DIM
FF_CHUNK = 256
N_FF_CHUNKS = D_FF // FF_CHUNK
AUG = LANES
AUG_GROUP = AUG // FOX_HEADS
QK_COLS = FOX_DIM + AUG
ATT_COLS = FOX_DIM + MEM_DIM
COL_Q = 3 * CONV_DIM
COL_K = COL_Q + FOX_DIM
COL_V = COL_K + FOX_DIM
COL_QM = COL_V + FOX_DIM
COL_FREP = COL_QM + MEM_DIM
COL_F = COL_FREP + AUG
IN_COLS_EXT = COL_F + LANES
NEG = -1e30
SCALE = HEAD_DIM ** -0.5
VMEM_LIMIT = 56 * 1024 * 1024


def _idiv(x, d):
    if d & (d - 1) == 0:
        return lax.shift_right_logical(x, d.bit_length() - 1)
    return x // d


def _imod(x, d):
    if d & (d - 1) == 0:
        return x & (d - 1)
    return x % d


def _rms(x, g):
    ms = jnp.mean(x * x, axis=-1, keepdims=True)
    return x * lax.rsqrt(ms + EPS) * g


def _pair_headnorm(x, g):
    lo = lax.broadcasted_iota(jnp.int32, (1, PAIR), 1) < HEAD_DIM
    x2 = x * x
    s_lo = jnp.sum(jnp.where(lo, x2, 0.0), axis=-1, keepdims=True)
    s_hi = jnp.sum(jnp.where(lo, 0.0, x2), axis=-1, keepdims=True)
    r = jnp.where(lo, lax.rsqrt(s_lo * (1.0 / HEAD_DIM) + EPS),
                  lax.rsqrt(s_hi * (1.0 / HEAD_DIM) + EPS))
    return x * r * g


def _split3(x):
    hi = x.astype(BF16).astype(F32)
    r = x - hi
    mid = r.astype(BF16).astype(F32)
    lo = (r - mid).astype(BF16).astype(F32)
    return hi, mid, lo


def _ffn(xn, wgu_ref, wd_ref):
    acc = None
    for c in range(N_FF_CHUNKS):
        gu = jnp.dot(xn, wgu_ref[c], preferred_element_type=F32)
        g = gu[:, :FF_CHUNK]
        u = gu[:, FF_CHUNK:]
        a = (g * jax.nn.sigmoid(g) * u).astype(BF16)
        d = jnp.dot(a, wd_ref[c], preferred_element_type=F32)
        acc = d if acc is None else acc + d
    return acc


def _log_sigmoid(x):
    return jnp.minimum(x, 0.0) - jnp.log1p(jnp.exp(-jnp.abs(x)))


def _softmax_rows(s):
    m = jnp.max(s, axis=-1, keepdims=True)
    p = jnp.exp(s - m)
    return p, jnp.sum(p, axis=-1, keepdims=True)


def _dot_nt(a, b):
    return lax.dot_general(a, b, (((1,), (1,)), ((), ())), preferred_element_type=F32)


def _stage1_body(nb, seq, carry, *refs):
    tm = nb * seq
    if carry:
        x_ref, rest = refs[0], refs[1:]
        st_ref = None
    else:
        x_ref, st_ref, rest = refs[0], refs[1], refs[2:]
    (g1_ref, wgu_ref, wd_ref, gmix_ref, win_ref, cw_ref, cb_ref, gq_ref, gk_ref, gqm_ref,
     bfr_ref, bf_ref,
     x1_ref, yc_ref, qa_ref, ka_ref, vb_ref, kf_ref, vf_ref, lf_ref, qm_ref, nc_ref,
     cs_ref, cum_ref) = rest

    x = x_ref[...]
    y = _ffn(_rms(x, g1_ref[...]).astype(BF16), wgu_ref, wd_ref)
    x1 = x + 0.5 * y
    x1_ref[...] = x1
    h = _rms(x1, gmix_ref[...]).astype(BF16)

    ucb = jnp.dot(h, win_ref[:, 0:COL_Q], preferred_element_type=F32)
    ci = ucb[:, CONV_DIM:2 * CONV_DIM] * ucb[:, 0:CONV_DIM]
    if carry:
        @pl.when(pl.program_id(1) == 0)
        def _():
            cs_ref[:, 0:8, :] = jnp.zeros((nb, 8, CONV_DIM), F32)
    else:
        cs_ref[:, 8 - (CONV_K - 1):8, :] = st_ref[...]
    cs_ref[:, 8:8 + seq, :] = ci.reshape(nb, seq, CONV_DIM)
    conv = cb_ref[...] + cw_ref[CONV_K - 1:CONV_K, :] * ci
    for i in range(CONV_K - 1):
        shifted = cs_ref[:, 8 - (CONV_K - 1) + i:8 - (CONV_K - 1) + i + seq, :]
        conv = conv + cw_ref[i:i + 1, :] * shifted.reshape(tm, CONV_DIM)
    yc_ref[...] = (ucb[:, 2 * CONV_DIM:3 * CONV_DIM] * conv).astype(BF16)
    tail = cs_ref[:, seq:seq + 8, :]
    nc_ref[...] = tail[:, 8 - (CONV_K - 1):, :].reshape(nc_ref.shape)
    if carry:
        cs_ref[:, 0:8, :] = tail

    for name, col, g_ref, out_ref in (("q", COL_Q, gq_ref, qa_ref), ("k", COL_K, gk_ref, ka_ref)):
        pr = jnp.dot(h, win_ref[:, col:col + FOX_DIM], preferred_element_type=F32)
        for g in range(FOX_HEADS // 2):
            sl = slice(g * PAIR, (g + 1) * PAIR)
            xn = _pair_headnorm(pr[:, sl], g_ref[:, sl])
            if name == "k":
                kf_ref[:, sl] = xn
            out_ref[:, sl] = xn.astype(BF16)
    v = jnp.dot(h, win_ref[:, COL_V:COL_V + FOX_DIM], preferred_element_type=F32)
    vf_ref[...] = v
    vb_ref[...] = v.astype(BF16)

    qm = jnp.dot(h, win_ref[:, COL_QM:COL_QM + MEM_DIM], preferred_element_type=F32)
    for g in range(MEM_HEADS // 2):
        sl = slice(g * PAIR, (g + 1) * PAIR)
        qm_ref[:, sl] = _pair_headnorm(qm[:, sl], gqm_ref[:, sl]).astype(BF16)

    fl = jnp.dot(h, win_ref[:, COL_FREP:IN_COLS_EXT], preferred_element_type=F32)
    lf_ref[...] = _log_sigmoid(fl[:, AUG:AUG + FOX_HEADS] + bf_ref[...])
    logf = _log_sigmoid(fl[:, 0:AUG] + bfr_ref[...])

    row = lax.broadcasted_iota(jnp.int32, (tm, tm), 0)
    col = lax.broadcasted_iota(jnp.int32, (tm, tm), 1)
    tri = col <= row
    if nb > 1:
        tri = jnp.logical_and(tri, _idiv(row, seq) == _idiv(col, seq))
    tri = jnp.where(tri, 1.0, 0.0).astype(BF16)
    hi, mid, lo = _split3(logf)
    parts = jnp.concatenate([hi.astype(BF16), mid.astype(BF16), lo.astype(BF16)], axis=1)
    cs = jnp.dot(tri, parts, preferred_element_type=F32)
    cum = cs[:, 0:AUG] + cs[:, AUG:2 * AUG] + cs[:, 2 * AUG:3 * AUG]
    if carry:
        @pl.when(pl.program_id(1) == 0)
        def _():
            cum_ref[...] = jnp.zeros(cum_ref.shape, F32)
        cum = cum + cum_ref[0:1, :]
        cum_ref[0:1, :] = cum[tm - 1:tm, :]

    j = _imod(lax.broadcasted_iota(jnp.int32, (1, AUG), 1), AUG_GROUP)
    chi, cmid, clo = _split3(cum)
    aq = jnp.where(j == 0, chi, jnp.where(j == 1, cmid, jnp.where(j == 2, clo, jnp.where(j < 6, 1.0, 0.0))))
    ak = jnp.where(j < 3, 1.0, jnp.where(j == 3, -chi, jnp.where(j == 4, -cmid, jnp.where(j == 5, -clo, 0.0))))
    qa_ref[:, FOX_DIM:QK_COLS] = aq.astype(BF16)
    ka_ref[:, FOX_DIM:QK_COLS] = ak.astype(BF16)


def _stage1(x3, state, wts, *, nb, seq, carry):
    G, T, _ = x3.shape
    tm = nb * seq
    nt = T // tm
    const = lambda shape: pl.BlockSpec(shape, lambda g, t: (0,) * len(shape), pipeline_mode=pl.Buffered(1))
    tok = lambda cols: pl.BlockSpec((None, tm, cols), lambda g, t: (g, t, 0))
    in_specs = [tok(D_MODEL)]
    args = [x3]
    if not carry:
        in_specs.append(pl.BlockSpec((nb, CONV_K - 1, CONV_DIM), lambda g, t: (t, 0, 0)))
        args.append(state)
    names = ("g1", "wgu1", "wd1", "gmix", "win", "conv_w", "conv_b", "gq", "gk", "gqm", "bf_rep", "bf")
    for n in names:
        in_specs.append(const(wts[n].shape))
        args.append(wts[n])
    shp = lambda cols, dt: jax.ShapeDtypeStruct((G, T, cols), dt)
    out_shape = [shp(D_MODEL, F32), shp(CONV_DIM, BF16), shp(QK_COLS, BF16), shp(QK_COLS, BF16),
                 shp(FOX_DIM, BF16), shp(FOX_DIM, F32), shp(FOX_DIM, F32), shp(FOX_HEADS, F32),
                 shp(MEM_DIM, BF16)]
    out_specs = [tok(D_MODEL), tok(CONV_DIM), tok(QK_COLS), tok(QK_COLS), tok(FOX_DIM), tok(FOX_DIM),
                 tok(FOX_DIM), tok(FOX_HEADS), tok(MEM_DIM)]
    if carry:
        out_shape.append(jax.ShapeDtypeStruct((G, CONV_K - 1, CONV_DIM), F32))
        out_specs.append(pl.BlockSpec((None, CONV_K - 1, CONV_DIM), lambda g, t: (g, 0, 0)))
    else:
        out_shape.append(jax.ShapeDtypeStruct((nt * nb, CONV_K - 1, CONV_DIM), F32))
        out_specs.append(pl.BlockSpec((nb, CONV_K - 1, CONV_DIM), lambda g, t: (t, 0, 0)))
    return pl.pallas_call(
        functools.partial(_stage1_body, nb, seq, carry),
        grid=(G, nt),
        in_specs=in_specs,
        out_specs=out_specs,
        out_shape=out_shape,
        scratch_shapes=[pltpu.VMEM((nb, seq + 8, CONV_DIM), F32), pltpu.VMEM((8, AUG), F32)],
        compiler_params=pltpu.CompilerParams(
            dimension_semantics=("arbitrary", "arbitrary"), vmem_limit_bytes=VMEM_LIMIT),
        name="stage1_carry" if carry else "stage1_batched",
    )(*args)


def _stage3_body(x1_ref, yc_ref, ya_ref, wo_ref, g2_ref, wgu_ref, wd_ref, gfin_ref, y_ref):
    x2 = (x1_ref[...]
          + jnp.dot(yc_ref[...], wo_ref[0:CONV_DIM, :], preferred_element_type=F32)
          + jnp.dot(ya_ref[...], wo_ref[CONV_DIM:, :], preferred_element_type=F32))
    y = _ffn(_rms(x2, g2_ref[...]).astype(BF16), wgu_ref, wd_ref)
    y_ref[...] = _rms(x2 + 0.5 * y, gfin_ref[...])


def _stage3(x1, yc, ya, wts, *, tm):
    G, T, _ = x1.shape
    const = lambda shape: pl.BlockSpec(shape, lambda g, t: (0,) * len(shape), pipeline_mode=pl.Buffered(1))
    tok = lambda cols: pl.BlockSpec((None, tm, cols), lambda g, t: (g, t, 0))
    names = ("wo", "g2", "wgu2", "wd2", "gfin")
    return pl.pallas_call(
        _stage3_body,
        grid=(G, T // tm),
        in_specs=[tok(D_MODEL), tok(CONV_DIM), tok(ATT_COLS)] + [const(wts[n].shape) for n in names],
        out_specs=tok(D_MODEL),
        out_shape=jax.ShapeDtypeStruct((G, T, D_MODEL), F32),
        compiler_params=pltpu.CompilerParams(
            dimension_semantics=("arbitrary", "arbitrary"), vmem_limit_bytes=VMEM_LIMIT),
        name="stage3",
    )(x1, yc, ya, *[wts[n] for n in names])


def _memkv_body(mem_ref, gmem_ref, w_ref, gk_ref, mk_ref, mv_ref, mkb_ref, mvb_ref):
    kv = jnp.dot(_rms(mem_ref[...], gmem_ref[...]).astype(BF16), w_ref[...], preferred_element_type=F32)
    for g in range(MEM_HEADS // 2):
        sl = slice(g * PAIR, (g + 1) * PAIR)
        mk = _pair_headnorm(kv[:, sl], gk_ref[:, sl])
        mk_ref[:, sl] = mk
        mkb_ref[:, sl] = mk.astype(BF16)
    mv = kv[:, MEM_DIM:]
    mv_ref[...] = mv
    mvb_ref[...] = mv.astype(BF16)


def _memkv(mem, wts):
    B = mem.shape[0]
    const = lambda shape: pl.BlockSpec(shape, lambda b: (0,) * len(shape))
    blk = lambda cols: pl.BlockSpec((None, N_MEM, cols), lambda b: (b, 0, 0))
    shp = lambda dt: jax.ShapeDtypeStruct((B, N_MEM, MEM_DIM), dt)
    names = ("gmem", "wmem", "gkm")
    return pl.pallas_call(
        _memkv_body,
        grid=(B,),
        in_specs=[blk(D_MODEL)] + [const(wts[n].shape) for n in names],
        out_specs=[blk(MEM_DIM)] * 4,
        out_shape=[shp(F32), shp(F32), shp(BF16), shp(BF16)],
        compiler_params=pltpu.CompilerParams(dimension_semantics=("arbitrary",)),
        name="memkv",
    )(mem, *[wts[n] for n in names])


def _head_masks():
    lane = lax.broadcasted_iota(jnp.int32, (1, LANES), 1)
    lo = lane < HEAD_DIM
    return lo, _idiv(lane, AUG_GROUP)


def _prompt_att_body(tq, tk, iq_ref, ik_ref, qa_ref, ka_ref, vb_ref, qm_ref, mk_ref, mv_ref,
                     o_ref, m_ref, l_ref, acc_ref):
    p = pl.program_id(1)
    iq = iq_ref[p]
    ik = ik_ref[p]
    lo, aug_head = _head_masks()

    @pl.when(ik == 0)
    def _():
        m_ref[...] = jnp.full(m_ref.shape, NEG, F32)
        l_ref[...] = jnp.zeros(l_ref.shape, F32)
        acc_ref[...] = jnp.zeros(acc_ref.shape, F32)

    def step(masked):
        if masked:
            qpos = iq * tq + lax.broadcasted_iota(jnp.int32, (tq, tk), 0)
            kpos = ik * tk + lax.broadcasted_iota(jnp.int32, (tq, tk), 1)
            visible = kpos <= qpos
        q_aug = qa_ref[:, FOX_DIM:QK_COLS]
        k_aug = ka_ref[:, FOX_DIM:QK_COLS]
        for h in range(FOX_HEADS):
            sl = slice((h // 2) * PAIR, (h // 2 + 1) * PAIR)
            mine = lo if h % 2 == 0 else jnp.logical_not(lo)
            qh = jnp.concatenate([jnp.where(mine, qa_ref[:, sl], 0.0).astype(BF16),
                                  jnp.where(aug_head == h, q_aug, 0.0).astype(BF16)], axis=1)
            kh = jnp.concatenate([ka_ref[:, sl], k_aug], axis=1)
            s = _dot_nt(qh, kh)
            if masked:
                s = jnp.where(visible, s, NEG)
            m_prev = m_ref[h]
            m_new = jnp.maximum(m_prev, jnp.max(s, axis=-1, keepdims=True))
            alpha = jnp.exp(m_prev - m_new)
            pexp = jnp.exp(s - m_new)
            l_ref[h] = alpha * l_ref[h] + jnp.sum(pexp, axis=-1, keepdims=True)
            acc_ref[h] = alpha * acc_ref[h] + jnp.dot(pexp.astype(BF16), vb_ref[:, sl],
                                                      preferred_element_type=F32)
            m_ref[h] = m_new

    diag = (ik + 1) * tk - 1 > iq * tq

    @pl.when(diag)
    def _():
        step(True)

    @pl.when(jnp.logical_not(diag))
    def _():
        step(False)

    @pl.when((ik + 1) * tk >= (iq + 1) * tq)
    def _():
        for g in range(FOX_HEADS // 2):
            even = acc_ref[2 * g] / l_ref[2 * g]
            odd = acc_ref[2 * g + 1] / l_ref[2 * g + 1]
            o_ref[:, g * PAIR:(g + 1) * PAIR] = jnp.where(lo, even, odd).astype(BF16)
        for g in range(MEM_HEADS // 2):
            sl = slice(g * PAIR, (g + 1) * PAIR)
            outs = []
            for mine in (lo, jnp.logical_not(lo)):
                qh = jnp.where(mine, qm_ref[:, sl], 0.0).astype(BF16)
                pexp, lsum = _softmax_rows(_dot_nt(qh, mk_ref[:, sl]) * SCALE)
                outs.append(jnp.dot(pexp.astype(BF16), mv_ref[:, sl], preferred_element_type=F32) / lsum)
            o_ref[:, FOX_DIM + g * PAIR:FOX_DIM + (g + 1) * PAIR] = jnp.where(lo, outs[0], outs[1]).astype(BF16)


def _prompt_attention(qa, ka, vb, qm, mkb, mvb, *, tq, tk):
    B, T, _ = qa.shape
    pairs = [(i, j) for i in range(T // tq) for j in range(T // tk) if j * tk <= i * tq + tq - 1]
    iq_tab = jnp.asarray([p[0] for p in pairs], jnp.int32)
    ik_tab = jnp.asarray([p[1] for p in pairs], jnp.int32)
    grid_spec = pltpu.PrefetchScalarGridSpec(
        num_scalar_prefetch=2,
        grid=(B, len(pairs)),
        in_specs=[
            pl.BlockSpec((None, tq, QK_COLS), lambda b, p, iq, ik: (b, iq[p], 0)),
            pl.BlockSpec((None, tk, QK_COLS), lambda b, p, iq, ik: (b, ik[p], 0)),
            pl.BlockSpec((None, tk, FOX_DIM), lambda b, p, iq, ik: (b, ik[p], 0)),
            pl.BlockSpec((None, tq, MEM_DIM), lambda b, p, iq, ik: (b, iq[p], 0)),
            pl.BlockSpec((None, N_MEM, MEM_DIM), lambda b, p, iq, ik: (b, 0, 0)),
            pl.BlockSpec((None, N_MEM, MEM_DIM), lambda b, p, iq, ik: (b, 0, 0)),
        ],
        out_specs=pl.BlockSpec((None, tq, ATT_COLS), lambda b, p, iq, ik: (b, iq[p], 0)),
        scratch_shapes=[pltpu.VMEM((FOX_HEADS, tq, 1), F32), pltpu.VMEM((FOX_HEADS, tq, 1), F32),
                        pltpu.VMEM((FOX_HEADS, tq, PAIR), F32)],
    )
    return pl.pallas_call(
        functools.partial(_prompt_att_body, tq, tk),
        grid_spec=grid_spec,
        out_shape=jax.ShapeDtypeStruct((B, T, ATT_COLS), BF16),
        compiler_params=pltpu.CompilerParams(
            dimension_semantics=("arbitrary", "arbitrary"), vmem_limit_bytes=VMEM_LIMIT),
        name="prompt_attention",
    )(iq_tab, ik_tab, qa, ka, vb, qm, mkb, mvb)


def _diag_blocks(x, nh, rows, width):
    head = _idiv(lax.broadcasted_iota(jnp.int32, (1, nh * width), 1), width)
    out = None
    for h in range(nh):
        part = jnp.where(head == h, x[h * rows:(h + 1) * rows, :], 0.0)
        out = part if out is None else out + part
    return out


def _sample_att_body(seq, past, qa_ref, ka_ref, vb_ref, qm_ref, pk_ref, pv_ref, plf_ref, mk_ref, mv_ref, o_ref):
    blk = LANES
    row = lax.broadcasted_iota(jnp.int32, (blk, blk), 0)
    col = lax.broadcasted_iota(jnp.int32, (blk, blk), 1)
    upper = jnp.where(col > row, 1.0, 0.0).astype(BF16)
    j = _imod(lax.broadcasted_iota(jnp.int32, (1, AUG), 1), AUG_GROUP)
    running = jnp.zeros((1, AUG), F32)
    kaug = [None] * (past // blk)
    for b in reversed(range(past // blk)):
        vals = plf_ref[b * blk:(b + 1) * blk, :]
        hi, mid, lo = _split3(vals)
        parts = jnp.concatenate([hi.astype(BF16), mid.astype(BF16), lo.astype(BF16)], axis=1)
        loc = jnp.dot(upper, parts, preferred_element_type=F32)
        loc = loc[:, 0:AUG] + loc[:, AUG:2 * AUG] + loc[:, 2 * AUG:]
        suffix = loc + running
        running = running + loc[0:1, :] + vals[0:1, :]
        rhi, rmid, rlo = _split3(suffix)
        kaug[b] = jnp.where(j < 3, 1.0, jnp.where(j == 3, rhi, jnp.where(j == 4, rmid, jnp.where(j == 5, rlo, 0.0)))
                            ).astype(BF16)
    k_past = jnp.concatenate([pk_ref[...].astype(BF16), jnp.concatenate(kaug, axis=0)], axis=1)

    nrow = FOX_HEADS * seq
    lane = lax.broadcasted_iota(jnp.int32, (nrow, QK_COLS), 1)
    lane_head = jnp.where(lane < FOX_DIM, _idiv(lane, HEAD_DIM), _idiv(lane - FOX_DIM, AUG_GROUP))
    row_head = _idiv(lax.broadcasted_iota(jnp.int32, (nrow, QK_COLS), 0), seq)
    qbd = jnp.where(lane_head == row_head, jnp.concatenate([qa_ref[...]] * FOX_HEADS, axis=0), 0.0).astype(BF16)
    s_past = _dot_nt(qbd, k_past)
    s_new = _dot_nt(qbd, ka_ref[...])
    qi = _imod(lax.broadcasted_iota(jnp.int32, (nrow, seq), 0), seq)
    kj = lax.broadcasted_iota(jnp.int32, (nrow, seq), 1)
    s_new = jnp.where(kj <= qi, s_new, NEG)
    m = jnp.maximum(jnp.max(s_past, axis=-1, keepdims=True), jnp.max(s_new, axis=-1, keepdims=True))
    p_past = jnp.exp(s_past - m)
    p_new = jnp.exp(s_new - m)
    lsum = jnp.sum(p_past, axis=-1, keepdims=True) + jnp.sum(p_new, axis=-1, keepdims=True)
    o = (jnp.dot(p_past.astype(BF16), pv_ref[...].astype(BF16), preferred_element_type=F32)
         + jnp.dot(p_new.astype(BF16), vb_ref[...], preferred_element_type=F32)) / lsum
    o_ref[:, 0:FOX_DIM] = _diag_blocks(o, FOX_HEADS, seq, HEAD_DIM).astype(BF16)

    nrow_m = MEM_HEADS * seq
    lane_m = _idiv(lax.broadcasted_iota(jnp.int32, (nrow_m, MEM_DIM), 1), HEAD_DIM)
    row_m = _idiv(lax.broadcasted_iota(jnp.int32, (nrow_m, MEM_DIM), 0), seq)
    qbd_m = jnp.where(lane_m == row_m, jnp.concatenate([qm_ref[...]] * MEM_HEADS, axis=0), 0.0).astype(BF16)
    pm, lm = _softmax_rows(_dot_nt(qbd_m, mk_ref[...].astype(BF16)) * SCALE)
    om = jnp.dot(pm.astype(BF16), mv_ref[...].astype(BF16), preferred_element_type=F32) / lm
    o_ref[:, FOX_DIM:ATT_COLS] = _diag_blocks(om, MEM_HEADS, seq, HEAD_DIM).astype(BF16)


def _sample_attention(qa, ka, vb, qm, pk, pv, plf_rep, mk, mv):
    nbatch, seq, _ = qa.shape
    past = pk.shape[1]
    blk = lambda rows, cols: pl.BlockSpec((None, rows, cols), lambda b: (b, 0, 0))
    return pl.pallas_call(
        functools.partial(_sample_att_body, seq, past),
        grid=(nbatch,),
        in_specs=[blk(seq, QK_COLS), blk(seq, QK_COLS), blk(seq, FOX_DIM), blk(seq, MEM_DIM),
                  blk(past, FOX_DIM), blk(past, FOX_DIM), blk(past, AUG),
                  blk(N_MEM, MEM_DIM), blk(N_MEM, MEM_DIM)],
        out_specs=blk(seq, ATT_COLS),
        out_shape=jax.ShapeDtypeStruct((nbatch, seq, ATT_COLS), BF16),
        compiler_params=pltpu.CompilerParams(
            dimension_semantics=("arbitrary",), vmem_limit_bytes=VMEM_LIMIT),
        name="sample_attention",
    )(qa, ka, vb, qm, pk, pv, plf_rep, mk, mv)


def _prep_weights(norm_ffn1, w_g1, w_u1, w_d1, norm_mix, w_in, b_forget, conv_w, conv_b,
                  q_norm_fox, k_norm_fox, q_norm_mem, k_norm_mem, norm_mem, w_mem_kv, w_out,
                  norm_ffn2, w_g2, w_u2, w_d2, norm_final):
    def gate_up(wg, wu):
        wg = wg.reshape(D_MODEL, N_FF_CHUNKS, FF_CHUNK)
        wu = wu.reshape(D_MODEL, N_FF_CHUNKS, FF_CHUNK)
        return jnp.concatenate([wg, wu], axis=2).transpose(1, 0, 2).astype(BF16)

    w_f = w_in[:, COL_QM:COL_QM + FOX_HEADS]
    w_qm = w_in[:, COL_QM + FOX_HEADS:]
    win = jnp.concatenate(
        [w_in[:, :COL_QM], w_qm, jnp.repeat(w_f, AUG_GROUP, axis=1), w_f,
         jnp.zeros((D_MODEL, LANES - FOX_HEADS), F32)], axis=1).astype(BF16)
    row = lambda v: v.reshape(1, -1).astype(F32)
    return {
        "g1": row(norm_ffn1), "wgu1": gate_up(w_g1, w_u1), "wd1": w_d1.reshape(N_FF_CHUNKS, FF_CHUNK, D_MODEL).astype(BF16),
        "gmix": row(norm_mix), "win": win, "conv_w": conv_w.astype(F32), "conv_b": row(conv_b),
        "gq": row(jnp.tile(q_norm_fox, FOX_HEADS)) * SCALE, "gk": row(jnp.tile(k_norm_fox, FOX_HEADS)),
        "gqm": row(jnp.tile(q_norm_mem, MEM_HEADS)),
        "bf_rep": row(jnp.repeat(b_forget, AUG_GROUP)), "bf": row(b_forget),
        "gmem": row(norm_mem), "wmem": w_mem_kv.astype(BF16), "gkm": row(jnp.tile(k_norm_mem, MEM_HEADS)),
        "wo": w_out.astype(BF16), "g2": row(norm_ffn2), "wgu2": gate_up(w_g2, w_u2),
        "wd2": w_d2.reshape(N_FF_CHUNKS, FF_CHUNK, D_MODEL).astype(BF16), "gfin": row(norm_final),
    }


def kernel(x_prompt, x_sample, cache_fox_k, cache_fox_v, cache_fox_logf, state_conv, cache_mem_k, cache_mem_v, mem_prompt, norm_ffn1, w_ffn1_gate, w_ffn1_up, w_ffn1_down, norm_mix, w_in, b_forget, conv_w, conv_b, q_norm_fox, k_norm_fox, q_norm_mem, k_norm_mem, norm_mem, w_mem_kv, w_out, norm_ffn2, w_ffn2_gate, w_ffn2_up, w_ffn2_down, norm_final):
    depth = w_in.shape[0]
    assert depth == 1, "single-layer step"
    B, T, _ = x_prompt.shape
    nbs, seq_s, _ = x_sample.shape
    past = cache_fox_k.shape[2]
    l = 0
    wts = _prep_weights(norm_ffn1[l], w_ffn1_gate[l], w_ffn1_up[l], w_ffn1_down[l], norm_mix[l], w_in[l],
                        b_forget[l], conv_w[l], conv_b[l], q_norm_fox[l], k_norm_fox[l], q_norm_mem[l],
                        k_norm_mem[l], norm_mem[l], w_mem_kv[l], w_out[l], norm_ffn2[l], w_ffn2_gate[l],
                        w_ffn2_up[l], w_ffn2_down[l], norm_final[l])
    tile = 512

    mk_p, mv_p, mkb, mvb = _memkv(mem_prompt, wts)
    x1, yc, qa, ka, vb, kf, vf, lf, qm, nc = _stage1(x_prompt, None, wts, nb=1, seq=tile, carry=True)
    ya = _prompt_attention(qa, ka, vb, qm, mkb, mvb, tq=tile, tk=tile)
    y_prompt = _stage3(x1, yc, ya, wts, tm=tile)

    xs = x_sample.reshape(1, nbs * seq_s, D_MODEL)
    x1s, ycs, qas, kas, vbs, kfs, vfs, lfs, qms, ncs = _stage1(
        xs, state_conv[l], wts, nb=tile // seq_s, seq=seq_s, carry=False)
    per_seq = lambda a: a.reshape(nbs, seq_s, a.shape[-1])
    yas = _sample_attention(
        per_seq(qas), per_seq(kas), per_seq(vbs), per_seq(qms),
        cache_fox_k[l].reshape(nbs, past, FOX_DIM), cache_fox_v[l].reshape(nbs, past, FOX_DIM),
        jnp.repeat(cache_fox_logf[l], AUG_GROUP, axis=-1),
        cache_mem_k[l].reshape(nbs, N_MEM, MEM_DIM), cache_mem_v[l].reshape(nbs, N_MEM, MEM_DIM))
    y_sample = _stage3(x1s, ycs, yas.reshape(1, nbs * seq_s, ATT_COLS), wts, tm=tile)

    heads = lambda a, b, t, nh: a.reshape(1, b, t, nh, HEAD_DIM)
    return (y_prompt, y_sample.reshape(nbs, seq_s, D_MODEL),
            heads(kf, B, T, FOX_HEADS), heads(vf, B, T, FOX_HEADS), lf.reshape(1, B, T, FOX_HEADS),
            nc.reshape(1, B, CONV_K - 1, CONV_DIM),
            heads(mk_p, B, N_MEM, MEM_HEADS), heads(mv_p, B, N_MEM, MEM_HEADS),
            heads(kfs, nbs, seq_s, FOX_HEADS), heads(vfs, nbs, seq_s, FOX_HEADS),
            lfs.reshape(1, nbs, seq_s, FOX_HEADS), ncs.reshape(1, nbs, CONV_K - 1, CONV_DIM))
```

```python
import functools

import jax
import jax.numpy as jnp
from jax import lax
from jax.experimental import pallas as pl
from jax.experimental.pallas import tpu as pltpu

F32 = jnp.float32
BF16 = jnp.bfloat16

D_MODEL = 1024
HEAD_DIM = 64
CONV_DIM = 256
CONV_K = 3
FOX_HEADS = 8
FOX_DIM = FOX_HEADS * HEAD_DIM
MEM_HEADS = 4
MEM_DIM = MEM_HEADS * HEAD_DIM
N_MEM = 256
D_FF = 2816
EPS = 1e-6

LANES = 128
PAIR = 2 * HEAD_DIM
FF_CHUNK = 256
N_FF_CHUNKS = D_FF // FF_CHUNK
AUG = LANES
AUG_GROUP = AUG // FOX_HEADS
QK_COLS = FOX_DIM + AUG
ATT_COLS = FOX_DIM + MEM_DIM
COL_Q = 3 * CONV_DIM
COL_K = COL_Q + FOX_DIM
COL_V = COL_K + FOX_DIM
COL_QM = COL_V + FOX_DIM
COL_FREP = COL_QM + MEM_DIM
COL_F = COL_FREP + AUG
IN_COLS_EXT = COL_F + LANES
NEG = -1e30
SCALE = HEAD_DIM ** -0.5
LOG2E = 1.4426950408889634
VMEM_LIMIT = 56 * 1024 * 1024


def _idiv(x, d):
    if d & (d - 1) == 0:
        return lax.shift_right_logical(x, d.bit_length() - 1)
    return x // d


def _imod(x, d):
    if d & (d - 1) == 0:
        return x & (d - 1)
    return x % d


def _rms(x, g):
    ms = jnp.mean(x * x, axis=-1, keepdims=True)
    return x * lax.rsqrt(ms + EPS) * g


def _pair_headnorm(x, g):
    lo = lax.broadcasted_iota(jnp.int32, (1, PAIR), 1) < HEAD_DIM
    x2 = x * x
    s_lo = jnp.sum(jnp.where(lo, x2, 0.0), axis=-1, keepdims=True)
    s_hi = jnp.sum(jnp.where(lo, 0.0, x2), axis=-1, keepdims=True)
    r = jnp.where(lo, lax.rsqrt(s_lo * (1.0 / HEAD_DIM) + EPS),
                  lax.rsqrt(s_hi * (1.0 / HEAD_DIM) + EPS))
    return x * r * g


def _split3(x):
    hi = x.astype(BF16).astype(F32)
    r = x - hi
    mid = r.astype(BF16).astype(F32)
    lo = (r - mid).astype(BF16).astype(F32)
    return hi, mid, lo


def _ffn(xn, wgu_ref, wd_ref):
    acc = None
    for c in range(N_FF_CHUNKS):
        gu = jnp.dot(xn, wgu_ref[c], preferred_element_type=F32)
        g = gu[:, :FF_CHUNK]
        u = gu[:, FF_CHUNK:]
        a = (g * jax.nn.sigmoid(g) * u).astype(BF16)
        d = jnp.dot(a, wd_ref[c], preferred_element_type=F32)
        acc = d if acc is None else acc + d
    return acc


def _log_sigmoid(x):
    return jnp.minimum(x, 0.0) - jnp.log1p(jnp.exp(-jnp.abs(x)))


def _softmax_rows(s):
    m = jnp.max(s, axis=-1, keepdims=True)
    p = jnp.exp(s - m)
    return p, jnp.sum(p, axis=-1, keepdims=True)


def _dot_nt(a, b):
    return lax.dot_general(a, b, (((1,), (1,)), ((), ())), preferred_element_type=F32)


def _stage1_body(nb, seq, carry, *refs):
    tm = nb * seq
    if carry:
        x_ref, rest = refs[0], refs[1:]
        st_ref = None
    else:
        x_ref, st_ref, rest = refs[0], refs[1], refs[2:]
    (g1_ref, wgu_ref, wd_ref, gmix_ref, win_ref, cw_ref, cb_ref, gq_ref, gk_ref, gqm_ref,
     bfr_ref, bf_ref,
     x1_ref, yc_ref, qa_ref, ka_ref, vb_ref, kf_ref, vf_ref, lf_ref, qm_ref, nc_ref,
     cs_ref, cum_ref) = rest

    x = x_ref[...]
    y = _ffn(_rms(x, g1_ref[...]).astype(BF16), wgu_ref, wd_ref)
    x1 = x + 0.5 * y
    x1_ref[...] = x1
    h = _rms(x1, gmix_ref[...]).astype(BF16)

    ucb = jnp.dot(h, win_ref[:, 0:COL_Q], preferred_element_type=F32)
    ci = ucb[:, CONV_DIM:2 * CONV_DIM] * ucb[:, 0:CONV_DIM]
    if carry:
        @pl.when(pl.program_id(1) == 0)
        def _():
            cs_ref[:, 0:8, :] = jnp.zeros((nb, 8, CONV_DIM), F32)
    else:
        cs_ref[:, 8 - (CONV_K - 1):8, :] = st_ref[...]
    cs_ref[:, 8:8 + seq, :] = ci.reshape(nb, seq, CONV_DIM)
    conv = cb_ref[...] + cw_ref[CONV_K - 1:CONV_K, :] * ci
    for i in range(CONV_K - 1):
        shifted = cs_ref[:, 8 - (CONV_K - 1) + i:8 - (CONV_K - 1) + i + seq, :]
        conv = conv + cw_ref[i:i + 1, :] * shifted.reshape(tm, CONV_DIM)
    yc_ref[...] = (ucb[:, 2 * CONV_DIM:3 * CONV_DIM] * conv).astype(BF16)
    tail = cs_ref[:, seq:seq + 8, :]
    nc_ref[...] = tail[:, 8 - (CONV_K - 1):, :].reshape(nc_ref.shape)
    if carry:
        cs_ref[:, 0:8, :] = tail

    for name, col, g_ref, out_ref in (("q", COL_Q, gq_ref, qa_ref), ("k", COL_K, gk_ref, ka_ref)):
        pr = jnp.dot(h, win_ref[:, col:col + FOX_DIM], preferred_element_type=F32)
        for g in range(FOX_HEADS // 2):
            sl = slice(g * PAIR, (g + 1) * PAIR)
            xn = _pair_headnorm(pr[:, sl], g_ref[:, sl])
            if name == "k":
                kf_ref[:, sl] = xn
            out_ref[:, sl] = xn.astype(BF16)
    v = jnp.dot(h, win_ref[:, COL_V:COL_V + FOX_DIM], preferred_element_type=F32)
    vf_ref[...] = v
    vb_ref[...] = v.astype(BF16)

    qm = jnp.dot(h, win_ref[:, COL_QM:COL_QM + MEM_DIM], preferred_element_type=F32)
    for g in range(MEM_HEADS // 2):
        sl = slice(g * PAIR, (g + 1) * PAIR)
        qm_ref[:, sl] = _pair_headnorm(qm[:, sl], gqm_ref[:, sl]).astype(BF16)

    fl = jnp.dot(h, win_ref[:, COL_FREP:IN_COLS_EXT], preferred_element_type=F32)
    lf_ref[...] = _log_sigmoid(fl[:, AUG:AUG + FOX_HEADS] + bf_ref[...])
    logf = _log_sigmoid(fl[:, 0:AUG] + bfr_ref[...])

    row = lax.broadcasted_iota(jnp.int32, (tm, tm), 0)
    col = lax.broadcasted_iota(jnp.int32, (tm, tm), 1)
    tri = col <= row
    if nb > 1:
        tri = jnp.logical_and(tri, _idiv(row, seq) == _idiv(col, seq))
    tri = jnp.where(tri, 1.0, 0.0).astype(BF16)
    hi, mid, lo = _split3(logf)
    parts = jnp.concatenate([hi.astype(BF16), mid.astype(BF16), lo.astype(BF16)], axis=1)
    cs = jnp.dot(tri, parts, preferred_element_type=F32)
    cum = cs[:, 0:AUG] + cs[:, AUG:2 * AUG] + cs[:, 2 * AUG:3 * AUG]
    if carry:
        @pl.when(pl.program_id(1) == 0)
        def _():
            cum_ref[...] = jnp.zeros(cum_ref.shape, F32)
        cum = cum + cum_ref[0:1, :]
        cum_ref[0:1, :] = cum[tm - 1:tm, :]

    j = _imod(lax.broadcasted_iota(jnp.int32, (1, AUG), 1), AUG_GROUP)
    chi, cmid, clo = _split3(cum * LOG2E)
    aq = jnp.where(j == 0, chi, jnp.where(j == 1, cmid, jnp.where(j == 2, clo, jnp.where(j < 6, 1.0, 0.0))))
    ak = jnp.where(j < 3, 1.0, jnp.where(j == 3, -chi, jnp.where(j == 4, -cmid, jnp.where(j == 5, -clo, 0.0))))
    qa_ref[:, FOX_DIM:QK_COLS] = aq.astype(BF16)
    ka_ref[:, FOX_DIM:QK_COLS] = ak.astype(BF16)


def _stage1(x3, state, wts, *, nb, seq, carry):
    G, T, _ = x3.shape
    tm = nb * seq
    nt = T // tm
    const = lambda shape: pl.BlockSpec(shape, lambda g, t: (0,) * len(shape), pipeline_mode=pl.Buffered(1))
    tok = lambda cols: pl.BlockSpec((None, tm, cols), lambda g, t: (g, t, 0))
    in_specs = [tok(D_MODEL)]
    args = [x3]
    if not carry:
        in_specs.append(pl.BlockSpec((nb, CONV_K - 1, CONV_DIM), lambda g, t: (t, 0, 0)))
        args.append(state)
    names = ("g1", "wgu1", "wd1", "gmix", "win", "conv_w", "conv_b", "gq", "gk", "gqm", "bf_rep", "bf")
    for n in names:
        in_specs.append(const(wts[n].shape))
        args.append(wts[n])
    shp = lambda cols, dt: jax.ShapeDtypeStruct((G, T, cols), dt)
    out_shape = [shp(D_MODEL, F32), shp(CONV_DIM, BF16), shp(QK_COLS, BF16), shp(QK_COLS, BF16),
                 shp(FOX_DIM, BF16), shp(FOX_DIM, F32), shp(FOX_DIM, F32), shp(FOX_HEADS, F32),
                 shp(MEM_DIM, BF16)]
    out_specs = [tok(D_MODEL), tok(CONV_DIM), tok(QK_COLS), tok(QK_COLS), tok(FOX_DIM), tok(FOX_DIM),
                 tok(FOX_DIM), tok(FOX_HEADS), tok(MEM_DIM)]
    if carry:
        out_shape.append(jax.ShapeDtypeStruct((G, CONV_K - 1, CONV_DIM), F32))
        out_specs.append(pl.BlockSpec((None, CONV_K - 1, CONV_DIM), lambda g, t: (g, 0, 0)))
    else:
        out_shape.append(jax.ShapeDtypeStruct((nt * nb, CONV_K - 1, CONV_DIM), F32))
        out_specs.append(pl.BlockSpec((nb, CONV_K - 1, CONV_DIM), lambda g, t: (t, 0, 0)))
    return pl.pallas_call(
        functools.partial(_stage1_body, nb, seq, carry),
        grid=(G, nt),
        in_specs=in_specs,
        out_specs=out_specs,
        out_shape=out_shape,
        scratch_shapes=[pltpu.VMEM((nb, seq + 8, CONV_DIM), F32), pltpu.VMEM((8, AUG), F32)],
        compiler_params=pltpu.CompilerParams(
            dimension_semantics=("arbitrary", "arbitrary"), vmem_limit_bytes=VMEM_LIMIT),
        name="stage1_carry" if carry else "stage1_batched",
    )(*args)


def _stage3_body(x1_ref, yc_ref, ya_ref, wo_ref, g2_ref, wgu_ref, wd_ref, gfin_ref, y_ref):
    x2 = (x1_ref[...]
          + jnp.dot(yc_ref[...], wo_ref[0:CONV_DIM, :], preferred_element_type=F32)
          + jnp.dot(ya_ref[...], wo_ref[CONV_DIM:, :], preferred_element_type=F32))
    y = _ffn(_rms(x2, g2_ref[...]).astype(BF16), wgu_ref, wd_ref)
    y_ref[...] = _rms(x2 + 0.5 * y, gfin_ref[...])


def _stage3(x1, yc, ya, wts, *, tm):
    G, T, _ = x1.shape
    const = lambda shape: pl.BlockSpec(shape, lambda g, t: (0,) * len(shape), pipeline_mode=pl.Buffered(1))
    tok = lambda cols: pl.BlockSpec((None, tm, cols), lambda g, t: (g, t, 0))
    names = ("wo", "g2", "wgu2", "wd2", "gfin")
    return pl.pallas_call(
        _stage3_body,
        grid=(G, T // tm),
        in_specs=[tok(D_MODEL), tok(CONV_DIM), tok(ATT_COLS)] + [const(wts[n].shape) for n in names],
        out_specs=tok(D_MODEL),
        out_shape=jax.ShapeDtypeStruct((G, T, D_MODEL), F32),
        compiler_params=pltpu.CompilerParams(
            dimension_semantics=("arbitrary", "arbitrary"), vmem_limit_bytes=VMEM_LIMIT),
        name="stage3",
    )(x1, yc, ya, *[wts[n] for n in names])


def _memkv_body(mem_ref, gmem_ref, w_ref, gk_ref, mk_ref, mv_ref, mkb_ref, mvb_ref):
    kv = jnp.dot(_rms(mem_ref[...], gmem_ref[...]).astype(BF16), w_ref[...], preferred_element_type=F32)
    for g in range(MEM_HEADS // 2):
        sl = slice(g * PAIR, (g + 1) * PAIR)
        mk = _pair_headnorm(kv[:, sl], gk_ref[:, sl])
        mk_ref[:, sl] = mk
        mkb_ref[:, sl] = mk.astype(BF16)
    mv = kv[:, MEM_DIM:]
    mv_ref[...] = mv
    mvb_ref[...] = mv.astype(BF16)


def _memkv(mem, wts):
    B = mem.shape[0]
    const = lambda shape: pl.BlockSpec(shape, lambda b: (0,) * len(shape))
    blk = lambda cols: pl.BlockSpec((None, N_MEM, cols), lambda b: (b, 0, 0))
    shp = lambda dt: jax.ShapeDtypeStruct((B, N_MEM, MEM_DIM), dt)
    names = ("gmem", "wmem", "gkm")
    return pl.pallas_call(
        _memkv_body,
        grid=(B,),
        in_specs=[blk(D_MODEL)] + [const(wts[n].shape) for n in names],
        out_specs=[blk(MEM_DIM)] * 4,
        out_shape=[shp(F32), shp(F32), shp(BF16), shp(BF16)],
        compiler_params=pltpu.CompilerParams(dimension_semantics=("arbitrary",)),
        name="memkv",
    )(mem, *[wts[n] for n in names])


def _head_masks():
    lane = lax.broadcasted_iota(jnp.int32, (1, LANES), 1)
    lo = lane < HEAD_DIM
    return lo, _idiv(lane, AUG_GROUP)


def _prompt_att_body(tq, tk, iq_ref, ik_ref, qa_ref, ka_ref, vb_ref, qm_ref, mk_ref, mv_ref,
                     o_ref, m_ref, acc_ref):
    p = pl.program_id(1)
    iq = iq_ref[p]
    ik = ik_ref[p]
    lo, aug_head = _head_masks()

    @pl.when(ik == 0)
    def _():
        m_ref[...] = jnp.full(m_ref.shape, NEG, F32)
        acc_ref[...] = jnp.zeros(acc_ref.shape, F32)

    def step(masked):
        if masked:
            qpos = iq * tq + lax.broadcasted_iota(jnp.int32, (tq, tk), 0)
            kpos = ik * tk + lax.broadcasted_iota(jnp.int32, (tq, tk), 1)
            visible = kpos <= qpos
        q_aug = qa_ref[:, FOX_DIM:QK_COLS]
        k_aug = ka_ref[:, FOX_DIM:QK_COLS]
        ones = jnp.ones((tk, LANES), BF16)
        for h in range(FOX_HEADS):
            sl = slice((h // 2) * PAIR, (h // 2 + 1) * PAIR)
            mine = lo if h % 2 == 0 else jnp.logical_not(lo)
            qh = jnp.concatenate([jnp.where(mine, qa_ref[:, sl], 0.0).astype(BF16),
                                  jnp.where(aug_head == h, q_aug, 0.0).astype(BF16)], axis=1)
            kh = jnp.concatenate([ka_ref[:, sl], k_aug], axis=1)
            s = _dot_nt(qh, kh)
            if masked:
                s = jnp.where(visible, s, NEG)
            blocks = [s[:, c * LANES:(c + 1) * LANES] for c in range(tk // LANES)]
            blk_max = functools.reduce(jnp.maximum, blocks)
            m_prev = m_ref[h]
            m_new = jnp.maximum(m_prev, jnp.max(blk_max, axis=-1, keepdims=True))
            alpha = jnp.exp2(m_prev - m_new)
            pexp = jnp.concatenate([jnp.exp2(b - m_new) for b in blocks], axis=1).astype(BF16)
            pv = jnp.dot(pexp, jnp.concatenate([vb_ref[:, sl], ones], axis=1), preferred_element_type=F32)
            acc_ref[h] = jnp.concatenate([alpha, alpha], axis=1) * acc_ref[h] + pv
            m_ref[h] = m_new

    diag = (ik + 1) * tk - 1 > iq * tq

    @pl.when(diag)
    def _():
        step(True)

    @pl.when(jnp.logical_not(diag))
    def _():
        step(False)

    @pl.when((ik + 1) * tk >= (iq + 1) * tq)
    def _():
        for g in range(FOX_HEADS // 2):
            even = acc_ref[2 * g, :, 0:PAIR] / acc_ref[2 * g, :, PAIR:2 * PAIR]
            odd = acc_ref[2 * g + 1, :, 0:PAIR] / acc_ref[2 * g + 1, :, PAIR:2 * PAIR]
            o_ref[:, g * PAIR:(g + 1) * PAIR] = jnp.where(lo, even, odd).astype(BF16)
        for g in range(MEM_HEADS // 2):
            sl = slice(g * PAIR, (g + 1) * PAIR)
            outs = []
            for mine in (lo, jnp.logical_not(lo)):
                qh = jnp.where(mine, qm_ref[:, sl], 0.0).astype(BF16)
                pexp, lsum = _softmax_rows(_dot_nt(qh, mk_ref[:, sl]) * SCALE)
                outs.append(jnp.dot(pexp.astype(BF16), mv_ref[:, sl], preferred_element_type=F32) / lsum)
            o_ref[:, FOX_DIM + g * PAIR:FOX_DIM + (g + 1) * PAIR] = jnp.where(lo, outs[0], outs[1]).astype(BF16)


def _prompt_attention(qa, ka, vb, qm, mkb, mvb, *, tq, tk):
    B, T, _ = qa.shape
    pairs = [(i, j) for i in range(T // tq) for j in range(T // tk) if j * tk <= i * tq + tq - 1]
    iq_tab = jnp.asarray([p[0] for p in pairs], jnp.int32)
    ik_tab = jnp.asarray([p[1] for p in pairs], jnp.int32)
    grid_spec = pltpu.PrefetchScalarGridSpec(
        num_scalar_prefetch=2,
        grid=(B, len(pairs)),
        in_specs=[
            pl.BlockSpec((None, tq, QK_COLS), lambda b, p, iq, ik: (b, iq[p], 0)),
            pl.BlockSpec((None, tk, QK_COLS), lambda b, p, iq, ik: (b, ik[p], 0)),
            pl.BlockSpec((None, tk, FOX_DIM), lambda b, p, iq, ik: (b, ik[p], 0)),
            pl.BlockSpec((None, tq, MEM_DIM), lambda b, p, iq, ik: (b, iq[p], 0)),
            pl.BlockSpec((None, N_MEM, MEM_DIM), lambda b, p, iq, ik: (b, 0, 0)),
            pl.BlockSpec((None, N_MEM, MEM_DIM), lambda b, p, iq, ik: (b, 0, 0)),
        ],
        out_specs=pl.BlockSpec((None, tq, ATT_COLS), lambda b, p, iq, ik: (b, iq[p], 0)),
        scratch_shapes=[pltpu.VMEM((FOX_HEADS, tq, LANES), F32), pltpu.VMEM((FOX_HEADS, tq, 2 * PAIR), F32)],
    )
    return pl.pallas_call(
        functools.partial(_prompt_att_body, tq, tk),
        grid_spec=grid_spec,
        out_shape=jax.ShapeDtypeStruct((B, T, ATT_COLS), BF16),
        compiler_params=pltpu.CompilerParams(
            dimension_semantics=("arbitrary", "arbitrary"), vmem_limit_bytes=VMEM_LIMIT),
        name="prompt_attention",
    )(iq_tab, ik_tab, qa, ka, vb, qm, mkb, mvb)


def _diag_blocks(x, nh, rows, width):
    head = _idiv(lax.broadcasted_iota(jnp.int32, (1, nh * width), 1), width)
    out = None
    for h in range(nh):
        part = jnp.where(head == h, x[h * rows:(h + 1) * rows, :], 0.0)
        out = part if out is None else out + part
    return out


def _sample_att_body(seq, past, qa_ref, ka_ref, vb_ref, qm_ref, pk_ref, pv_ref, plf_ref, mk_ref, mv_ref, o_ref):
    blk = LANES
    row = lax.broadcasted_iota(jnp.int32, (blk, blk), 0)
    col = lax.broadcasted_iota(jnp.int32, (blk, blk), 1)
    upper = jnp.where(col > row, 1.0, 0.0).astype(BF16)
    j = _imod(lax.broadcasted_iota(jnp.int32, (1, AUG), 1), AUG_GROUP)
    running = jnp.zeros((1, AUG), F32)
    kaug = [None] * (past // blk)
    for b in reversed(range(past // blk)):
        vals = plf_ref[b * blk:(b + 1) * blk, :]
        hi, mid, lo = _split3(vals)
        parts = jnp.concatenate([hi.astype(BF16), mid.astype(BF16), lo.astype(BF16)], axis=1)
        loc = jnp.dot(upper, parts, preferred_element_type=F32)
        loc = loc[:, 0:AUG] + loc[:, AUG:2 * AUG] + loc[:, 2 * AUG:]
        suffix = loc + running
        running = running + loc[0:1, :] + vals[0:1, :]
        rhi, rmid, rlo = _split3(suffix * LOG2E)
        kaug[b] = jnp.where(j < 3, 1.0, jnp.where(j == 3, rhi, jnp.where(j == 4, rmid, jnp.where(j == 5, rlo, 0.0)))
                            ).astype(BF16)
    k_past = jnp.concatenate([pk_ref[...].astype(BF16), jnp.concatenate(kaug, axis=0)], axis=1)

    nrow = FOX_HEADS * seq
    lane = lax.broadcasted_iota(jnp.int32, (nrow, QK_COLS), 1)
    lane_head = jnp.where(lane < FOX_DIM, _idiv(lane, HEAD_DIM), _idiv(lane - FOX_DIM, AUG_GROUP))
    row_head = _idiv(lax.broadcasted_iota(jnp.int32, (nrow, QK_COLS), 0), seq)
    qbd = jnp.where(lane_head == row_head, jnp.concatenate([qa_ref[...]] * FOX_HEADS, axis=0), 0.0).astype(BF16)
    s_past = _dot_nt(qbd, k_past)
    s_new = _dot_nt(qbd, ka_ref[...])
    qi = _imod(lax.broadcasted_iota(jnp.int32, (nrow, seq), 0), seq)
    kj = lax.broadcasted_iota(jnp.int32, (nrow, seq), 1)
    s_new = jnp.where(kj <= qi, s_new, NEG)
    m = jnp.maximum(jnp.max(s_past, axis=-1, keepdims=True), jnp.max(s_new, axis=-1, keepdims=True))
    p_past = jnp.exp2(s_past - m)
    p_new = jnp.exp2(s_new - m)
    lsum = jnp.sum(p_past, axis=-1, keepdims=True) + jnp.sum(p_new, axis=-1, keepdims=True)
    o = (jnp.dot(p_past.astype(BF16), pv_ref[...].astype(BF16), preferred_element_type=F32)
         + jnp.dot(p_new.astype(BF16), vb_ref[...], preferred_element_type=F32)) / lsum
    o_ref[:, 0:FOX_DIM] = _diag_blocks(o, FOX_HEADS, seq, HEAD_DIM).astype(BF16)

    nrow_m = MEM_HEADS * seq
    lane_m = _idiv(lax.broadcasted_iota(jnp.int32, (nrow_m, MEM_DIM), 1), HEAD_DIM)
    row_m = _idiv(lax.broadcasted_iota(jnp.int32, (nrow_m, MEM_DIM), 0), seq)
    qbd_m = jnp.where(lane_m == row_m, jnp.concatenate([qm_ref[...]] * MEM_HEADS, axis=0), 0.0).astype(BF16)
    pm, lm = _softmax_rows(_dot_nt(qbd_m, mk_ref[...].astype(BF16)) * SCALE)
    om = jnp.dot(pm.astype(BF16), mv_ref[...].astype(BF16), preferred_element_type=F32) / lm
    o_ref[:, FOX_DIM:ATT_COLS] = _diag_blocks(om, MEM_HEADS, seq, HEAD_DIM).astype(BF16)


def _sample_attention(qa, ka, vb, qm, pk, pv, plf_rep, mk, mv):
    nbatch, seq, _ = qa.shape
    past = pk.shape[1]
    blk = lambda rows, cols: pl.BlockSpec((None, rows, cols), lambda b: (b, 0, 0))
    return pl.pallas_call(
        functools.partial(_sample_att_body, seq, past),
        grid=(nbatch,),
        in_specs=[blk(seq, QK_COLS), blk(seq, QK_COLS), blk(seq, FOX_DIM), blk(seq, MEM_DIM),
                  blk(past, FOX_DIM), blk(past, FOX_DIM), blk(past, AUG),
                  blk(N_MEM, MEM_DIM), blk(N_MEM, MEM_DIM)],
        out_specs=blk(seq, ATT_COLS),
        out_shape=jax.ShapeDtypeStruct((nbatch, seq, ATT_COLS), BF16),
        compiler_params=pltpu.CompilerParams(
            dimension_semantics=("arbitrary",), vmem_limit_bytes=VMEM_LIMIT),
        name="sample_attention",
    )(qa, ka, vb, qm, pk, pv, plf_rep, mk, mv)


def _prep_weights(norm_ffn1, w_g1, w_u1, w_d1, norm_mix, w_in, b_forget, conv_w, conv_b,
                  q_norm_fox, k_norm_fox, q_norm_mem, k_norm_mem, norm_mem, w_mem_kv, w_out,
                  norm_ffn2, w_g2, w_u2, w_d2, norm_final):
    def gate_up(wg, wu):
        wg = wg.reshape(D_MODEL, N_FF_CHUNKS, FF_CHUNK)
        wu = wu.reshape(D_MODEL, N_FF_CHUNKS, FF_CHUNK)
        return jnp.concatenate([wg, wu], axis=2).transpose(1, 0, 2).astype(BF16)

    w_f = w_in[:, COL_QM:COL_QM + FOX_HEADS]
    w_qm = w_in[:, COL_QM + FOX_HEADS:]
    win = jnp.concatenate(
        [w_in[:, :COL_QM], w_qm, jnp.repeat(w_f, AUG_GROUP, axis=1), w_f,
         jnp.zeros((D_MODEL, LANES - FOX_HEADS), F32)], axis=1).astype(BF16)
    row = lambda v: v.reshape(1, -1).astype(F32)
    return {
        "g1": row(norm_ffn1), "wgu1": gate_up(w_g1, w_u1), "wd1": w_d1.reshape(N_FF_CHUNKS, FF_CHUNK, D_MODEL).astype(BF16),
        "gmix": row(norm_mix), "win": win, "conv_w": conv_w.astype(F32), "conv_b": row(conv_b),
        "gq": row(jnp.tile(q_norm_fox, FOX_HEADS)) * (SCALE * LOG2E), "gk": row(jnp.tile(k_norm_fox, FOX_HEADS)),
        "gqm": row(jnp.tile(q_norm_mem, MEM_HEADS)),
        "bf_rep": row(jnp.repeat(b_forget, AUG_GROUP)), "bf": row(b_forget),
        "gmem": row(norm_mem), "wmem": w_mem_kv.astype(BF16), "gkm": row(jnp.tile(k_norm_mem, MEM_HEADS)),
        "wo": w_out.astype(BF16), "g2": row(norm_ffn2), "wgu2": gate_up(w_g2, w_u2),
        "wd2": w_d2.reshape(N_FF_CHUNKS, FF_CHUNK, D_MODEL).astype(BF16), "gfin": row(norm_final),
    }


def kernel(x_prompt, x_sample, cache_fox_k, cache_fox_v, cache_fox_logf, state_conv, cache_mem_k, cache_mem_v, mem_prompt, norm_ffn1, w_ffn1_gate, w_ffn1_up, w_ffn1_down, norm_mix, w_in, b_forget, conv_w, conv_b, q_norm_fox, k_norm_fox, q_norm_mem, k_norm_mem, norm_mem, w_mem_kv, w_out, norm_ffn2, w_ffn2_gate, w_ffn2_up, w_ffn2_down, norm_final):
    depth = w_in.shape[0]
    assert depth == 1, "single-layer step"
    B, T, _ = x_prompt.shape
    nbs, seq_s, _ = x_sample.shape
    past = cache_fox_k.shape[2]
    l = 0
    wts = _prep_weights(norm_ffn1[l], w_ffn1_gate[l], w_ffn1_up[l], w_ffn1_down[l], norm_mix[l], w_in[l],
                        b_forget[l], conv_w[l], conv_b[l], q_norm_fox[l], k_norm_fox[l], q_norm_mem[l],
                        k_norm_mem[l], norm_mem[l], w_mem_kv[l], w_out[l], norm_ffn2[l], w_ffn2_gate[l],
                        w_ffn2_up[l], w_ffn2_down[l], norm_final[l])
    tile = 512

    mk_p, mv_p, mkb, mvb = _memkv(mem_prompt, wts)
    x1, yc, qa, ka, vb, kf, vf, lf, qm, nc = _stage1(x_prompt, None, wts, nb=1, seq=tile, carry=True)
    ya = _prompt_attention(qa, ka, vb, qm, mkb, mvb, tq=tile, tk=tile)
    y_prompt = _stage3(x1, yc, ya, wts, tm=tile)

    xs = x_sample.reshape(1, nbs * seq_s, D_MODEL)
    x1s, ycs, qas, kas, vbs, kfs, vfs, lfs, qms, ncs = _stage1(
        xs, state_conv[l], wts, nb=tile // seq_s, seq=seq_s, carry=False)
    per_seq = lambda a: a.reshape(nbs, seq_s, a.shape[-1])
    yas = _sample_attention(
        per_seq(qas), per_seq(kas), per_seq(vbs), per_seq(qms),
        cache_fox_k[l].reshape(nbs, past, FOX_DIM), cache_fox_v[l].reshape(nbs, past, FOX_DIM),
        jnp.repeat(cache_fox_logf[l], AUG_GROUP, axis=-1),
        cache_mem_k[l].reshape(nbs, N_MEM, MEM_DIM), cache_mem_v[l].reshape(nbs, N_MEM, MEM_DIM))
    y_sample = _stage3(x1s, ycs, yas.reshape(1, nbs * seq_s, ATT_COLS), wts, tm=tile)

    heads = lambda a, b, t, nh: a.reshape(1, b, t, nh, HEAD_DIM)
    return (y_prompt, y_sample.reshape(nbs, seq_s, D_MODEL),
            heads(kf, B, T, FOX_HEADS), heads(vf, B, T, FOX_HEADS), lf.reshape(1, B, T, FOX_HEADS),
            nc.reshape(1, B, CONV_K - 1, CONV_DIM),
            heads(mk_p, B, N_MEM, MEM_HEADS), heads(mv_p, B, N_MEM, MEM_HEADS),
            heads(kfs, nbs, seq_s, FOX_HEADS), heads(vfs, nbs, seq_s, FOX_HEADS),
            lfs.reshape(1, nbs, seq_s, FOX_HEADS), ncs.reshape(1, nbs, CONV_K - 1, CONV_DIM))
```

```python
import functools

import jax
import jax.numpy as jnp
from jax import lax
from jax.experimental import pallas as pl
from jax.experimental.pallas import tpu as pltpu

F32 = jnp.float32
BF16 = jnp.bfloat16

D_MODEL = 1024
HEAD_DIM = 64
CONV_DIM = 256
CONV_K = 3
FOX_HEADS = 8
FOX_DIM = FOX_HEADS * HEAD_DIM
MEM_HEADS = 4
MEM_DIM = MEM_HEADS * HEAD_DIM
N_MEM = 256
D_FF = 2816
EPS = 1e-6

LANES = 128
PAIR = 2 * HEAD_DIM
FF_CHUNK = 256
N_FF_CHUNKS = D_FF // FF_CHUNK
AUG = LANES
AUG_GROUP = AUG // FOX_HEADS
QK_COLS = FOX_DIM + AUG
ATT_COLS = FOX_DIM + MEM_DIM
COL_Q = 3 * CONV_DIM
COL_K = COL_Q + FOX_DIM
COL_V = COL_K + FOX_DIM
COL_F = COL_V + FOX_DIM
COL_QM = COL_F + FOX_HEADS
TAIL_COLS = MEM_DIM + AUG + LANES
NEG = -1e30
SCALE = HEAD_DIM ** -0.5
LOG2E = 1.4426950408889634
VMEM_LIMIT = 56 * 1024 * 1024


def _idiv(x, d):
    if d & (d - 1) == 0:
        return lax.shift_right_logical(x, d.bit_length() - 1)
    return x // d


def _imod(x, d):
    if d & (d - 1) == 0:
        return x & (d - 1)
    return x % d


def _rms(x, g):
    ms = jnp.mean(x * x, axis=-1, keepdims=True)
    return x * lax.rsqrt(ms + EPS) * g


def _pair_headnorm(x, g):
    lo = lax.broadcasted_iota(jnp.int32, (1, PAIR), 1) < HEAD_DIM
    x2 = x * x
    s_lo = jnp.sum(jnp.where(lo, x2, 0.0), axis=-1, keepdims=True)
    s_hi = jnp.sum(jnp.where(lo, 0.0, x2), axis=-1, keepdims=True)
    r = jnp.where(lo, lax.rsqrt(s_lo * (1.0 / HEAD_DIM) + EPS),
                  lax.rsqrt(s_hi * (1.0 / HEAD_DIM) + EPS))
    return x * r * g


def _store_cols(ref, sl, x, transposed):
    if transposed:
        ref[sl, :] = x.T
    else:
        ref[:, sl] = x


def _split3(x):
    hi = x.astype(BF16).astype(F32)
    r = x - hi
    mid = r.astype(BF16).astype(F32)
    lo = (r - mid).astype(BF16).astype(F32)
    return hi, mid, lo


def _ffn(xn, wg_ref, wu_ref, wd_ref):
    acc = None
    for c in range(N_FF_CHUNKS):
        sl = slice(c * FF_CHUNK, (c + 1) * FF_CHUNK)
        g = jnp.dot(xn, wg_ref[:, sl], preferred_element_type=F32)
        u = jnp.dot(xn, wu_ref[:, sl], preferred_element_type=F32)
        a = (g * jax.nn.sigmoid(g) * u).astype(BF16)
        d = jnp.dot(a, wd_ref[sl, :], preferred_element_type=F32)
        acc = d if acc is None else acc + d
    return acc


def _log_sigmoid(x):
    return jnp.minimum(x, 0.0) - jnp.log1p(jnp.exp(-jnp.abs(x)))


def _softmax_rows(s):
    m = jnp.max(s, axis=-1, keepdims=True)
    p = jnp.exp(s - m)
    return p, jnp.sum(p, axis=-1, keepdims=True)


def _dot_nt(a, b):
    return lax.dot_general(a, b, (((1,), (1,)), ((), ())), preferred_element_type=F32)


def _stage1_body(nb, seq, carry, *refs):
    tm = nb * seq
    if carry:
        x_ref, rest = refs[0], refs[1:]
        st_ref = None
    else:
        x_ref, st_ref, rest = refs[0], refs[1], refs[2:]
    (g1_ref, wg_ref, wu_ref, wd_ref, gmix_ref, win_ref, wtail_ref, cw_ref, cb_ref, gq_ref, gk_ref, gqm_ref,
     bfr_ref, bf_ref,
     x1_ref, yc_ref, qa_ref, ka_ref, vb_ref, kf_ref, vf_ref, lf_ref, qm_ref, nc_ref,
     cs_ref, cum_ref) = rest

    x = x_ref[...]
    y = _ffn(_rms(x, g1_ref[...]).astype(BF16), wg_ref, wu_ref, wd_ref)
    x1 = x + 0.5 * y
    x1_ref[...] = x1
    h = _rms(x1, gmix_ref[...]).astype(BF16)

    ucb = jnp.dot(h, win_ref[:, 0:COL_Q], preferred_element_type=F32)
    ci = ucb[:, CONV_DIM:2 * CONV_DIM] * ucb[:, 0:CONV_DIM]
    if carry:
        @pl.when(pl.program_id(1) == 0)
        def _():
            cs_ref[:, 0:8, :] = jnp.zeros((nb, 8, CONV_DIM), F32)
    else:
        cs_ref[:, 8 - (CONV_K - 1):8, :] = st_ref[...]
    cs_ref[:, 8:8 + seq, :] = ci.reshape(nb, seq, CONV_DIM)
    conv = cb_ref[...] + cw_ref[CONV_K - 1:CONV_K, :] * ci
    for i in range(CONV_K - 1):
        shifted = cs_ref[:, 8 - (CONV_K - 1) + i:8 - (CONV_K - 1) + i + seq, :]
        conv = conv + cw_ref[i:i + 1, :] * shifted.reshape(tm, CONV_DIM)
    yc_ref[...] = (ucb[:, 2 * CONV_DIM:3 * CONV_DIM] * conv).astype(BF16)
    tail = cs_ref[:, seq:seq + 8, :]
    nc_ref[...] = tail[:, 8 - (CONV_K - 1):, :].reshape(nc_ref.shape)
    if carry:
        cs_ref[:, 0:8, :] = tail

    for name, col, g_ref, out_ref in (("q", COL_Q, gq_ref, qa_ref), ("k", COL_K, gk_ref, ka_ref)):
        pr = jnp.dot(h, win_ref[:, col:col + FOX_DIM], preferred_element_type=F32)
        for g in range(FOX_HEADS // 2):
            sl = slice(g * PAIR, (g + 1) * PAIR)
            xn = _pair_headnorm(pr[:, sl], g_ref[:, sl])
            if name == "k":
                _store_cols(kf_ref, sl, xn, carry)
            out_ref[:, sl] = xn.astype(BF16)
    v = jnp.dot(h, win_ref[:, COL_V:COL_V + FOX_DIM], preferred_element_type=F32)
    for g in range(FOX_HEADS // 2):
        sl = slice(g * PAIR, (g + 1) * PAIR)
        _store_cols(vf_ref, sl, v[:, sl], carry)
    vb_ref[...] = v.astype(BF16)

    tail_cols = jnp.dot(h, wtail_ref[...], preferred_element_type=F32)
    for g in range(MEM_HEADS // 2):
        sl = slice(g * PAIR, (g + 1) * PAIR)
        qm_ref[:, sl] = _pair_headnorm(tail_cols[:, sl], gqm_ref[:, sl]).astype(BF16)
    fl = tail_cols[:, MEM_DIM:]
    lf = _log_sigmoid(fl[:, AUG:2 * AUG] + bf_ref[...])
    if carry:
        lf_ref[...] = lf.T[0:FOX_HEADS, :]
    else:
        lf_ref[...] = lf[:, 0:FOX_HEADS]
    logf = _log_sigmoid(fl[:, 0:AUG] + bfr_ref[...])

    row = lax.broadcasted_iota(jnp.int32, (tm, tm), 0)
    col = lax.broadcasted_iota(jnp.int32, (tm, tm), 1)
    tri = col <= row
    if nb > 1:
        tri = jnp.logical_and(tri, _idiv(row, seq) == _idiv(col, seq))
    tri = jnp.where(tri, 1.0, 0.0).astype(BF16)
    hi, mid, lo = _split3(logf)
    parts = jnp.concatenate([hi.astype(BF16), mid.astype(BF16), lo.astype(BF16)], axis=1)
    cs = jnp.dot(tri, parts, preferred_element_type=F32)
    cum = cs[:, 0:AUG] + cs[:, AUG:2 * AUG] + cs[:, 2 * AUG:3 * AUG]
    if carry:
        @pl.when(pl.program_id(1) == 0)
        def _():
            cum_ref[...] = jnp.zeros(cum_ref.shape, F32)
        cum = cum + cum_ref[0:1, :]
        cum_ref[0:1, :] = cum[tm - 1:tm, :]

    j = _imod(lax.broadcasted_iota(jnp.int32, (1, AUG), 1), AUG_GROUP)
    chi, cmid, clo = _split3(cum * LOG2E)
    aq = jnp.where(j == 0, chi, jnp.where(j == 1, cmid, jnp.where(j == 2, clo, jnp.where(j < 6, 1.0, 0.0))))
    ak = jnp.where(j < 3, 1.0, jnp.where(j == 3, -chi, jnp.where(j == 4, -cmid, jnp.where(j == 5, -clo, 0.0))))
    qa_ref[:, FOX_DIM:QK_COLS] = aq.astype(BF16)
    ka_ref[:, FOX_DIM:QK_COLS] = ak.astype(BF16)


def _stage1(x3, state, wts, *, nb, seq, carry):
    G, T, _ = x3.shape
    tm = nb * seq
    nt = T // tm
    const = lambda shape: pl.BlockSpec(shape, lambda g, t: (0,) * len(shape), pipeline_mode=pl.Buffered(1))
    tok = lambda cols: pl.BlockSpec((None, tm, cols), lambda g, t: (g, t, 0))
    if carry:
        feat = lambda rows: pl.BlockSpec((None, rows, tm), lambda g, t: (g, 0, t))
        feat_shape = lambda rows: jax.ShapeDtypeStruct((G, rows, T), F32)
    else:
        feat = lambda rows: pl.BlockSpec((None, tm, rows), lambda g, t: (g, t, 0))
        feat_shape = lambda rows: jax.ShapeDtypeStruct((G, T, rows), F32)
    in_specs = [tok(D_MODEL)]
    args = [x3]
    if not carry:
        in_specs.append(pl.BlockSpec((nb, CONV_K - 1, CONV_DIM), lambda g, t: (t, 0, 0)))
        args.append(state)
    names = ("g1", "wg1", "wu1", "wd1", "gmix", "win", "wtail", "conv_w", "conv_b", "gq", "gk", "gqm",
             "bf_rep", "bf")
    for n in names:
        in_specs.append(const(wts[n].shape))
        args.append(wts[n])
    shp = lambda cols, dt: jax.ShapeDtypeStruct((G, T, cols), dt)
    out_shape = [shp(D_MODEL, F32), shp(CONV_DIM, BF16), shp(QK_COLS, BF16), shp(QK_COLS, BF16),
                 shp(FOX_DIM, BF16), feat_shape(FOX_DIM), feat_shape(FOX_DIM), feat_shape(FOX_HEADS),
                 shp(MEM_DIM, BF16)]
    out_specs = [tok(D_MODEL), tok(CONV_DIM), tok(QK_COLS), tok(QK_COLS), tok(FOX_DIM), feat(FOX_DIM),
                 feat(FOX_DIM), feat(FOX_HEADS), tok(MEM_DIM)]
    if carry:
        out_shape.append(jax.ShapeDtypeStruct((G, CONV_K - 1, CONV_DIM), F32))
        out_specs.append(pl.BlockSpec((None, CONV_K - 1, CONV_DIM), lambda g, t: (g, 0, 0)))
    else:
        out_shape.append(jax.ShapeDtypeStruct((nt * nb, CONV_K - 1, CONV_DIM), F32))
        out_specs.append(pl.BlockSpec((nb, CONV_K - 1, CONV_DIM), lambda g, t: (t, 0, 0)))
    return pl.pallas_call(
        functools.partial(_stage1_body, nb, seq, carry),
        grid=(G, nt),
        in_specs=in_specs,
        out_specs=out_specs,
        out_shape=out_shape,
        scratch_shapes=[pltpu.VMEM((nb, seq + 8, CONV_DIM), F32), pltpu.VMEM((8, AUG), F32)],
        compiler_params=pltpu.CompilerParams(
            dimension_semantics=("arbitrary", "arbitrary"), vmem_limit_bytes=VMEM_LIMIT),
        name="stage1_carry" if carry else "stage1_batched",
    )(*args)


def _stage3_body(x1_ref, yc_ref, ya_ref, wo_ref, g2_ref, wg_ref, wu_ref, wd_ref, gfin_ref, y_ref):
    x2 = (x1_ref[...]
          + jnp.dot(yc_ref[...], wo_ref[0:CONV_DIM, :], preferred_element_type=F32)
          + jnp.dot(ya_ref[...], wo_ref[CONV_DIM:, :], preferred_element_type=F32))
    y = _ffn(_rms(x2, g2_ref[...]).astype(BF16), wg_ref, wu_ref, wd_ref)
    y_ref[...] = _rms(x2 + 0.5 * y, gfin_ref[...])


def _stage3(x1, yc, ya, wts, *, tm):
    G, T, _ = x1.shape
    const = lambda shape: pl.BlockSpec(shape, lambda g, t: (0,) * len(shape), pipeline_mode=pl.Buffered(1))
    tok = lambda cols: pl.BlockSpec((None, tm, cols), lambda g, t: (g, t, 0))
    names = ("wo", "g2", "wg2", "wu2", "wd2", "gfin")
    return pl.pallas_call(
        _stage3_body,
        grid=(G, T // tm),
        in_specs=[tok(D_MODEL), tok(CONV_DIM), tok(ATT_COLS)] + [const(wts[n].shape) for n in names],
        out_specs=tok(D_MODEL),
        out_shape=jax.ShapeDtypeStruct((G, T, D_MODEL), F32),
        compiler_params=pltpu.CompilerParams(
            dimension_semantics=("arbitrary", "arbitrary"), vmem_limit_bytes=VMEM_LIMIT),
        name="stage3",
    )(x1, yc, ya, *[wts[n] for n in names])


def _memkv_body(mem_ref, gmem_ref, w_ref, gk_ref, mk_ref, mv_ref, mkb_ref, mvb_ref):
    kv = jnp.dot(_rms(mem_ref[...], gmem_ref[...]).astype(BF16), w_ref[...], preferred_element_type=F32)
    for g in range(MEM_HEADS // 2):
        sl = slice(g * PAIR, (g + 1) * PAIR)
        mk = _pair_headnorm(kv[:, sl], gk_ref[:, sl])
        mk_ref[sl, :] = mk.T
        mkb_ref[:, sl] = mk.astype(BF16)
    mv = kv[:, MEM_DIM:]
    mv_ref[...] = mv.T
    mvb_ref[...] = mv.astype(BF16)


def _memkv(mem, wts):
    B = mem.shape[0]
    const = lambda shape: pl.BlockSpec(shape, lambda b: (0,) * len(shape))
    blk = lambda cols: pl.BlockSpec((None, N_MEM, cols), lambda b: (b, 0, 0))
    shp = lambda dt: jax.ShapeDtypeStruct((B, N_MEM, MEM_DIM), dt)
    names = ("gmem", "wmem", "gkm")
    return pl.pallas_call(
        _memkv_body,
        grid=(B,),
        in_specs=[blk(D_MODEL)] + [const(wts[n].shape) for n in names],
        out_specs=[blk(MEM_DIM)] * 4,
        out_shape=[shp(F32), shp(F32), shp(BF16), shp(BF16)],
        compiler_params=pltpu.CompilerParams(dimension_semantics=("arbitrary",)),
        name="memkv",
    )(mem, *[wts[n] for n in names])


def _head_masks():
    lane = lax.broadcasted_iota(jnp.int32, (1, LANES), 1)
    lo = lane < HEAD_DIM
    return lo, _idiv(lane, AUG_GROUP)


def _prompt_att_body(tq, tk, iq_ref, ik_ref, qa_ref, ka_ref, vb_ref, qm_ref, mk_ref, mv_ref,
                     o_ref, m_ref, acc_ref):
    p = pl.program_id(1)
    iq = iq_ref[p]
    ik = ik_ref[p]
    lo, aug_head = _head_masks()

    @pl.when(ik == 0)
    def _():
        m_ref[...] = jnp.full(m_ref.shape, NEG, F32)
        acc_ref[...] = jnp.zeros(acc_ref.shape, F32)

    def step(masked):
        if masked:
            qpos = iq * tq + lax.broadcasted_iota(jnp.int32, (tq, tk), 0)
            kpos = ik * tk + lax.broadcasted_iota(jnp.int32, (tq, tk), 1)
            visible = kpos <= qpos
        q_aug = qa_ref[:, FOX_DIM:QK_COLS]
        k_aug = ka_ref[:, FOX_DIM:QK_COLS]
        ones = jnp.ones((tk, LANES), BF16)
        for h in range(FOX_HEADS):
            sl = slice((h // 2) * PAIR, (h // 2 + 1) * PAIR)
            mine = lo if h % 2 == 0 else jnp.logical_not(lo)
            qh = jnp.concatenate([jnp.where(mine, qa_ref[:, sl], 0.0).astype(BF16),
                                  jnp.where(aug_head == h, q_aug, 0.0).astype(BF16)], axis=1)
            kh = jnp.concatenate([ka_ref[:, sl], k_aug], axis=1)
            s = _dot_nt(qh, kh)
            if masked:
                s = jnp.where(visible, s, NEG)
            blocks = [s[:, c * LANES:(c + 1) * LANES] for c in range(tk // LANES)]
            blk_max = functools.reduce(jnp.maximum, blocks)
            m_prev = m_ref[h]
            m_new = jnp.maximum(m_prev, jnp.max(blk_max, axis=-1, keepdims=True))
            alpha = jnp.exp2(m_prev - m_new)
            pexp = jnp.concatenate([jnp.exp2(b - m_new) for b in blocks], axis=1).astype(BF16)
            pv = jnp.dot(pexp, jnp.concatenate([vb_ref[:, sl], ones], axis=1), preferred_element_type=F32)
            acc_ref[h] = jnp.concatenate([alpha, alpha], axis=1) * acc_ref[h] + pv
            m_ref[h] = m_new

    diag = (ik + 1) * tk - 1 > iq * tq

    @pl.when(diag)
    def _():
        step(True)

    @pl.when(jnp.logical_not(diag))
    def _():
        step(False)

    @pl.when((ik + 1) * tk >= (iq + 1) * tq)
    def _():
        for g in range(FOX_HEADS // 2):
            even = acc_ref[2 * g, :, 0:PAIR] / acc_ref[2 * g, :, PAIR:2 * PAIR]
            odd = acc_ref[2 * g + 1, :, 0:PAIR] / acc_ref[2 * g + 1, :, PAIR:2 * PAIR]
            o_ref[:, g * PAIR:(g + 1) * PAIR] = jnp.where(lo, even, odd).astype(BF16)
        for g in range(MEM_HEADS // 2):
            sl = slice(g * PAIR, (g + 1) * PAIR)
            outs = []
            for mine in (lo, jnp.logical_not(lo)):
                qh = jnp.where(mine, qm_ref[:, sl], 0.0).astype(BF16)
                pexp, lsum = _softmax_rows(_dot_nt(qh, mk_ref[:, sl]) * SCALE)
                outs.append(jnp.dot(pexp.astype(BF16), mv_ref[:, sl], preferred_element_type=F32) / lsum)
            o_ref[:, FOX_DIM + g * PAIR:FOX_DIM + (g + 1) * PAIR] = jnp.where(lo, outs[0], outs[1]).astype(BF16)


def _prompt_attention(qa, ka, vb, qm, mkb, mvb, *, tq, tk):
    B, T, _ = qa.shape
    pairs = [(i, j) for i in range(T // tq) for j in range(T // tk) if j * tk <= i * tq + tq - 1]
    iq_tab = jnp.asarray([p[0] for p in pairs], jnp.int32)
    ik_tab = jnp.asarray([p[1] for p in pairs], jnp.int32)
    grid_spec = pltpu.PrefetchScalarGridSpec(
        num_scalar_prefetch=2,
        grid=(B, len(pairs)),
        in_specs=[
            pl.BlockSpec((None, tq, QK_COLS), lambda b, p, iq, ik: (b, iq[p], 0)),
            pl.BlockSpec((None, tk, QK_COLS), lambda b, p, iq, ik: (b, ik[p], 0)),
            pl.BlockSpec((None, tk, FOX_DIM), lambda b, p, iq, ik: (b, ik[p], 0)),
            pl.BlockSpec((None, tq, MEM_DIM), lambda b, p, iq, ik: (b, iq[p], 0)),
            pl.BlockSpec((None, N_MEM, MEM_DIM), lambda b, p, iq, ik: (b, 0, 0)),
            pl.BlockSpec((None, N_MEM, MEM_DIM), lambda b, p, iq, ik: (b, 0, 0)),
        ],
        out_specs=pl.BlockSpec((None, tq, ATT_COLS), lambda b, p, iq, ik: (b, iq[p], 0)),
        scratch_shapes=[pltpu.VMEM((FOX_HEADS, tq, LANES), F32), pltpu.VMEM((FOX_HEADS, tq, 2 * PAIR), F32)],
    )
    return pl.pallas_call(
        functools.partial(_prompt_att_body, tq, tk),
        grid_spec=grid_spec,
        out_shape=jax.ShapeDtypeStruct((B, T, ATT_COLS), BF16),
        compiler_params=pltpu.CompilerParams(
            dimension_semantics=("arbitrary", "arbitrary"), vmem_limit_bytes=VMEM_LIMIT),
        name="prompt_attention",
    )(iq_tab, ik_tab, qa, ka, vb, qm, mkb, mvb)


def _diag_blocks(x, nh, rows, width):
    head = _idiv(lax.broadcasted_iota(jnp.int32, (1, nh * width), 1), width)
    out = None
    for h in range(nh):
        part = jnp.where(head == h, x[h * rows:(h + 1) * rows, :], 0.0)
        out = part if out is None else out + part
    return out


def _sample_att_body(seq, past, qa_ref, ka_ref, vb_ref, qm_ref, kt_ref, vt_ref, lt_ref, mkt_ref, mvt_ref, o_ref):
    blk = LANES
    nblk = past // blk
    lt = lt_ref[...]
    x = jnp.concatenate([lt[:, b * blk:(b + 1) * blk] for b in range(nblk)], axis=0)
    n = x.shape[0]
    parts = jnp.concatenate(_split3(x), axis=0).astype(BF16)
    src = lax.broadcasted_iota(jnp.int32, (blk, blk), 0)
    dst = lax.broadcasted_iota(jnp.int32, (blk, blk), 1)
    later = jnp.where(src > dst, 1.0, 0.0).astype(BF16)
    loc = jnp.dot(parts, later, preferred_element_type=F32)
    tot = jnp.dot(parts, jnp.ones((blk, blk), BF16), preferred_element_type=F32)
    loc = loc[0:n] + loc[n:2 * n] + loc[2 * n:3 * n]
    tot = tot[0:n] + tot[n:2 * n] + tot[2 * n:3 * n]
    running = jnp.zeros((FOX_HEADS, blk), F32)
    suffix = [None] * nblk
    for b in reversed(range(nblk)):
        rows = slice(b * FOX_HEADS, (b + 1) * FOX_HEADS)
        suffix[b] = loc[rows] + running
        running = running + tot[rows]
    rt = jnp.concatenate(suffix, axis=1) * LOG2E
    bias = jnp.concatenate([jnp.broadcast_to(rt[h:h + 1, :], (seq, past)) for h in range(FOX_HEADS)], axis=0)

    nrow = FOX_HEADS * seq
    lane = lax.broadcasted_iota(jnp.int32, (nrow, QK_COLS), 1)
    lane_head = jnp.where(lane < FOX_DIM, _idiv(lane, HEAD_DIM), _idiv(lane - FOX_DIM, AUG_GROUP))
    row_head = _idiv(lax.broadcasted_iota(jnp.int32, (nrow, QK_COLS), 0), seq)
    qbd = jnp.where(lane_head == row_head, jnp.concatenate([qa_ref[...]] * FOX_HEADS, axis=0), 0.0).astype(BF16)
    aug_row = _imod(lax.broadcasted_iota(jnp.int32, (AUG, past), 0), AUG_GROUP)
    kt = jnp.concatenate([kt_ref[...].astype(BF16), jnp.where(aug_row < 3, 1.0, 0.0).astype(BF16)], axis=0)
    s_past = jnp.dot(qbd, kt, preferred_element_type=F32) + bias
    s_new = _dot_nt(qbd, ka_ref[...])
    qi = _imod(lax.broadcasted_iota(jnp.int32, (nrow, seq), 0), seq)
    kj = lax.broadcasted_iota(jnp.int32, (nrow, seq), 1)
    s_new = jnp.where(kj <= qi, s_new, NEG)
    m = jnp.maximum(jnp.max(s_past, axis=-1, keepdims=True), jnp.max(s_new, axis=-1, keepdims=True))
    p_past = jnp.exp2(s_past - m)
    p_new = jnp.exp2(s_new - m)
    lsum = jnp.sum(p_past, axis=-1, keepdims=True) + jnp.sum(p_new, axis=-1, keepdims=True)
    o = (_dot_nt(p_past.astype(BF16), vt_ref[...].astype(BF16))
         + jnp.dot(p_new.astype(BF16), vb_ref[...], preferred_element_type=F32)) / lsum
    o_ref[:, 0:FOX_DIM] = _diag_blocks(o, FOX_HEADS, seq, HEAD_DIM).astype(BF16)

    nrow_m = MEM_HEADS * seq
    lane_m = _idiv(lax.broadcasted_iota(jnp.int32, (nrow_m, MEM_DIM), 1), HEAD_DIM)
    row_m = _idiv(lax.broadcasted_iota(jnp.int32, (nrow_m, MEM_DIM), 0), seq)
    qbd_m = jnp.where(lane_m == row_m, jnp.concatenate([qm_ref[...]] * MEM_HEADS, axis=0), 0.0).astype(BF16)
    pm, lm = _softmax_rows(jnp.dot(qbd_m, mkt_ref[...].astype(BF16), preferred_element_type=F32) * SCALE)
    om = _dot_nt(pm.astype(BF16), mvt_ref[...].astype(BF16)) / lm
    o_ref[:, FOX_DIM:ATT_COLS] = _diag_blocks(om, MEM_HEADS, seq, HEAD_DIM).astype(BF16)


def _sample_attention(qa, ka, vb, qm, pk, pv, plf, mk, mv):
    nbatch, seq, _ = qa.shape
    past = pk.shape[2]
    blk = lambda rows, cols: pl.BlockSpec((None, rows, cols), lambda b: (b, 0, 0))
    return pl.pallas_call(
        functools.partial(_sample_att_body, seq, past),
        grid=(nbatch,),
        in_specs=[blk(seq, QK_COLS), blk(seq, QK_COLS), blk(seq, FOX_DIM), blk(seq, MEM_DIM),
                  blk(FOX_DIM, past), blk(FOX_DIM, past), blk(FOX_HEADS, past),
                  blk(MEM_DIM, N_MEM), blk(MEM_DIM, N_MEM)],
        out_specs=blk(seq, ATT_COLS),
        out_shape=jax.ShapeDtypeStruct((nbatch, seq, ATT_COLS), BF16),
        compiler_params=pltpu.CompilerParams(
            dimension_semantics=("arbitrary",), vmem_limit_bytes=VMEM_LIMIT),
        name="sample_attention",
    )(qa, ka, vb, qm, pk, pv, plf, mk, mv)


def _prep_weights(norm_ffn1, w_g1, w_u1, w_d1, norm_mix, w_in, b_forget, conv_w, conv_b,
                  q_norm_fox, k_norm_fox, q_norm_mem, k_norm_mem, norm_mem, w_mem_kv, w_out,
                  norm_ffn2, w_g2, w_u2, w_d2, norm_final):
    w_f = w_in[:, COL_F:COL_QM]
    wtail = jnp.concatenate(
        [w_in[:, COL_QM:], jnp.repeat(w_f, AUG_GROUP, axis=1), w_f,
         jnp.zeros((D_MODEL, LANES - FOX_HEADS), F32)], axis=1).astype(BF16)
    row = lambda v: v.reshape(1, -1).astype(F32)
    return {
        "g1": row(norm_ffn1), "wg1": w_g1.astype(BF16), "wu1": w_u1.astype(BF16), "wd1": w_d1.astype(BF16),
        "gmix": row(norm_mix), "win": w_in.astype(BF16), "wtail": wtail,
        "conv_w": conv_w.astype(F32), "conv_b": row(conv_b),
        "gq": row(jnp.tile(q_norm_fox, FOX_HEADS)) * (SCALE * LOG2E), "gk": row(jnp.tile(k_norm_fox, FOX_HEADS)),
        "gqm": row(jnp.tile(q_norm_mem, MEM_HEADS)),
        "bf_rep": row(jnp.repeat(b_forget, AUG_GROUP)),
        "bf": jnp.pad(row(b_forget), ((0, 0), (0, LANES - FOX_HEADS))),
        "gmem": row(norm_mem), "wmem": w_mem_kv.astype(BF16), "gkm": row(jnp.tile(k_norm_mem, MEM_HEADS)),
        "wo": w_out.astype(BF16), "g2": row(norm_ffn2), "wg2": w_g2.astype(BF16), "wu2": w_u2.astype(BF16),
        "wd2": w_d2.astype(BF16), "gfin": row(norm_final),
    }


def kernel(x_prompt, x_sample, cache_fox_k, cache_fox_v, cache_fox_logf, state_conv, cache_mem_k, cache_mem_v, mem_prompt, norm_ffn1, w_ffn1_gate, w_ffn1_up, w_ffn1_down, norm_mix, w_in, b_forget, conv_w, conv_b, q_norm_fox, k_norm_fox, q_norm_mem, k_norm_mem, norm_mem, w_mem_kv, w_out, norm_ffn2, w_ffn2_gate, w_ffn2_up, w_ffn2_down, norm_final):
    depth = w_in.shape[0]
    assert depth == 1, "single-layer step"
    B, T, _ = x_prompt.shape
    nbs, seq_s, _ = x_sample.shape
    past = cache_fox_k.shape[2]
    l = 0
    wts = _prep_weights(norm_ffn1[l], w_ffn1_gate[l], w_ffn1_up[l], w_ffn1_down[l], norm_mix[l], w_in[l],
                        b_forget[l], conv_w[l], conv_b[l], q_norm_fox[l], k_norm_fox[l], q_norm_mem[l],
                        k_norm_mem[l], norm_mem[l], w_mem_kv[l], w_out[l], norm_ffn2[l], w_ffn2_gate[l],
                        w_ffn2_up[l], w_ffn2_down[l], norm_final[l])
    tile = 512

    mk_p, mv_p, mkb, mvb = _memkv(mem_prompt, wts)
    x1, yc, qa, ka, vb, kf, vf, lf, qm, nc = _stage1(x_prompt, None, wts, nb=1, seq=tile, carry=True)
    ya = _prompt_attention(qa, ka, vb, qm, mkb, mvb, tq=tile, tk=tile)
    y_prompt = _stage3(x1, yc, ya, wts, tm=tile)

    xs = x_sample.reshape(1, nbs * seq_s, D_MODEL)
    x1s, ycs, qas, kas, vbs, kfs, vfs, lfs, qms, ncs = _stage1(
        xs, state_conv[l], wts, nb=tile // seq_s, seq=seq_s, carry=False)
    per_seq = lambda a: a.reshape(nbs, seq_s, a.shape[-1])
    feat_major = lambda a: a.reshape(a.shape[0], a.shape[1], -1).transpose(0, 2, 1)
    yas = _sample_attention(
        per_seq(qas), per_seq(kas), per_seq(vbs), per_seq(qms),
        feat_major(cache_fox_k[l]), feat_major(cache_fox_v[l]), feat_major(cache_fox_logf[l]),
        feat_major(cache_mem_k[l]), feat_major(cache_mem_v[l]))
    y_sample = _stage3(x1s, ycs, yas.reshape(1, nbs * seq_s, ATT_COLS), wts, tm=tile)

    heads = lambda a, b, t, nh: a.reshape(1, b, t, nh, HEAD_DIM)
    token_major = lambda a, nh: a.reshape(a.shape[0], nh, -1, a.shape[2]).transpose(0, 3, 1, 2)[None]
    return (y_prompt, y_sample.reshape(nbs, seq_s, D_MODEL),
            token_major(kf, FOX_HEADS), token_major(vf, FOX_HEADS), lf.transpose(0, 2, 1)[None],
            nc.reshape(1, B, CONV_K - 1, CONV_DIM),
            token_major(mk_p, MEM_HEADS), token_major(mv_p, MEM_HEADS),
            heads(kfs, nbs, seq_s, FOX_HEADS), heads(vfs, nbs, seq_s, FOX_HEADS),
            lfs.reshape(1, nbs, seq_s, FOX_HEADS), ncs.reshape(1, nbs, CONV_K - 1, CONV_DIM))
```

```python
import functools

import jax
import jax.numpy as jnp
from jax import lax
from jax.experimental import pallas as pl
from jax.experimental.pallas import tpu as pltpu

F32 = jnp.float32
BF16 = jnp.bfloat16

D_MODEL = 1024
HEAD_DIM = 64
CONV_DIM = 256
CONV_K = 3
FOX_HEADS = 8
FOX_DIM = FOX_HEADS * HEAD_DIM
MEM_HEADS = 4
MEM_DIM = MEM_HEADS * HEAD_DIM
N_MEM = 256
D_FF = 2816
EPS = 1e-6

LANES = 128
PAIR = 2 * HEAD_DIM
FF_CHUNK = 256
N_FF_CHUNKS = D_FF // FF_CHUNK
AUG = LANES
AUG_GROUP = AUG // FOX_HEADS
QK_COLS = FOX_DIM + AUG
ATT_COLS = FOX_DIM + MEM_DIM
COL_Q = 3 * CONV_DIM
COL_K = COL_Q + FOX_DIM
COL_V = COL_K + FOX_DIM
COL_F = COL_V + FOX_DIM
COL_QM = COL_F + FOX_HEADS
TAIL_COLS = MEM_DIM + AUG + LANES
NEG = -1e30
SCALE = HEAD_DIM ** -0.5
LOG2E = 1.4426950408889634
VMEM_LIMIT = 56 * 1024 * 1024


def _idiv(x, d):
    if d & (d - 1) == 0:
        return lax.shift_right_logical(x, d.bit_length() - 1)
    return x // d


def _imod(x, d):
    if d & (d - 1) == 0:
        return x & (d - 1)
    return x % d


def _rms(x, g):
    ms = jnp.mean(x * x, axis=-1, keepdims=True)
    return x * lax.rsqrt(ms + EPS) * g


def _pair_headnorm(x, g):
    lo = lax.broadcasted_iota(jnp.int32, (1, PAIR), 1) < HEAD_DIM
    x2 = x * x
    s_lo = jnp.sum(jnp.where(lo, x2, 0.0), axis=-1, keepdims=True)
    s_hi = jnp.sum(jnp.where(lo, 0.0, x2), axis=-1, keepdims=True)
    r = jnp.where(lo, lax.rsqrt(s_lo * (1.0 / HEAD_DIM) + EPS),
                  lax.rsqrt(s_hi * (1.0 / HEAD_DIM) + EPS))
    return x * r * g


def _store_cols(ref, sl, x, transposed):
    if transposed:
        ref[sl, :] = x.T
    else:
        ref[:, sl] = x


def _split3(x):
    hi = x.astype(BF16).astype(F32)
    r = x - hi
    mid = r.astype(BF16).astype(F32)
    lo = (r - mid).astype(BF16).astype(F32)
    return hi, mid, lo


def _ffn(xn, wg_ref, wu_ref, wd_ref):
    acc = None
    for c in range(N_FF_CHUNKS):
        sl = slice(c * FF_CHUNK, (c + 1) * FF_CHUNK)
        g = jnp.dot(xn, wg_ref[:, sl], preferred_element_type=F32)
        u = jnp.dot(xn, wu_ref[:, sl], preferred_element_type=F32)
        a = (g * jax.nn.sigmoid(g) * u).astype(BF16)
        d = jnp.dot(a, wd_ref[sl, :], preferred_element_type=F32)
        acc = d if acc is None else acc + d
    return acc


def _log_sigmoid(x):
    return jnp.minimum(x, 0.0) - jnp.log1p(jnp.exp(-jnp.abs(x)))


def _softmax_rows(s):
    m = jnp.max(s, axis=-1, keepdims=True)
    p = jnp.exp(s - m)
    return p, jnp.sum(p, axis=-1, keepdims=True)


def _dot_nt(a, b):
    return lax.dot_general(a, b, (((1,), (1,)), ((), ())), preferred_element_type=F32)


def _stage1_body(nb, seq, carry, *refs):
    tm = nb * seq
    if carry:
        x_ref, rest = refs[0], refs[1:]
        st_ref = None
    else:
        x_ref, st_ref, rest = refs[0], refs[1], refs[2:]
    (g1_ref, wg_ref, wu_ref, wd_ref, gmix_ref, win_ref, wtail_ref, cw_ref, cb_ref, gq_ref, gk_ref, gqm_ref,
     bfr_ref, bf_ref,
     x1_ref, yc_ref, qa_ref, ka_ref, vb_ref, kf_ref, vf_ref, lf_ref, qm_ref, nc_ref,
     cs_ref, cum_ref) = rest

    x = x_ref[...]
    y = _ffn(_rms(x, g1_ref[...]).astype(BF16), wg_ref, wu_ref, wd_ref)
    x1 = x + 0.5 * y
    x1_ref[...] = x1
    h = _rms(x1, gmix_ref[...]).astype(BF16)

    ucb = jnp.dot(h, win_ref[:, 0:COL_Q], preferred_element_type=F32)
    ci = ucb[:, CONV_DIM:2 * CONV_DIM] * ucb[:, 0:CONV_DIM]
    if carry:
        @pl.when(pl.program_id(1) == 0)
        def _():
            cs_ref[:, 0:8, :] = jnp.zeros((nb, 8, CONV_DIM), F32)
    else:
        cs_ref[:, 8 - (CONV_K - 1):8, :] = st_ref[...]
    cs_ref[:, 8:8 + seq, :] = ci.reshape(nb, seq, CONV_DIM)
    conv = cb_ref[...] + cw_ref[CONV_K - 1:CONV_K, :] * ci
    for i in range(CONV_K - 1):
        shifted = cs_ref[:, 8 - (CONV_K - 1) + i:8 - (CONV_K - 1) + i + seq, :]
        conv = conv + cw_ref[i:i + 1, :] * shifted.reshape(tm, CONV_DIM)
    yc_ref[...] = (ucb[:, 2 * CONV_DIM:3 * CONV_DIM] * conv).astype(BF16)
    tail = cs_ref[:, seq:seq + 8, :]
    nc_ref[...] = tail[:, 8 - (CONV_K - 1):, :].reshape(nc_ref.shape)
    if carry:
        cs_ref[:, 0:8, :] = tail

    for name, col, g_ref, out_ref in (("q", COL_Q, gq_ref, qa_ref), ("k", COL_K, gk_ref, ka_ref)):
        pr = jnp.dot(h, win_ref[:, col:col + FOX_DIM], preferred_element_type=F32)
        for g in range(FOX_HEADS // 2):
            sl = slice(g * PAIR, (g + 1) * PAIR)
            xn = _pair_headnorm(pr[:, sl], g_ref[:, sl])
            if name == "k":
                _store_cols(kf_ref, sl, xn, carry)
            out_ref[:, sl] = xn.astype(BF16)
    v = jnp.dot(h, win_ref[:, COL_V:COL_V + FOX_DIM], preferred_element_type=F32)
    for g in range(FOX_HEADS // 2):
        sl = slice(g * PAIR, (g + 1) * PAIR)
        _store_cols(vf_ref, sl, v[:, sl], carry)
    vb_ref[...] = v.astype(BF16)

    tail_cols = jnp.dot(h, wtail_ref[...], preferred_element_type=F32)
    for g in range(MEM_HEADS // 2):
        sl = slice(g * PAIR, (g + 1) * PAIR)
        qm_ref[:, sl] = _pair_headnorm(tail_cols[:, sl], gqm_ref[:, sl]).astype(BF16)
    fl = tail_cols[:, MEM_DIM:]
    lf = _log_sigmoid(fl[:, AUG:2 * AUG] + bf_ref[...])
    if carry:
        lf_ref[...] = lf.T[0:FOX_HEADS, :]
    else:
        lf_ref[...] = lf[:, 0:FOX_HEADS]
    logf = _log_sigmoid(fl[:, 0:AUG] + bfr_ref[...])

    row = lax.broadcasted_iota(jnp.int32, (tm, tm), 0)
    col = lax.broadcasted_iota(jnp.int32, (tm, tm), 1)
    tri = col <= row
    if nb > 1:
        tri = jnp.logical_and(tri, _idiv(row, seq) == _idiv(col, seq))
    tri = jnp.where(tri, 1.0, 0.0).astype(BF16)
    hi, mid, lo = _split3(logf)
    parts = jnp.concatenate([hi.astype(BF16), mid.astype(BF16), lo.astype(BF16)], axis=1)
    cs = jnp.dot(tri, parts, preferred_element_type=F32)
    cum = cs[:, 0:AUG] + cs[:, AUG:2 * AUG] + cs[:, 2 * AUG:3 * AUG]
    if carry:
        @pl.when(pl.program_id(1) == 0)
        def _():
            cum_ref[...] = jnp.zeros(cum_ref.shape, F32)
        cum = cum + cum_ref[0:1, :]
        cum_ref[0:1, :] = cum[tm - 1:tm, :]

    j = _imod(lax.broadcasted_iota(jnp.int32, (1, AUG), 1), AUG_GROUP)
    chi, cmid, clo = _split3(cum * LOG2E)
    aq = jnp.where(j == 0, chi, jnp.where(j == 1, cmid, jnp.where(j == 2, clo, jnp.where(j < 6, 1.0, 0.0))))
    ak = jnp.where(j < 3, 1.0, jnp.where(j == 3, -chi, jnp.where(j == 4, -cmid, jnp.where(j == 5, -clo, 0.0))))
    qa_ref[:, FOX_DIM:QK_COLS] = aq.astype(BF16)
    ka_ref[:, FOX_DIM:QK_COLS] = ak.astype(BF16)


def _stage1(x3, state, wts, *, nb, seq, carry):
    G, T, _ = x3.shape
    tm = nb * seq
    nt = T // tm
    const = lambda shape: pl.BlockSpec(shape, lambda g, t: (0,) * len(shape), pipeline_mode=pl.Buffered(1))
    tok = lambda cols: pl.BlockSpec((None, tm, cols), lambda g, t: (g, t, 0))
    if carry:
        feat = lambda rows: pl.BlockSpec((None, rows, tm), lambda g, t: (g, 0, t))
        feat_shape = lambda rows: jax.ShapeDtypeStruct((G, rows, T), F32)
    else:
        feat = lambda rows: pl.BlockSpec((None, tm, rows), lambda g, t: (g, t, 0))
        feat_shape = lambda rows: jax.ShapeDtypeStruct((G, T, rows), F32)
    in_specs = [tok(D_MODEL)]
    args = [x3]
    if not carry:
        in_specs.append(pl.BlockSpec((nb, CONV_K - 1, CONV_DIM), lambda g, t: (t, 0, 0)))
        args.append(state)
    names = ("g1", "wg1", "wu1", "wd1", "gmix", "win", "wtail", "conv_w", "conv_b", "gq", "gk", "gqm",
             "bf_rep", "bf")
    for n in names:
        in_specs.append(const(wts[n].shape))
        args.append(wts[n])
    shp = lambda cols, dt: jax.ShapeDtypeStruct((G, T, cols), dt)
    out_shape = [shp(D_MODEL, F32), shp(CONV_DIM, BF16), shp(QK_COLS, BF16), shp(QK_COLS, BF16),
                 shp(FOX_DIM, BF16), feat_shape(FOX_DIM), feat_shape(FOX_DIM), feat_shape(FOX_HEADS),
                 shp(MEM_DIM, BF16)]
    out_specs = [tok(D_MODEL), tok(CONV_DIM), tok(QK_COLS), tok(QK_COLS), tok(FOX_DIM), feat(FOX_DIM),
                 feat(FOX_DIM), feat(FOX_HEADS), tok(MEM_DIM)]
    if carry:
        out_shape.append(jax.ShapeDtypeStruct((G, CONV_K - 1, CONV_DIM), F32))
        out_specs.append(pl.BlockSpec((None, CONV_K - 1, CONV_DIM), lambda g, t: (g, 0, 0)))
    else:
        out_shape.append(jax.ShapeDtypeStruct((nt * nb, CONV_K - 1, CONV_DIM), F32))
        out_specs.append(pl.BlockSpec((nb, CONV_K - 1, CONV_DIM), lambda g, t: (t, 0, 0)))
    return pl.pallas_call(
        functools.partial(_stage1_body, nb, seq, carry),
        grid=(G, nt),
        in_specs=in_specs,
        out_specs=out_specs,
        out_shape=out_shape,
        scratch_shapes=[pltpu.VMEM((nb, seq + 8, CONV_DIM), F32), pltpu.VMEM((8, AUG), F32)],
        compiler_params=pltpu.CompilerParams(
            dimension_semantics=("arbitrary", "arbitrary"), vmem_limit_bytes=VMEM_LIMIT),
        name="stage1_carry" if carry else "stage1_batched",
    )(*args)


def _stage3_body(x1_ref, yc_ref, ya_ref, wo_ref, g2_ref, wg_ref, wu_ref, wd_ref, gfin_ref, y_ref):
    x2 = (x1_ref[...]
          + jnp.dot(yc_ref[...], wo_ref[0:CONV_DIM, :], preferred_element_type=F32)
          + jnp.dot(ya_ref[...], wo_ref[CONV_DIM:, :], preferred_element_type=F32))
    y = _ffn(_rms(x2, g2_ref[...]).astype(BF16), wg_ref, wu_ref, wd_ref)
    y_ref[...] = _rms(x2 + 0.5 * y, gfin_ref[...])


def _stage3(x1, yc, ya, wts, *, tm):
    G, T, _ = x1.shape
    const = lambda shape: pl.BlockSpec(shape, lambda g, t: (0,) * len(shape), pipeline_mode=pl.Buffered(1))
    tok = lambda cols: pl.BlockSpec((None, tm, cols), lambda g, t: (g, t, 0))
    names = ("wo", "g2", "wg2", "wu2", "wd2", "gfin")
    return pl.pallas_call(
        _stage3_body,
        grid=(G, T // tm),
        in_specs=[tok(D_MODEL), tok(CONV_DIM), tok(ATT_COLS)] + [const(wts[n].shape) for n in names],
        out_specs=tok(D_MODEL),
        out_shape=jax.ShapeDtypeStruct((G, T, D_MODEL), F32),
        compiler_params=pltpu.CompilerParams(
            dimension_semantics=("arbitrary", "arbitrary"), vmem_limit_bytes=VMEM_LIMIT),
        name="stage3",
    )(x1, yc, ya, *[wts[n] for n in names])


def _memkv_body(mem_ref, gmem_ref, w_ref, gk_ref, mk_ref, mv_ref, mkb_ref, mvb_ref):
    kv = jnp.dot(_rms(mem_ref[...], gmem_ref[...]).astype(BF16), w_ref[...], preferred_element_type=F32)
    for g in range(MEM_HEADS // 2):
        sl = slice(g * PAIR, (g + 1) * PAIR)
        mk = _pair_headnorm(kv[:, sl], gk_ref[:, sl])
        mk_ref[sl, :] = mk.T
        mkb_ref[:, sl] = mk.astype(BF16)
    mv = kv[:, MEM_DIM:]
    mv_ref[...] = mv.T
    mvb_ref[...] = mv.astype(BF16)


def _memkv(mem, wts):
    B = mem.shape[0]
    const = lambda shape: pl.BlockSpec(shape, lambda b: (0,) * len(shape))
    blk = lambda cols: pl.BlockSpec((None, N_MEM, cols), lambda b: (b, 0, 0))
    shp = lambda dt: jax.ShapeDtypeStruct((B, N_MEM, MEM_DIM), dt)
    names = ("gmem", "wmem", "gkm")
    return pl.pallas_call(
        _memkv_body,
        grid=(B,),
        in_specs=[blk(D_MODEL)] + [const(wts[n].shape) for n in names],
        out_specs=[blk(MEM_DIM)] * 4,
        out_shape=[shp(F32), shp(F32), shp(BF16), shp(BF16)],
        compiler_params=pltpu.CompilerParams(dimension_semantics=("arbitrary",)),
        name="memkv",
    )(mem, *[wts[n] for n in names])


def _head_masks():
    lane = lax.broadcasted_iota(jnp.int32, (1, LANES), 1)
    lo = lane < HEAD_DIM
    return lo, _idiv(lane, AUG_GROUP)


def _prompt_att_body(tq, tk, iq_ref, ik_ref, qa_ref, ka_ref, vb_ref, qm_ref, mk_ref, mv_ref,
                     o_ref, m_ref, acc_ref):
    p = pl.program_id(1)
    iq = iq_ref[p]
    ik = ik_ref[p]
    lo, aug_head = _head_masks()

    @pl.when(ik == 0)
    def _():
        m_ref[...] = jnp.full(m_ref.shape, NEG, F32)
        acc_ref[...] = jnp.zeros(acc_ref.shape, F32)

    def step(masked, r0):
        rows = tq - r0
        if masked:
            qpos = iq * tq + r0 + lax.broadcasted_iota(jnp.int32, (rows, tk), 0)
            kpos = ik * tk + lax.broadcasted_iota(jnp.int32, (rows, tk), 1)
            visible = kpos <= qpos
        q_aug = qa_ref[r0:, FOX_DIM:QK_COLS]
        k_aug = ka_ref[:, FOX_DIM:QK_COLS]
        ones = jnp.ones((tk, LANES), BF16)
        for h in range(FOX_HEADS):
            sl = slice((h // 2) * PAIR, (h // 2 + 1) * PAIR)
            mine = lo if h % 2 == 0 else jnp.logical_not(lo)
            qh = jnp.concatenate([jnp.where(mine, qa_ref[r0:, sl], 0.0).astype(BF16),
                                  jnp.where(aug_head == h, q_aug, 0.0).astype(BF16)], axis=1)
            kh = jnp.concatenate([ka_ref[:, sl], k_aug], axis=1)
            s = _dot_nt(qh, kh)
            if masked:
                s = jnp.where(visible, s, NEG)
            blocks = [s[:, c * LANES:(c + 1) * LANES] for c in range(tk // LANES)]
            blk_max = functools.reduce(jnp.maximum, blocks)
            m_prev = m_ref[h, r0:, :]
            m_new = jnp.maximum(m_prev, jnp.max(blk_max, axis=-1, keepdims=True))
            alpha = jnp.exp2(m_prev - m_new)
            pexp = jnp.concatenate([jnp.exp2(b - m_new) for b in blocks], axis=1).astype(BF16)
            pv = jnp.dot(pexp, jnp.concatenate([vb_ref[:, sl], ones], axis=1), preferred_element_type=F32)
            acc_ref[h, r0:, :] = jnp.concatenate([alpha, alpha], axis=1) * acc_ref[h, r0:, :] + pv
            m_ref[h, r0:, :] = m_new

    first_row = jnp.maximum(ik * tk - iq * tq, 0)
    diag = (ik + 1) * tk - 1 > iq * tq

    @pl.when(jnp.logical_not(diag))
    def _():
        step(False, 0)

    for r0 in range(0, tq, tk):
        @pl.when(jnp.logical_and(diag, first_row == r0))
        def _(r0=r0):
            step(True, r0)

    @pl.when((ik + 1) * tk >= (iq + 1) * tq)
    def _():
        for g in range(FOX_HEADS // 2):
            even = acc_ref[2 * g, :, 0:PAIR] / acc_ref[2 * g, :, PAIR:2 * PAIR]
            odd = acc_ref[2 * g + 1, :, 0:PAIR] / acc_ref[2 * g + 1, :, PAIR:2 * PAIR]
            o_ref[:, g * PAIR:(g + 1) * PAIR] = jnp.where(lo, even, odd).astype(BF16)
        for g in range(MEM_HEADS // 2):
            sl = slice(g * PAIR, (g + 1) * PAIR)
            outs = []
            for mine in (lo, jnp.logical_not(lo)):
                qh = jnp.where(mine, qm_ref[:, sl], 0.0).astype(BF16)
                pexp, lsum = _softmax_rows(_dot_nt(qh, mk_ref[:, sl]) * SCALE)
                outs.append(jnp.dot(pexp.astype(BF16), mv_ref[:, sl], preferred_element_type=F32) / lsum)
            o_ref[:, FOX_DIM + g * PAIR:FOX_DIM + (g + 1) * PAIR] = jnp.where(lo, outs[0], outs[1]).astype(BF16)


def _prompt_attention(qa, ka, vb, qm, mkb, mvb, *, tq, tk):
    B, T, _ = qa.shape
    assert tq % tk == 0 and T % tq == 0
    pairs = [(i, j) for i in range(T // tq) for j in range(T // tk) if j * tk <= i * tq + tq - 1]
    iq_tab = jnp.asarray([p[0] for p in pairs], jnp.int32)
    ik_tab = jnp.asarray([p[1] for p in pairs], jnp.int32)
    grid_spec = pltpu.PrefetchScalarGridSpec(
        num_scalar_prefetch=2,
        grid=(B, len(pairs)),
        in_specs=[
            pl.BlockSpec((None, tq, QK_COLS), lambda b, p, iq, ik: (b, iq[p], 0)),
            pl.BlockSpec((None, tk, QK_COLS), lambda b, p, iq, ik: (b, ik[p], 0)),
            pl.BlockSpec((None, tk, FOX_DIM), lambda b, p, iq, ik: (b, ik[p], 0)),
            pl.BlockSpec((None, tq, MEM_DIM), lambda b, p, iq, ik: (b, iq[p], 0)),
            pl.BlockSpec((None, N_MEM, MEM_DIM), lambda b, p, iq, ik: (b, 0, 0)),
            pl.BlockSpec((None, N_MEM, MEM_DIM), lambda b, p, iq, ik: (b, 0, 0)),
        ],
        out_specs=pl.BlockSpec((None, tq, ATT_COLS), lambda b, p, iq, ik: (b, iq[p], 0)),
        scratch_shapes=[pltpu.VMEM((FOX_HEADS, tq, LANES), F32), pltpu.VMEM((FOX_HEADS, tq, 2 * PAIR), F32)],
    )
    return pl.pallas_call(
        functools.partial(_prompt_att_body, tq, tk),
        grid_spec=grid_spec,
        out_shape=jax.ShapeDtypeStruct((B, T, ATT_COLS), BF16),
        compiler_params=pltpu.CompilerParams(
            dimension_semantics=("arbitrary", "arbitrary"), vmem_limit_bytes=VMEM_LIMIT),
        name="prompt_attention",
    )(iq_tab, ik_tab, qa, ka, vb, qm, mkb, mvb)


def _diag_blocks(x, nh, rows, width):
    head = _idiv(lax.broadcasted_iota(jnp.int32, (1, nh * width), 1), width)
    out = None
    for h in range(nh):
        part = jnp.where(head == h, x[h * rows:(h + 1) * rows, :], 0.0)
        out = part if out is None else out + part
    return out


def _sample_att_body(seq, past, per_step, *refs):
    for i in range(per_step):
        _sample_att_one(seq, past, *[r.at[i] for r in refs])


def _sample_att_one(seq, past, qa_ref, ka_ref, vb_ref, qm_ref, kt_ref, vt_ref, lt_ref, mkt_ref, mvt_ref, o_ref):
    blk = LANES
    nblk = past // blk
    lt = lt_ref[...]
    x = jnp.concatenate([lt[:, b * blk:(b + 1) * blk] for b in range(nblk)], axis=0)
    n = x.shape[0]
    parts = jnp.concatenate(_split3(x), axis=0).astype(BF16)
    src = lax.broadcasted_iota(jnp.int32, (blk, blk), 0)
    dst = lax.broadcasted_iota(jnp.int32, (blk, blk), 1)
    later = jnp.where(src > dst, 1.0, 0.0).astype(BF16)
    loc = jnp.dot(parts, later, preferred_element_type=F32)
    tot = jnp.dot(parts, jnp.ones((blk, blk), BF16), preferred_element_type=F32)
    loc = loc[0:n] + loc[n:2 * n] + loc[2 * n:3 * n]
    tot = tot[0:n] + tot[n:2 * n] + tot[2 * n:3 * n]
    running = jnp.zeros((FOX_HEADS, blk), F32)
    suffix = [None] * nblk
    for b in reversed(range(nblk)):
        rows = slice(b * FOX_HEADS, (b + 1) * FOX_HEADS)
        suffix[b] = loc[rows] + running
        running = running + tot[rows]
    rt = jnp.concatenate(suffix, axis=1) * LOG2E
    bias = jnp.concatenate([jnp.broadcast_to(rt[h:h + 1, :], (seq, past)) for h in range(FOX_HEADS)], axis=0)

    nrow = FOX_HEADS * seq
    lane = lax.broadcasted_iota(jnp.int32, (nrow, QK_COLS), 1)
    lane_head = jnp.where(lane < FOX_DIM, _idiv(lane, HEAD_DIM), _idiv(lane - FOX_DIM, AUG_GROUP))
    row_head = _idiv(lax.broadcasted_iota(jnp.int32, (nrow, QK_COLS), 0), seq)
    qbd = jnp.where(lane_head == row_head, jnp.concatenate([qa_ref[...]] * FOX_HEADS, axis=0), 0.0).astype(BF16)
    aug_row = _imod(lax.broadcasted_iota(jnp.int32, (AUG, past), 0), AUG_GROUP)
    kt = jnp.concatenate([kt_ref[...].astype(BF16), jnp.where(aug_row < 3, 1.0, 0.0).astype(BF16)], axis=0)
    s_past = jnp.dot(qbd, kt, preferred_element_type=F32) + bias
    s_new = _dot_nt(qbd, ka_ref[...])
    qi = _imod(lax.broadcasted_iota(jnp.int32, (nrow, seq), 0), seq)
    kj = lax.broadcasted_iota(jnp.int32, (nrow, seq), 1)
    s_new = jnp.where(kj <= qi, s_new, NEG)
    m = jnp.maximum(jnp.max(s_past, axis=-1, keepdims=True), jnp.max(s_new, axis=-1, keepdims=True))
    p_past = jnp.exp2(s_past - m)
    p_new = jnp.exp2(s_new - m)
    lsum = jnp.sum(p_past, axis=-1, keepdims=True) + jnp.sum(p_new, axis=-1, keepdims=True)
    o = (_dot_nt(p_past.astype(BF16), vt_ref[...].astype(BF16))
         + jnp.dot(p_new.astype(BF16), vb_ref[...], preferred_element_type=F32)) / lsum
    o_ref[:, 0:FOX_DIM] = _diag_blocks(o, FOX_HEADS, seq, HEAD_DIM).astype(BF16)

    nrow_m = MEM_HEADS * seq
    lane_m = _idiv(lax.broadcasted_iota(jnp.int32, (nrow_m, MEM_DIM), 1), HEAD_DIM)
    row_m = _idiv(lax.broadcasted_iota(jnp.int32, (nrow_m, MEM_DIM), 0), seq)
    qbd_m = jnp.where(lane_m == row_m, jnp.concatenate([qm_ref[...]] * MEM_HEADS, axis=0), 0.0).astype(BF16)
    pm, lm = _softmax_rows(jnp.dot(qbd_m, mkt_ref[...].astype(BF16), preferred_element_type=F32) * SCALE)
    om = _dot_nt(pm.astype(BF16), mvt_ref[...].astype(BF16)) / lm
    o_ref[:, FOX_DIM:ATT_COLS] = _diag_blocks(om, MEM_HEADS, seq, HEAD_DIM).astype(BF16)


def _sample_attention(qa, ka, vb, qm, pk, pv, plf, mk, mv):
    nbatch, seq, _ = qa.shape
    past = pk.shape[2]
    per_step = 2
    assert nbatch % per_step == 0
    blk = lambda rows, cols: pl.BlockSpec((per_step, rows, cols), lambda b: (b, 0, 0))
    return pl.pallas_call(
        functools.partial(_sample_att_body, seq, past, per_step),
        grid=(nbatch // per_step,),
        in_specs=[blk(seq, QK_COLS), blk(seq, QK_COLS), blk(seq, FOX_DIM), blk(seq, MEM_DIM),
                  blk(FOX_DIM, past), blk(FOX_DIM, past), blk(FOX_HEADS, past),
                  blk(MEM_DIM, N_MEM), blk(MEM_DIM, N_MEM)],
        out_specs=blk(seq, ATT_COLS),
        out_shape=jax.ShapeDtypeStruct((nbatch, seq, ATT_COLS), BF16),
        compiler_params=pltpu.CompilerParams(
            dimension_semantics=("arbitrary",), vmem_limit_bytes=VMEM_LIMIT),
        name="sample_attention",
    )(qa, ka, vb, qm, pk, pv, plf, mk, mv)


def _prep_weights(norm_ffn1, w_g1, w_u1, w_d1, norm_mix, w_in, b_forget, conv_w, conv_b,
                  q_norm_fox, k_norm_fox, q_norm_mem, k_norm_mem, norm_mem, w_mem_kv, w_out,
                  norm_ffn2, w_g2, w_u2, w_d2, norm_final):
    w_f = w_in[:, COL_F:COL_QM]
    wtail = jnp.concatenate(
        [w_in[:, COL_QM:], jnp.repeat(w_f, AUG_GROUP, axis=1), w_f,
         jnp.zeros((D_MODEL, LANES - FOX_HEADS), F32)], axis=1).astype(BF16)
    row = lambda v: v.reshape(1, -1).astype(F32)
    return {
        "g1": row(norm_ffn1), "wg1": w_g1.astype(BF16), "wu1": w_u1.astype(BF16), "wd1": w_d1.astype(BF16),
        "gmix": row(norm_mix), "win": w_in.astype(BF16), "wtail": wtail,
        "conv_w": conv_w.astype(F32), "conv_b": row(conv_b),
        "gq": row(jnp.tile(q_norm_fox, FOX_HEADS)) * (SCALE * LOG2E), "gk": row(jnp.tile(k_norm_fox, FOX_HEADS)),
        "gqm": row(jnp.tile(q_norm_mem, MEM_HEADS)),
        "bf_rep": row(jnp.repeat(b_forget, AUG_GROUP)),
        "bf": jnp.pad(row(b_forget), ((0, 0), (0, LANES - FOX_HEADS))),
        "gmem": row(norm_mem), "wmem": w_mem_kv.astype(BF16), "gkm": row(jnp.tile(k_norm_mem, MEM_HEADS)),
        "wo": w_out.astype(BF16), "g2": row(norm_ffn2), "wg2": w_g2.astype(BF16), "wu2": w_u2.astype(BF16),
        "wd2": w_d2.astype(BF16), "gfin": row(norm_final),
    }


def kernel(x_prompt, x_sample, cache_fox_k, cache_fox_v, cache_fox_logf, state_conv, cache_mem_k, cache_mem_v, mem_prompt, norm_ffn1, w_ffn1_gate, w_ffn1_up, w_ffn1_down, norm_mix, w_in, b_forget, conv_w, conv_b, q_norm_fox, k_norm_fox, q_norm_mem, k_norm_mem, norm_mem, w_mem_kv, w_out, norm_ffn2, w_ffn2_gate, w_ffn2_up, w_ffn2_down, norm_final):
    depth = w_in.shape[0]
    assert depth == 1, "single-layer step"
    B, T, _ = x_prompt.shape
    nbs, seq_s, _ = x_sample.shape
    past = cache_fox_k.shape[2]
    l = 0
    wts = _prep_weights(norm_ffn1[l], w_ffn1_gate[l], w_ffn1_up[l], w_ffn1_down[l], norm_mix[l], w_in[l],
                        b_forget[l], conv_w[l], conv_b[l], q_norm_fox[l], k_norm_fox[l], q_norm_mem[l],
                        k_norm_mem[l], norm_mem[l], w_mem_kv[l], w_out[l], norm_ffn2[l], w_ffn2_gate[l],
                        w_ffn2_up[l], w_ffn2_down[l], norm_final[l])
    tile = 512

    mk_p, mv_p, mkb, mvb = _memkv(mem_prompt, wts)
    x1, yc, qa, ka, vb, kf, vf, lf, qm, nc = _stage1(x_prompt, None, wts, nb=1, seq=tile, carry=True)
    ya = _prompt_attention(qa, ka, vb, qm, mkb, mvb, tq=2 * tile, tk=tile)
    y_prompt = _stage3(x1, yc, ya, wts, tm=tile)

    xs = x_sample.reshape(1, nbs * seq_s, D_MODEL)
    x1s, ycs, qas, kas, vbs, kfs, vfs, lfs, qms, ncs = _stage1(
        xs, state_conv[l], wts, nb=tile // seq_s, seq=seq_s, carry=False)
    per_seq = lambda a: a.reshape(nbs, seq_s, a.shape[-1])
    feat_major = lambda a: a.reshape(a.shape[0], a.shape[1], -1).transpose(0, 2, 1)
    yas = _sample_attention(
        per_seq(qas), per_seq(kas), per_seq(vbs), per_seq(qms),
        feat_major(cache_fox_k[l]), feat_major(cache_fox_v[l]), feat_major(cache_fox_logf[l]),
        feat_major(cache_mem_k[l]), feat_major(cache_mem_v[l]))
    y_sample = _stage3(x1s, ycs, yas.reshape(1, nbs * seq_s, ATT_COLS), wts, tm=tile)

    heads = lambda a, b, t, nh: a.reshape(1, b, t, nh, HEAD_DIM)
    token_major = lambda a, nh: a.reshape(a.shape[0], nh, -1, a.shape[2]).transpose(0, 3, 1, 2)[None]
    return (y_prompt, y_sample.reshape(nbs, seq_s, D_MODEL),
            token_major(kf, FOX_HEADS), token_major(vf, FOX_HEADS), lf.transpose(0, 2, 1)[None],
            nc.reshape(1, B, CONV_K - 1, CONV_DIM),
            token_major(mk_p, MEM_HEADS), token_major(mv_p, MEM_HEADS),
            heads(kfs, nbs, seq_s, FOX_HEADS), heads(vfs, nbs, seq_s, FOX_HEADS),
            lfs.reshape(1, nbs, seq_s, FOX_HEADS), ncs.reshape(1, nbs, CONV_K - 1, CONV_DIM))
```

```python
import functools

import jax
import jax.numpy as jnp
from jax import lax
from jax.experimental import pallas as pl
from jax.experimental.pallas import tpu as pltpu

F32 = jnp.float32
BF16 = jnp.bfloat16

D_MODEL = 1024
HEAD_DIM = 64
CONV_DIM = 256
CONV_K = 3
FOX_HEADS = 8
FOX_DIM = FOX_HEADS * HEAD_DIM
MEM_HEADS = 4
MEM_DIM = MEM_HEADS * HEAD_DIM
N_MEM = 256
D_FF = 2816
EPS = 1e-6

LANES = 128
PAIR = 2 * HEAD_DIM
FF_CHUNK = 256
N_FF_CHUNKS = D_FF // FF_CHUNK
AUG = LANES
AUG_GROUP = AUG // FOX_HEADS
QK_COLS = FOX_DIM + AUG
ATT_COLS = FOX_DIM + MEM_DIM
COL_Q = 3 * CONV_DIM
COL_K = COL_Q + FOX_DIM
COL_V = COL_K + FOX_DIM
COL_F = COL_V + FOX_DIM
COL_QM = COL_F + FOX_HEADS
TAIL_COLS = MEM_DIM + AUG + LANES
NEG = -1e30
SCALE = HEAD_DIM ** -0.5
LOG2E = 1.4426950408889634
VMEM_LIMIT = 56 * 1024 * 1024


def _idiv(x, d):
    if d & (d - 1) == 0:
        return lax.shift_right_logical(x, d.bit_length() - 1)
    return x // d


def _imod(x, d):
    if d & (d - 1) == 0:
        return x & (d - 1)
    return x % d


def _rms(x, g):
    ms = jnp.mean(x * x, axis=-1, keepdims=True)
    return x * lax.rsqrt(ms + EPS) * g


def _pair_headnorm(x, g):
    lo = lax.broadcasted_iota(jnp.int32, (1, PAIR), 1) < HEAD_DIM
    x2 = x * x
    s_lo = jnp.sum(jnp.where(lo, x2, 0.0), axis=-1, keepdims=True)
    s_hi = jnp.sum(jnp.where(lo, 0.0, x2), axis=-1, keepdims=True)
    r = jnp.where(lo, lax.rsqrt(s_lo * (1.0 / HEAD_DIM) + EPS),
                  lax.rsqrt(s_hi * (1.0 / HEAD_DIM) + EPS))
    return x * r * g


def _store_cols(ref, sl, x, transposed):
    if transposed:
        ref[sl, :] = x.T
    else:
        ref[:, sl] = x


def _split3(x):
    hi = x.astype(BF16).astype(F32)
    r = x - hi
    mid = r.astype(BF16).astype(F32)
    lo = (r - mid).astype(BF16).astype(F32)
    return hi, mid, lo


def _ffn(xn, wg_ref, wu_ref, wd_ref):
    acc = None
    for c in range(N_FF_CHUNKS):
        sl = slice(c * FF_CHUNK, (c + 1) * FF_CHUNK)
        g = jnp.dot(xn, wg_ref[:, sl], preferred_element_type=F32)
        u = jnp.dot(xn, wu_ref[:, sl], preferred_element_type=F32)
        a = (g * jax.nn.sigmoid(g) * u).astype(BF16)
        d = jnp.dot(a, wd_ref[sl, :], preferred_element_type=F32)
        acc = d if acc is None else acc + d
    return acc


def _log_sigmoid(x):
    return jnp.minimum(x, 0.0) - jnp.log1p(jnp.exp(-jnp.abs(x)))


def _softmax_rows(s):
    m = jnp.max(s, axis=-1, keepdims=True)
    p = jnp.exp(s - m)
    return p, jnp.sum(p, axis=-1, keepdims=True)


def _dot_nt(a, b):
    return lax.dot_general(a, b, (((1,), (1,)), ((), ())), preferred_element_type=F32)


_STAGE1_SHARED_IN = ("gmix", "wtail", "conv_w", "conv_b", "gq", "gk", "gqm", "bf_rep", "bf")
_STAGE1_SHARED_OUT = ("yc", "qa", "ka", "vb", "kf", "vf", "lf", "qm", "nc")
_STAGE1_IN = {True: ("x", "g1", "wg1", "wu1", "wd1", "win") + _STAGE1_SHARED_IN,
              False: ("x1", "state", "win32t") + _STAGE1_SHARED_IN}
_STAGE1_OUT = {True: ("x1",) + _STAGE1_SHARED_OUT, False: _STAGE1_SHARED_OUT + ("winb",)}


def _stage1_body(nb, seq, carry, *refs):
    tm = nb * seq
    names = _STAGE1_IN[carry] + _STAGE1_OUT[carry]
    r = dict(zip(names, refs[:len(names)]))
    cs_ref, cum_ref = refs[len(names):]
    gmix_ref, wtail_ref, cw_ref, cb_ref, gq_ref, gk_ref, gqm_ref, bfr_ref, bf_ref = (
        r[n] for n in _STAGE1_SHARED_IN)
    yc_ref, qa_ref, ka_ref, vb_ref, kf_ref, vf_ref, lf_ref, qm_ref, nc_ref = (r[n] for n in _STAGE1_SHARED_OUT)
    if carry:
        x = r["x"][...]
        y = _ffn(_rms(x, r["g1"][...]).astype(BF16), r["wg1"], r["wu1"], r["wd1"])
        x1 = x + 0.5 * y
        r["x1"][...] = x1
        win_ref = r["win"]
    else:
        x1 = r["x1"][...]
        st_ref = r["state"]
        win_ref = r["winb"]

        @pl.when(pl.program_id(1) == 0)
        def _():
            win_ref[...] = r["win32t"][0:COL_F, :].T.astype(BF16)
    h = _rms(x1, gmix_ref[...]).astype(BF16)

    ucb = jnp.dot(h, win_ref[:, 0:COL_Q], preferred_element_type=F32)
    ci = ucb[:, CONV_DIM:2 * CONV_DIM] * ucb[:, 0:CONV_DIM]
    if carry:
        @pl.when(pl.program_id(1) == 0)
        def _():
            cs_ref[:, 0:8, :] = jnp.zeros((nb, 8, CONV_DIM), F32)
    else:
        cs_ref[:, 8 - (CONV_K - 1):8, :] = st_ref[...]
    cs_ref[:, 8:8 + seq, :] = ci.reshape(nb, seq, CONV_DIM)
    conv = cb_ref[...] + cw_ref[CONV_K - 1:CONV_K, :] * ci
    for i in range(CONV_K - 1):
        shifted = cs_ref[:, 8 - (CONV_K - 1) + i:8 - (CONV_K - 1) + i + seq, :]
        conv = conv + cw_ref[i:i + 1, :] * shifted.reshape(tm, CONV_DIM)
    yc_ref[...] = (ucb[:, 2 * CONV_DIM:3 * CONV_DIM] * conv).astype(BF16)
    tail = cs_ref[:, seq:seq + 8, :]
    nc_ref[...] = tail[:, 8 - (CONV_K - 1):, :].reshape(nc_ref.shape)
    if carry:
        cs_ref[:, 0:8, :] = tail

    for name, col, g_ref, out_ref in (("q", COL_Q, gq_ref, qa_ref), ("k", COL_K, gk_ref, ka_ref)):
        pr = jnp.dot(h, win_ref[:, col:col + FOX_DIM], preferred_element_type=F32)
        for g in range(FOX_HEADS // 2):
            sl = slice(g * PAIR, (g + 1) * PAIR)
            xn = _pair_headnorm(pr[:, sl], g_ref[:, sl])
            if name == "k":
                _store_cols(kf_ref, sl, xn, carry)
            out_ref[:, sl] = xn.astype(BF16)
    v = jnp.dot(h, win_ref[:, COL_V:COL_V + FOX_DIM], preferred_element_type=F32)
    for g in range(FOX_HEADS // 2):
        sl = slice(g * PAIR, (g + 1) * PAIR)
        _store_cols(vf_ref, sl, v[:, sl], carry)
    vb_ref[...] = v.astype(BF16)

    tail_cols = jnp.dot(h, wtail_ref[...], preferred_element_type=F32)
    for g in range(MEM_HEADS // 2):
        sl = slice(g * PAIR, (g + 1) * PAIR)
        qm_ref[:, sl] = _pair_headnorm(tail_cols[:, sl], gqm_ref[:, sl]).astype(BF16)
    fl = tail_cols[:, MEM_DIM:]
    lf = _log_sigmoid(fl[:, AUG:2 * AUG] + bf_ref[...])
    if carry:
        lf_ref[...] = lf.T[0:FOX_HEADS, :]
    else:
        lf_ref[...] = lf[:, 0:FOX_HEADS]
    logf = _log_sigmoid(fl[:, 0:AUG] + bfr_ref[...])

    row = lax.broadcasted_iota(jnp.int32, (tm, tm), 0)
    col = lax.broadcasted_iota(jnp.int32, (tm, tm), 1)
    tri = col <= row
    if nb > 1:
        tri = jnp.logical_and(tri, _idiv(row, seq) == _idiv(col, seq))
    tri = jnp.where(tri, 1.0, 0.0).astype(BF16)
    hi, mid, lo = _split3(logf)
    parts = jnp.concatenate([hi.astype(BF16), mid.astype(BF16), lo.astype(BF16)], axis=1)
    cs = jnp.dot(tri, parts, preferred_element_type=F32)
    cum = cs[:, 0:AUG] + cs[:, AUG:2 * AUG] + cs[:, 2 * AUG:3 * AUG]
    if carry:
        @pl.when(pl.program_id(1) == 0)
        def _():
            cum_ref[...] = jnp.zeros(cum_ref.shape, F32)
        cum = cum + cum_ref[0:1, :]
        cum_ref[0:1, :] = cum[tm - 1:tm, :]

    j = _imod(lax.broadcasted_iota(jnp.int32, (1, AUG), 1), AUG_GROUP)
    chi, cmid, clo = _split3(cum * LOG2E)
    aq = jnp.where(j == 0, chi, jnp.where(j == 1, cmid, jnp.where(j == 2, clo, jnp.where(j < 6, 1.0, 0.0))))
    ak = jnp.where(j < 3, 1.0, jnp.where(j == 3, -chi, jnp.where(j == 4, -cmid, jnp.where(j == 5, -clo, 0.0))))
    qa_ref[:, FOX_DIM:QK_COLS] = aq.astype(BF16)
    ka_ref[:, FOX_DIM:QK_COLS] = ak.astype(BF16)


def _stage1(x3, state, wts, *, nb, seq, carry):
    G, T, _ = x3.shape
    tm = nb * seq
    nt = T // tm
    const = lambda shape: pl.BlockSpec(shape, lambda g, t: (0,) * len(shape), pipeline_mode=pl.Buffered(1))
    tok = lambda cols: pl.BlockSpec((None, tm, cols), lambda g, t: (g, t, 0))
    if carry:
        feat = lambda rows: pl.BlockSpec((None, rows, tm), lambda g, t: (g, 0, t))
        feat_shape = lambda rows: jax.ShapeDtypeStruct((G, rows, T), F32)
    else:
        feat = lambda rows: pl.BlockSpec((None, tm, rows), lambda g, t: (g, t, 0))
        feat_shape = lambda rows: jax.ShapeDtypeStruct((G, T, rows), F32)
    shp = lambda cols, dt: jax.ShapeDtypeStruct((G, T, cols), dt)
    if carry:
        nc_spec = pl.BlockSpec((None, CONV_K - 1, CONV_DIM), lambda g, t: (g, 0, 0))
        nc_shape = jax.ShapeDtypeStruct((G, CONV_K - 1, CONV_DIM), F32)
    else:
        nc_spec = pl.BlockSpec((nb, CONV_K - 1, CONV_DIM), lambda g, t: (t, 0, 0))
        nc_shape = jax.ShapeDtypeStruct((nt * nb, CONV_K - 1, CONV_DIM), F32)
    tiled_in = {"x": (tok(D_MODEL), x3), "x1": (tok(D_MODEL), x3), "state": (nc_spec, state)}
    in_specs, args = [], []
    for n in _STAGE1_IN[carry]:
        spec, arg = tiled_in[n] if n in tiled_in else (const(wts[n].shape), wts[n])
        in_specs.append(spec)
        args.append(arg)
    outs = {"x1": (tok(D_MODEL), shp(D_MODEL, F32)), "yc": (tok(CONV_DIM), shp(CONV_DIM, BF16)),
            "qa": (tok(QK_COLS), shp(QK_COLS, BF16)), "ka": (tok(QK_COLS), shp(QK_COLS, BF16)),
            "vb": (tok(FOX_DIM), shp(FOX_DIM, BF16)), "kf": (feat(FOX_DIM), feat_shape(FOX_DIM)),
            "vf": (feat(FOX_DIM), feat_shape(FOX_DIM)), "lf": (feat(FOX_HEADS), feat_shape(FOX_HEADS)),
            "qm": (tok(MEM_DIM), shp(MEM_DIM, BF16)), "nc": (nc_spec, nc_shape)}
    if not carry:
        w_shape = (D_MODEL, COL_F)
        outs["winb"] = (pl.BlockSpec(w_shape, lambda g, t: (0, 0)), jax.ShapeDtypeStruct(w_shape, BF16))
    out_specs = [outs[n][0] for n in _STAGE1_OUT[carry]]
    out_shape = [outs[n][1] for n in _STAGE1_OUT[carry]]
    return pl.pallas_call(
        functools.partial(_stage1_body, nb, seq, carry),
        grid=(G, nt),
        in_specs=in_specs,
        out_specs=out_specs,
        out_shape=out_shape,
        scratch_shapes=[pltpu.VMEM((nb, seq + 8, CONV_DIM), F32), pltpu.VMEM((8, AUG), F32)],
        compiler_params=pltpu.CompilerParams(
            dimension_semantics=("arbitrary", "arbitrary"), vmem_limit_bytes=VMEM_LIMIT),
        name="stage1_carry" if carry else "stage1_batched",
    )(*args)


def _stage3_body(x1_ref, yc_ref, ya_ref, wo_ref, g2_ref, wg_ref, wu_ref, wd_ref, gfin_ref, y_ref):
    x2 = (x1_ref[...]
          + jnp.dot(yc_ref[...], wo_ref[0:CONV_DIM, :], preferred_element_type=F32)
          + jnp.dot(ya_ref[...], wo_ref[CONV_DIM:, :], preferred_element_type=F32))
    y = _ffn(_rms(x2, g2_ref[...]).astype(BF16), wg_ref, wu_ref, wd_ref)
    y_ref[...] = _rms(x2 + 0.5 * y, gfin_ref[...])


def _stage3(x1, yc, ya, wts, *, tm):
    G, T, _ = x1.shape
    const = lambda shape: pl.BlockSpec(shape, lambda g, t: (0,) * len(shape), pipeline_mode=pl.Buffered(1))
    tok = lambda cols: pl.BlockSpec((None, tm, cols), lambda g, t: (g, t, 0))
    names = ("wo", "g2", "wg2", "wu2", "wd2", "gfin")
    return pl.pallas_call(
        _stage3_body,
        grid=(G, T // tm),
        in_specs=[tok(D_MODEL), tok(CONV_DIM), tok(ATT_COLS)] + [const(wts[n].shape) for n in names],
        out_specs=tok(D_MODEL),
        out_shape=jax.ShapeDtypeStruct((G, T, D_MODEL), F32),
        compiler_params=pltpu.CompilerParams(
            dimension_semantics=("arbitrary", "arbitrary"), vmem_limit_bytes=VMEM_LIMIT),
        name="stage3",
    )(x1, yc, ya, *[wts[n] for n in names])


def _ffn_stream_body(with_out_proj, with_final_norm, *refs):
    refs = list(refs)
    n_x = 4 if with_out_proj else 1
    x_refs, refs = refs[:n_x], refs[n_x:]
    g_ref, wg_ref, wu_ref, wd_ref = refs[:4]
    refs = refs[4:]
    gfin_ref = refs.pop(0) if with_final_norm else None
    y_ref, wgb_ref, wub_ref, wdb_ref, x_ref, xn_ref, acc_ref = refs
    c = pl.program_id(0)

    @pl.when(c == 0)
    def _():
        if with_out_proj:
            x1_ref, yc_ref, ya_ref, wo_ref = x_refs
            x = (x1_ref[...]
                 + jnp.dot(yc_ref[...], wo_ref[0:CONV_DIM, :], preferred_element_type=F32)
                 + jnp.dot(ya_ref[...], wo_ref[CONV_DIM:, :], preferred_element_type=F32))
        else:
            x = x_refs[0][...]
        x_ref[...] = x
        xn_ref[...] = _rms(x, g_ref[...]).astype(BF16)
        acc_ref[...] = jnp.zeros(acc_ref.shape, F32)

    wg = wg_ref[...].astype(BF16)
    wu = wu_ref[...].astype(BF16)
    wd = wd_ref[...].astype(BF16)
    wgb_ref[...] = wg
    wub_ref[...] = wu
    wdb_ref[...] = wd
    xn = xn_ref[...]
    g = jnp.dot(xn, wg, preferred_element_type=F32)
    u = jnp.dot(xn, wu, preferred_element_type=F32)
    a = (g * jax.nn.sigmoid(g) * u).astype(BF16)
    acc_ref[...] += jnp.dot(a, wd, preferred_element_type=F32)

    @pl.when(c == pl.num_programs(0) - 1)
    def _():
        y = x_ref[...] + 0.5 * acc_ref[...]
        y_ref[...] = _rms(y, gfin_ref[...]) if with_final_norm else y


def _ffn_stream(x_parts, gain, wg, wu, wd, *, wo=None, gfin=None):
    m = x_parts[0].shape[0]
    const = lambda a: pl.BlockSpec(a.shape, lambda c: (0,) * a.ndim, pipeline_mode=pl.Buffered(1))
    col_chunk = pl.BlockSpec((D_MODEL, FF_CHUNK), lambda c: (0, c))
    row_chunk = pl.BlockSpec((FF_CHUNK, D_MODEL), lambda c: (c, 0))
    args = list(x_parts) + ([wo] if wo is not None else []) + [gain]
    in_specs = [const(a) for a in args] + [col_chunk, col_chunk, row_chunk]
    args += [wg, wu, wd]
    if gfin is not None:
        in_specs.append(const(gfin))
        args.append(gfin)
    return pl.pallas_call(
        functools.partial(_ffn_stream_body, wo is not None, gfin is not None),
        grid=(N_FF_CHUNKS,),
        in_specs=in_specs,
        out_specs=[pl.BlockSpec((m, D_MODEL), lambda c: (0, 0)), col_chunk, col_chunk, row_chunk],
        out_shape=[jax.ShapeDtypeStruct((m, D_MODEL), F32), jax.ShapeDtypeStruct(wg.shape, BF16),
                   jax.ShapeDtypeStruct(wu.shape, BF16), jax.ShapeDtypeStruct(wd.shape, BF16)],
        scratch_shapes=[pltpu.VMEM((m, D_MODEL), F32), pltpu.VMEM((m, D_MODEL), BF16),
                        pltpu.VMEM((m, D_MODEL), F32)],
        compiler_params=pltpu.CompilerParams(dimension_semantics=("arbitrary",), vmem_limit_bytes=VMEM_LIMIT),
        name="ffn_stream_out" if wo is not None else "ffn_stream_in",
    )(*args)


def _memkv_body(mem_ref, gmem_ref, w_ref, gk_ref, mk_ref, mv_ref, mkb_ref, mvb_ref):
    kv = jnp.dot(_rms(mem_ref[...], gmem_ref[...]).astype(BF16), w_ref[...], preferred_element_type=F32)
    for g in range(MEM_HEADS // 2):
        sl = slice(g * PAIR, (g + 1) * PAIR)
        mk = _pair_headnorm(kv[:, sl], gk_ref[:, sl])
        mk_ref[sl, :] = mk.T
        mkb_ref[:, sl] = mk.astype(BF16)
    mv = kv[:, MEM_DIM:]
    mv_ref[...] = mv.T
    mvb_ref[...] = mv.astype(BF16)


def _memkv(mem, wts):
    B = mem.shape[0]
    const = lambda shape: pl.BlockSpec(shape, lambda b: (0,) * len(shape))
    blk = lambda cols: pl.BlockSpec((None, N_MEM, cols), lambda b: (b, 0, 0))
    shp = lambda dt: jax.ShapeDtypeStruct((B, N_MEM, MEM_DIM), dt)
    names = ("gmem", "wmem", "gkm")
    return pl.pallas_call(
        _memkv_body,
        grid=(B,),
        in_specs=[blk(D_MODEL)] + [const(wts[n].shape) for n in names],
        out_specs=[blk(MEM_DIM)] * 4,
        out_shape=[shp(F32), shp(F32), shp(BF16), shp(BF16)],
        compiler_params=pltpu.CompilerParams(dimension_semantics=("arbitrary",)),
        name="memkv",
    )(mem, *[wts[n] for n in names])


def _head_masks():
    lane = lax.broadcasted_iota(jnp.int32, (1, LANES), 1)
    lo = lane < HEAD_DIM
    return lo, _idiv(lane, AUG_GROUP)


def _prompt_att_body(tq, tk, iq_ref, ik_ref, qa_ref, ka_ref, vb_ref, qm_ref, mk_ref, mv_ref,
                     o_ref, m_ref, acc_ref):
    p = pl.program_id(1)
    iq = iq_ref[p]
    ik = ik_ref[p]
    lo, aug_head = _head_masks()

    @pl.when(ik == 0)
    def _():
        m_ref[...] = jnp.full(m_ref.shape, NEG, F32)
        acc_ref[...] = jnp.zeros(acc_ref.shape, F32)

    def step(masked, r0):
        rows = tq - r0
        if masked:
            visible = (lax.broadcasted_iota(jnp.int32, (tk, tk), 1)
                       <= lax.broadcasted_iota(jnp.int32, (tk, tk), 0))
        q_aug = qa_ref[r0:, FOX_DIM:QK_COLS]
        k_aug = ka_ref[:, FOX_DIM:QK_COLS]
        ones = jnp.ones((tk, LANES), BF16)
        for h in range(FOX_HEADS):
            sl = slice((h // 2) * PAIR, (h // 2 + 1) * PAIR)
            mine = lo if h % 2 == 0 else jnp.logical_not(lo)
            qh = jnp.concatenate([jnp.where(mine, qa_ref[r0:, sl], 0.0).astype(BF16),
                                  jnp.where(aug_head == h, q_aug, 0.0).astype(BF16)], axis=1)
            kh = jnp.concatenate([ka_ref[:, sl], k_aug], axis=1)
            s = _dot_nt(qh, kh)
            if masked:
                top = jnp.where(visible, s[:tk], NEG)
                s = top if rows == tk else jnp.concatenate([top, s[tk:]], axis=0)
            blocks = [s[:, c * LANES:(c + 1) * LANES] for c in range(tk // LANES)]
            blk_max = functools.reduce(jnp.maximum, blocks)
            m_prev = m_ref[h, r0:, :]
            m_new = jnp.maximum(m_prev, jnp.max(blk_max, axis=-1, keepdims=True))
            alpha = jnp.exp2(m_prev - m_new)
            pexp = jnp.concatenate([jnp.exp2(b - m_new) for b in blocks], axis=1).astype(BF16)
            pv = jnp.dot(pexp, jnp.concatenate([vb_ref[:, sl], ones], axis=1), preferred_element_type=F32)
            acc_ref[h, r0:, :] = jnp.concatenate([alpha, alpha], axis=1) * acc_ref[h, r0:, :] + pv
            m_ref[h, r0:, :] = m_new

    first_row = jnp.maximum(ik * tk - iq * tq, 0)
    diag = (ik + 1) * tk - 1 > iq * tq

    @pl.when(jnp.logical_not(diag))
    def _():
        step(False, 0)

    for r0 in range(0, tq, tk):
        @pl.when(jnp.logical_and(diag, first_row == r0))
        def _(r0=r0):
            step(True, r0)

    @pl.when((ik + 1) * tk >= (iq + 1) * tq)
    def _():
        for g in range(FOX_HEADS // 2):
            even = acc_ref[2 * g, :, 0:PAIR] / acc_ref[2 * g, :, PAIR:2 * PAIR]
            odd = acc_ref[2 * g + 1, :, 0:PAIR] / acc_ref[2 * g + 1, :, PAIR:2 * PAIR]
            o_ref[:, g * PAIR:(g + 1) * PAIR] = jnp.where(lo, even, odd).astype(BF16)
        for g in range(MEM_HEADS // 2):
            sl = slice(g * PAIR, (g + 1) * PAIR)
            outs = []
            for mine in (lo, jnp.logical_not(lo)):
                qh = jnp.where(mine, qm_ref[:, sl], 0.0).astype(BF16)
                pexp, lsum = _softmax_rows(_dot_nt(qh, mk_ref[:, sl]) * SCALE)
                outs.append(jnp.dot(pexp.astype(BF16), mv_ref[:, sl], preferred_element_type=F32) / lsum)
            o_ref[:, FOX_DIM + g * PAIR:FOX_DIM + (g + 1) * PAIR] = jnp.where(lo, outs[0], outs[1]).astype(BF16)


def _prompt_attention(qa, ka, vb, qm, mkb, mvb, *, tq, tk):
    B, T, _ = qa.shape
    assert tq % tk == 0 and T % tq == 0
    pairs = [(i, j) for i in range(T // tq) for j in range(T // tk) if j * tk <= i * tq + tq - 1]
    iq_tab = jnp.asarray([p[0] for p in pairs], jnp.int32)
    ik_tab = jnp.asarray([p[1] for p in pairs], jnp.int32)
    grid_spec = pltpu.PrefetchScalarGridSpec(
        num_scalar_prefetch=2,
        grid=(B, len(pairs)),
        in_specs=[
            pl.BlockSpec((None, tq, QK_COLS), lambda b, p, iq, ik: (b, iq[p], 0)),
            pl.BlockSpec((None, tk, QK_COLS), lambda b, p, iq, ik: (b, ik[p], 0)),
            pl.BlockSpec((None, tk, FOX_DIM), lambda b, p, iq, ik: (b, ik[p], 0)),
            pl.BlockSpec((None, tq, MEM_DIM), lambda b, p, iq, ik: (b, iq[p], 0)),
            pl.BlockSpec((None, N_MEM, MEM_DIM), lambda b, p, iq, ik: (b, 0, 0)),
            pl.BlockSpec((None, N_MEM, MEM_DIM), lambda b, p, iq, ik: (b, 0, 0)),
        ],
        out_specs=pl.BlockSpec((None, tq, ATT_COLS), lambda b, p, iq, ik: (b, iq[p], 0)),
        scratch_shapes=[pltpu.VMEM((FOX_HEADS, tq, LANES), F32), pltpu.VMEM((FOX_HEADS, tq, 2 * PAIR), F32)],
    )
    return pl.pallas_call(
        functools.partial(_prompt_att_body, tq, tk),
        grid_spec=grid_spec,
        out_shape=jax.ShapeDtypeStruct((B, T, ATT_COLS), BF16),
        compiler_params=pltpu.CompilerParams(
            dimension_semantics=("arbitrary", "arbitrary"), vmem_limit_bytes=VMEM_LIMIT),
        name="prompt_attention",
    )(iq_tab, ik_tab, qa, ka, vb, qm, mkb, mvb)


def _diag_blocks(x, nh, rows, width):
    head = _idiv(lax.broadcasted_iota(jnp.int32, (1, nh * width), 1), width)
    out = None
    for h in range(nh):
        part = jnp.where(head == h, x[h * rows:(h + 1) * rows, :], 0.0)
        out = part if out is None else out + part
    return out


def _sample_att_body(seq, past, per_step, *refs):
    for i in range(per_step):
        _sample_att_one(seq, past, *[r.at[i] for r in refs])


def _sample_att_one(seq, past, qa_ref, ka_ref, vb_ref, qm_ref, kt_ref, vt_ref, lt_ref, mkt_ref, mvt_ref, o_ref):
    blk = LANES
    nblk = past // blk
    lt = lt_ref[...]
    x = jnp.concatenate([lt[:, b * blk:(b + 1) * blk] for b in range(nblk)], axis=0)
    n = x.shape[0]
    parts = jnp.concatenate(_split3(x), axis=0).astype(BF16)
    src = lax.broadcasted_iota(jnp.int32, (blk, blk), 0)
    dst = lax.broadcasted_iota(jnp.int32, (blk, blk), 1)
    later = jnp.where(src > dst, 1.0, 0.0).astype(BF16)
    loc = jnp.dot(parts, later, preferred_element_type=F32)
    tot = jnp.dot(parts, jnp.ones((blk, blk), BF16), preferred_element_type=F32)
    loc = loc[0:n] + loc[n:2 * n] + loc[2 * n:3 * n]
    tot = tot[0:n] + tot[n:2 * n] + tot[2 * n:3 * n]
    running = jnp.zeros((FOX_HEADS, blk), F32)
    suffix = [None] * nblk
    for b in reversed(range(nblk)):
        rows = slice(b * FOX_HEADS, (b + 1) * FOX_HEADS)
        suffix[b] = loc[rows] + running
        running = running + tot[rows]
    rt = jnp.concatenate(suffix, axis=1) * LOG2E
    bias = jnp.concatenate([jnp.broadcast_to(rt[h:h + 1, :], (seq, past)) for h in range(FOX_HEADS)], axis=0)

    nrow = FOX_HEADS * seq
    lane = lax.broadcasted_iota(jnp.int32, (nrow, QK_COLS), 1)
    lane_head = jnp.where(lane < FOX_DIM, _idiv(lane, HEAD_DIM), _idiv(lane - FOX_DIM, AUG_GROUP))
    row_head = _idiv(lax.broadcasted_iota(jnp.int32, (nrow, QK_COLS), 0), seq)
    qbd = jnp.where(lane_head == row_head, jnp.concatenate([qa_ref[...]] * FOX_HEADS, axis=0), 0.0).astype(BF16)
    aug_row = _imod(lax.broadcasted_iota(jnp.int32, (AUG, past), 0), AUG_GROUP)
    kt = jnp.concatenate([kt_ref[...].astype(BF16), jnp.where(aug_row < 3, 1.0, 0.0).astype(BF16)], axis=0)
    s_past = jnp.dot(qbd, kt, preferred_element_type=F32) + bias
    s_new = _dot_nt(qbd, ka_ref[...])
    qi = _imod(lax.broadcasted_iota(jnp.int32, (nrow, seq), 0), seq)
    kj = lax.broadcasted_iota(jnp.int32, (nrow, seq), 1)
    s_new = jnp.where(kj <= qi, s_new, NEG)
    m = jnp.maximum(jnp.max(s_past, axis=-1, keepdims=True), jnp.max(s_new, axis=-1, keepdims=True))
    p_past = jnp.exp2(s_past - m)
    p_new = jnp.exp2(s_new - m)
    lsum = jnp.sum(p_past, axis=-1, keepdims=True) + jnp.sum(p_new, axis=-1, keepdims=True)
    o = (_dot_nt(p_past.astype(BF16), vt_ref[...].astype(BF16))
         + jnp.dot(p_new.astype(BF16), vb_ref[...], preferred_element_type=F32)) / lsum
    o_ref[:, 0:FOX_DIM] = _diag_blocks(o, FOX_HEADS, seq, HEAD_DIM).astype(BF16)

    nrow_m = MEM_HEADS * seq
    lane_m = _idiv(lax.broadcasted_iota(jnp.int32, (nrow_m, MEM_DIM), 1), HEAD_DIM)
    row_m = _idiv(lax.broadcasted_iota(jnp.int32, (nrow_m, MEM_DIM), 0), seq)
    qbd_m = jnp.where(lane_m == row_m, jnp.concatenate([qm_ref[...]] * MEM_HEADS, axis=0), 0.0).astype(BF16)
    pm, lm = _softmax_rows(jnp.dot(qbd_m, mkt_ref[...].astype(BF16), preferred_element_type=F32) * SCALE)
    om = _dot_nt(pm.astype(BF16), mvt_ref[...].astype(BF16)) / lm
    o_ref[:, FOX_DIM:ATT_COLS] = _diag_blocks(om, MEM_HEADS, seq, HEAD_DIM).astype(BF16)


def _sample_attention(qa, ka, vb, qm, pk, pv, plf, mk, mv):
    nbatch, seq, _ = qa.shape
    past = pk.shape[2]
    per_step = 2
    assert nbatch % per_step == 0
    blk = lambda rows, cols: pl.BlockSpec((per_step, rows, cols), lambda b: (b, 0, 0))
    return pl.pallas_call(
        functools.partial(_sample_att_body, seq, past, per_step),
        grid=(nbatch // per_step,),
        in_specs=[blk(seq, QK_COLS), blk(seq, QK_COLS), blk(seq, FOX_DIM), blk(seq, MEM_DIM),
                  blk(FOX_DIM, past), blk(FOX_DIM, past), blk(FOX_HEADS, past),
                  blk(MEM_DIM, N_MEM), blk(MEM_DIM, N_MEM)],
        out_specs=blk(seq, ATT_COLS),
        out_shape=jax.ShapeDtypeStruct((nbatch, seq, ATT_COLS), BF16),
        compiler_params=pltpu.CompilerParams(
            dimension_semantics=("arbitrary",), vmem_limit_bytes=VMEM_LIMIT),
        name="sample_attention",
    )(qa, ka, vb, qm, pk, pv, plf, mk, mv)


def _prep_weights(norm_ffn1, norm_mix, w_in, b_forget, conv_w, conv_b,
                  q_norm_fox, k_norm_fox, q_norm_mem, k_norm_mem, norm_mem, w_mem_kv, w_out,
                  norm_ffn2, norm_final):
    w_in_t = w_in.T
    w_f_t = w_in_t[COL_F:COL_QM]
    wtail = jnp.concatenate(
        [w_in_t[COL_QM:], jnp.repeat(w_f_t, AUG_GROUP, axis=0), w_f_t,
         jnp.zeros((LANES - FOX_HEADS, D_MODEL), F32)], axis=0).T.astype(BF16)
    row = lambda v: v.reshape(1, -1).astype(F32)
    return {
        "g1": row(norm_ffn1), "gmix": row(norm_mix), "win32t": w_in_t, "wtail": wtail,
        "conv_w": conv_w.astype(F32), "conv_b": row(conv_b),
        "gq": row(jnp.tile(q_norm_fox, FOX_HEADS)) * (SCALE * LOG2E), "gk": row(jnp.tile(k_norm_fox, FOX_HEADS)),
        "gqm": row(jnp.tile(q_norm_mem, MEM_HEADS)),
        "bf_rep": row(jnp.repeat(b_forget, AUG_GROUP)),
        "bf": jnp.pad(row(b_forget), ((0, 0), (0, LANES - FOX_HEADS))),
        "gmem": row(norm_mem), "wmem": w_mem_kv.astype(BF16), "gkm": row(jnp.tile(k_norm_mem, MEM_HEADS)),
        "wo": w_out.astype(BF16), "g2": row(norm_ffn2), "gfin": row(norm_final),
    }


def kernel(x_prompt, x_sample, cache_fox_k, cache_fox_v, cache_fox_logf, state_conv, cache_mem_k, cache_mem_v, mem_prompt, norm_ffn1, w_ffn1_gate, w_ffn1_up, w_ffn1_down, norm_mix, w_in, b_forget, conv_w, conv_b, q_norm_fox, k_norm_fox, q_norm_mem, k_norm_mem, norm_mem, w_mem_kv, w_out, norm_ffn2, w_ffn2_gate, w_ffn2_up, w_ffn2_down, norm_final):
    depth = w_in.shape[0]
    assert depth == 1, "single-layer step"
    B, T, _ = x_prompt.shape
    nbs, seq_s, _ = x_sample.shape
    past = cache_fox_k.shape[2]
    l = 0
    wts = _prep_weights(norm_ffn1[l], norm_mix[l], w_in[l], b_forget[l], conv_w[l], conv_b[l], q_norm_fox[l],
                        k_norm_fox[l], q_norm_mem[l], k_norm_mem[l], norm_mem[l], w_mem_kv[l], w_out[l],
                        norm_ffn2[l], norm_final[l])
    tile = 512
    n_s = nbs * seq_s

    x1s, wts["wg1"], wts["wu1"], wts["wd1"] = _ffn_stream(
        (x_sample.reshape(n_s, D_MODEL),), wts["g1"], w_ffn1_gate[l], w_ffn1_up[l], w_ffn1_down[l])
    ycs, qas, kas, vbs, kfs, vfs, lfs, qms, ncs, wts["win"] = _stage1(
        x1s.reshape(1, n_s, D_MODEL), state_conv[l], wts, nb=tile // seq_s, seq=seq_s, carry=False)
    per_seq = lambda a: a.reshape(nbs, seq_s, a.shape[-1])
    feat_major = lambda a: a.reshape(a.shape[0], a.shape[1], -1).transpose(0, 2, 1)
    yas = _sample_attention(
        per_seq(qas), per_seq(kas), per_seq(vbs), per_seq(qms),
        feat_major(cache_fox_k[l]), feat_major(cache_fox_v[l]), feat_major(cache_fox_logf[l]),
        feat_major(cache_mem_k[l]), feat_major(cache_mem_v[l]))
    y_sample, wts["wg2"], wts["wu2"], wts["wd2"] = _ffn_stream(
        (x1s, ycs.reshape(n_s, CONV_DIM), yas.reshape(n_s, ATT_COLS)), wts["g2"],
        w_ffn2_gate[l], w_ffn2_up[l], w_ffn2_down[l], wo=wts["wo"], gfin=wts["gfin"])

    mk_p, mv_p, mkb, mvb = _memkv(mem_prompt, wts)
    x1, yc, qa, ka, vb, kf, vf, lf, qm, nc = _stage1(x_prompt, None, wts, nb=1, seq=tile, carry=True)
    ya = _prompt_attention(qa, ka, vb, qm, mkb, mvb, tq=2 * tile, tk=tile)
    y_prompt = _stage3(x1, yc, ya, wts, tm=tile)

    heads = lambda a, b, t, nh: a.reshape(1, b, t, nh, HEAD_DIM)
    token_major = lambda a, nh: a.reshape(a.shape[0], nh, -1, a.shape[2]).transpose(0, 3, 1, 2)[None]
    return (y_prompt, y_sample.reshape(nbs, seq_s, D_MODEL),
            token_major(kf, FOX_HEADS), token_major(vf, FOX_HEADS), lf.transpose(0, 2, 1)[None],
            nc.reshape(1, B, CONV_K - 1, CONV_DIM),
            token_major(mk_p, MEM_HEADS), token_major(mv_p, MEM_HEADS),
            heads(kfs, nbs, seq_s, FOX_HEADS), heads(vfs, nbs, seq_s, FOX_HEADS),
            lfs.reshape(1, nbs, seq_s, FOX_HEADS), ncs.reshape(1, nbs, CONV_K - 1, CONV_DIM))
```

```python
import functools

import jax
import jax.numpy as jnp
from jax import lax
from jax.experimental import pallas as pl
from jax.experimental.pallas import tpu as pltpu

F32 = jnp.float32
BF16 = jnp.bfloat16

D_MODEL = 1024
HEAD_DIM = 64
CONV_DIM = 256
CONV_K = 3
FOX_HEADS = 8
FOX_DIM = FOX_HEADS * HEAD_DIM
MEM_HEADS = 4
MEM_DIM = MEM_HEADS * HEAD_DIM
N_MEM = 256
D_FF = 2816
EPS = 1e-6

LANES = 128
PAIR = 2 * HEAD_DIM
FF_CHUNK = 256
N_FF_CHUNKS = D_FF // FF_CHUNK
AUG = LANES
AUG_GROUP = AUG // FOX_HEADS
QK_COLS = FOX_DIM + AUG
ATT_COLS = FOX_DIM + MEM_DIM
COL_Q = 3 * CONV_DIM
COL_K = COL_Q + FOX_DIM
COL_V = COL_K + FOX_DIM
COL_F = COL_V + FOX_DIM
COL_QM = COL_F + FOX_HEADS
TAIL_COLS = MEM_DIM + AUG + LANES
NEG = -1e30
SCALE = HEAD_DIM ** -0.5
LOG2E = 1.4426950408889634
VMEM_LIMIT = 56 * 1024 * 1024


def _idiv(x, d):
    if d & (d - 1) == 0:
        return lax.shift_right_logical(x, d.bit_length() - 1)
    return x // d


def _imod(x, d):
    if d & (d - 1) == 0:
        return x & (d - 1)
    return x % d


def _rms(x, g):
    ms = jnp.mean(x * x, axis=-1, keepdims=True)
    return x * lax.rsqrt(ms + EPS) * g


def _pair_headnorm(x, g):
    lo = lax.broadcasted_iota(jnp.int32, (1, PAIR), 1) < HEAD_DIM
    x2 = x * x
    s_lo = jnp.sum(jnp.where(lo, x2, 0.0), axis=-1, keepdims=True)
    s_hi = jnp.sum(jnp.where(lo, 0.0, x2), axis=-1, keepdims=True)
    r = jnp.where(lo, lax.rsqrt(s_lo * (1.0 / HEAD_DIM) + EPS),
                  lax.rsqrt(s_hi * (1.0 / HEAD_DIM) + EPS))
    return x * r * g


def _store_cols(ref, sl, x, transposed):
    if transposed:
        ref[sl, :] = x.T
    else:
        ref[:, sl] = x


def _split3(x):
    hi = x.astype(BF16).astype(F32)
    r = x - hi
    mid = r.astype(BF16).astype(F32)
    lo = (r - mid).astype(BF16).astype(F32)
    return hi, mid, lo


def _ffn(xn, wg_ref, wu_ref, wd_ref):
    acc = None
    for c in range(N_FF_CHUNKS):
        sl = slice(c * FF_CHUNK, (c + 1) * FF_CHUNK)
        g = jnp.dot(xn, wg_ref[:, sl], preferred_element_type=F32)
        u = jnp.dot(xn, wu_ref[:, sl], preferred_element_type=F32)
        a = (g * jax.nn.sigmoid(g) * u).astype(BF16)
        d = jnp.dot(a, wd_ref[sl, :], preferred_element_type=F32)
        acc = d if acc is None else acc + d
    return acc


def _log_sigmoid(x):
    return jnp.minimum(x, 0.0) - jnp.log1p(jnp.exp(-jnp.abs(x)))


def _softmax_rows(s):
    m = jnp.max(s, axis=-1, keepdims=True)
    p = jnp.exp(s - m)
    return p, jnp.sum(p, axis=-1, keepdims=True)


def _dot_nt(a, b):
    return lax.dot_general(a, b, (((1,), (1,)), ((), ())), preferred_element_type=F32)


_STAGE1_SHARED_IN = ("gmix", "wtail", "conv_w", "conv_b", "gq", "gk", "gqm", "bf_rep", "bf")
_STAGE1_SHARED_OUT = ("yc", "qa", "ka", "vb", "kf", "vf", "lf", "qm", "nc")
_STAGE1_IN = {True: ("x", "g1", "wg1", "wu1", "wd1", "win") + _STAGE1_SHARED_IN,
              False: ("x1", "state", "win32t") + _STAGE1_SHARED_IN}
_STAGE1_OUT = {True: ("x1",) + _STAGE1_SHARED_OUT, False: _STAGE1_SHARED_OUT + ("winb",)}


def _stage1_body(nb, seq, carry, *refs):
    tm = nb * seq
    names = _STAGE1_IN[carry] + _STAGE1_OUT[carry]
    r = dict(zip(names, refs[:len(names)]))
    cs_ref, cum_ref = refs[len(names):]
    gmix_ref, wtail_ref, cw_ref, cb_ref, gq_ref, gk_ref, gqm_ref, bfr_ref, bf_ref = (
        r[n] for n in _STAGE1_SHARED_IN)
    yc_ref, qa_ref, ka_ref, vb_ref, kf_ref, vf_ref, lf_ref, qm_ref, nc_ref = (r[n] for n in _STAGE1_SHARED_OUT)
    if carry:
        x = r["x"][...]
        y = _ffn(_rms(x, r["g1"][...]).astype(BF16), r["wg1"], r["wu1"], r["wd1"])
        x1 = x + 0.5 * y
        r["x1"][...] = x1
        win_ref = r["win"]
    else:
        x1 = r["x1"][...]
        st_ref = r["state"]
        win_ref = r["winb"]

        @pl.when(pl.program_id(1) == 0)
        def _():
            win_ref[...] = r["win32t"][0:COL_F, :].T.astype(BF16)
    h = _rms(x1, gmix_ref[...]).astype(BF16)

    ucb = jnp.dot(h, win_ref[:, 0:COL_Q], preferred_element_type=F32)
    ci = ucb[:, CONV_DIM:2 * CONV_DIM] * ucb[:, 0:CONV_DIM]
    if carry:
        @pl.when(pl.program_id(1) == 0)
        def _():
            cs_ref[:, 0:8, :] = jnp.zeros((nb, 8, CONV_DIM), F32)
    else:
        cs_ref[:, 8 - (CONV_K - 1):8, :] = st_ref[...]
    cs_ref[:, 8:8 + seq, :] = ci.reshape(nb, seq, CONV_DIM)
    conv = cb_ref[...] + cw_ref[CONV_K - 1:CONV_K, :] * ci
    for i in range(CONV_K - 1):
        shifted = cs_ref[:, 8 - (CONV_K - 1) + i:8 - (CONV_K - 1) + i + seq, :]
        conv = conv + cw_ref[i:i + 1, :] * shifted.reshape(tm, CONV_DIM)
    yc_ref[...] = (ucb[:, 2 * CONV_DIM:3 * CONV_DIM] * conv).astype(BF16)
    tail = cs_ref[:, seq:seq + 8, :]
    nc_ref[...] = tail[:, 8 - (CONV_K - 1):, :].reshape(nc_ref.shape)
    if carry:
        cs_ref[:, 0:8, :] = tail

    for name, col, g_ref, out_ref in (("q", COL_Q, gq_ref, qa_ref), ("k", COL_K, gk_ref, ka_ref)):
        pr = jnp.dot(h, win_ref[:, col:col + FOX_DIM], preferred_element_type=F32)
        for g in range(FOX_HEADS // 2):
            sl = slice(g * PAIR, (g + 1) * PAIR)
            xn = _pair_headnorm(pr[:, sl], g_ref[:, sl])
            if name == "k":
                _store_cols(kf_ref, sl, xn, carry)
            out_ref[:, sl] = xn.astype(BF16)
    v = jnp.dot(h, win_ref[:, COL_V:COL_V + FOX_DIM], preferred_element_type=F32)
    for g in range(FOX_HEADS // 2):
        sl = slice(g * PAIR, (g + 1) * PAIR)
        _store_cols(vf_ref, sl, v[:, sl], carry)
    vb_ref[...] = v.astype(BF16)

    tail_cols = jnp.dot(h, wtail_ref[...], preferred_element_type=F32)
    for g in range(MEM_HEADS // 2):
        sl = slice(g * PAIR, (g + 1) * PAIR)
        qm_ref[:, sl] = _pair_headnorm(tail_cols[:, sl], gqm_ref[:, sl]).astype(BF16)
    fl = tail_cols[:, MEM_DIM:]
    lf = _log_sigmoid(fl[:, AUG:2 * AUG] + bf_ref[...])
    if carry:
        lf_ref[...] = lf.T[0:FOX_HEADS, :]
    else:
        lf_ref[...] = lf[:, 0:FOX_HEADS]
    logf = _log_sigmoid(fl[:, 0:AUG] + bfr_ref[...])

    row = lax.broadcasted_iota(jnp.int32, (tm, tm), 0)
    col = lax.broadcasted_iota(jnp.int32, (tm, tm), 1)
    tri = col <= row
    if nb > 1:
        tri = jnp.logical_and(tri, _idiv(row, seq) == _idiv(col, seq))
    tri = jnp.where(tri, 1.0, 0.0).astype(BF16)
    hi, mid, lo = _split3(logf)
    parts = jnp.concatenate([hi.astype(BF16), mid.astype(BF16), lo.astype(BF16)], axis=1)
    cs = jnp.dot(tri, parts, preferred_element_type=F32)
    cum = cs[:, 0:AUG] + cs[:, AUG:2 * AUG] + cs[:, 2 * AUG:3 * AUG]
    if carry:
        @pl.when(pl.program_id(1) == 0)
        def _():
            cum_ref[...] = jnp.zeros(cum_ref.shape, F32)
        cum = cum + cum_ref[0:1, :]
        cum_ref[0:1, :] = cum[tm - 1:tm, :]

    j = _imod(lax.broadcasted_iota(jnp.int32, (1, AUG), 1), AUG_GROUP)
    chi, cmid, clo = _split3(cum * LOG2E)
    aq = jnp.where(j == 0, chi, jnp.where(j == 1, cmid, jnp.where(j == 2, clo, jnp.where(j < 6, 1.0, 0.0))))
    ak = jnp.where(j < 3, 1.0, jnp.where(j == 3, -chi, jnp.where(j == 4, -cmid, jnp.where(j == 5, -clo, 0.0))))
    qa_ref[:, FOX_DIM:QK_COLS] = aq.astype(BF16)
    ka_ref[:, FOX_DIM:QK_COLS] = ak.astype(BF16)


def _stage1(x3, state, wts, *, nb, seq, carry):
    G, T, _ = x3.shape
    tm = nb * seq
    nt = T // tm
    const = lambda shape: pl.BlockSpec(shape, lambda g, t: (0,) * len(shape), pipeline_mode=pl.Buffered(1))
    tok = lambda cols: pl.BlockSpec((None, tm, cols), lambda g, t: (g, t, 0))
    if carry:
        feat = lambda rows: pl.BlockSpec((None, rows, tm), lambda g, t: (g, 0, t))
        feat_shape = lambda rows: jax.ShapeDtypeStruct((G, rows, T), F32)
    else:
        feat = lambda rows: pl.BlockSpec((None, tm, rows), lambda g, t: (g, t, 0))
        feat_shape = lambda rows: jax.ShapeDtypeStruct((G, T, rows), F32)
    shp = lambda cols, dt: jax.ShapeDtypeStruct((G, T, cols), dt)
    if carry:
        nc_spec = pl.BlockSpec((None, CONV_K - 1, CONV_DIM), lambda g, t: (g, 0, 0))
        nc_shape = jax.ShapeDtypeStruct((G, CONV_K - 1, CONV_DIM), F32)
    else:
        nc_spec = pl.BlockSpec((nb, CONV_K - 1, CONV_DIM), lambda g, t: (t, 0, 0))
        nc_shape = jax.ShapeDtypeStruct((nt * nb, CONV_K - 1, CONV_DIM), F32)
    tiled_in = {"x": (tok(D_MODEL), x3), "x1": (tok(D_MODEL), x3), "state": (nc_spec, state)}
    in_specs, args = [], []
    for n in _STAGE1_IN[carry]:
        spec, arg = tiled_in[n] if n in tiled_in else (const(wts[n].shape), wts[n])
        in_specs.append(spec)
        args.append(arg)
    outs = {"x1": (tok(D_MODEL), shp(D_MODEL, F32)), "yc": (tok(CONV_DIM), shp(CONV_DIM, BF16)),
            "qa": (tok(QK_COLS), shp(QK_COLS, BF16)), "ka": (tok(QK_COLS), shp(QK_COLS, BF16)),
            "vb": (tok(FOX_DIM), shp(FOX_DIM, BF16)), "kf": (feat(FOX_DIM), feat_shape(FOX_DIM)),
            "vf": (feat(FOX_DIM), feat_shape(FOX_DIM)), "lf": (feat(FOX_HEADS), feat_shape(FOX_HEADS)),
            "qm": (tok(MEM_DIM), shp(MEM_DIM, BF16)), "nc": (nc_spec, nc_shape)}
    if not carry:
        w_shape = (D_MODEL, COL_F)
        outs["winb"] = (pl.BlockSpec(w_shape, lambda g, t: (0, 0)), jax.ShapeDtypeStruct(w_shape, BF16))
    out_specs = [outs[n][0] for n in _STAGE1_OUT[carry]]
    out_shape = [outs[n][1] for n in _STAGE1_OUT[carry]]
    return pl.pallas_call(
        functools.partial(_stage1_body, nb, seq, carry),
        grid=(G, nt),
        in_specs=in_specs,
        out_specs=out_specs,
        out_shape=out_shape,
        scratch_shapes=[pltpu.VMEM((nb, seq + 8, CONV_DIM), F32), pltpu.VMEM((8, AUG), F32)],
        compiler_params=pltpu.CompilerParams(
            dimension_semantics=("arbitrary", "arbitrary"), vmem_limit_bytes=VMEM_LIMIT),
        name="stage1_carry" if carry else "stage1_batched",
    )(*args)


def _stage3_body(x1_ref, yc_ref, ya_ref, wo_ref, g2_ref, wg_ref, wu_ref, wd_ref, gfin_ref, y_ref):
    x2 = (x1_ref[...]
          + jnp.dot(yc_ref[...], wo_ref[0:CONV_DIM, :], preferred_element_type=F32)
          + jnp.dot(ya_ref[...], wo_ref[CONV_DIM:, :], preferred_element_type=F32))
    y = _ffn(_rms(x2, g2_ref[...]).astype(BF16), wg_ref, wu_ref, wd_ref)
    y_ref[...] = _rms(x2 + 0.5 * y, gfin_ref[...])


def _stage3(x1, yc, ya, wts, *, tm):
    G, T, _ = x1.shape
    const = lambda shape: pl.BlockSpec(shape, lambda g, t: (0,) * len(shape), pipeline_mode=pl.Buffered(1))
    tok = lambda cols: pl.BlockSpec((None, tm, cols), lambda g, t: (g, t, 0))
    names = ("wo", "g2", "wg2", "wu2", "wd2", "gfin")
    return pl.pallas_call(
        _stage3_body,
        grid=(G, T // tm),
        in_specs=[tok(D_MODEL), tok(CONV_DIM), tok(ATT_COLS)] + [const(wts[n].shape) for n in names],
        out_specs=tok(D_MODEL),
        out_shape=jax.ShapeDtypeStruct((G, T, D_MODEL), F32),
        compiler_params=pltpu.CompilerParams(
            dimension_semantics=("arbitrary", "arbitrary"), vmem_limit_bytes=VMEM_LIMIT),
        name="stage3",
    )(x1, yc, ya, *[wts[n] for n in names])


def _ffn_stream_body(with_out_proj, with_final_norm, *refs):
    refs = list(refs)
    n_x = 4 if with_out_proj else 1
    x_refs, refs = refs[:n_x], refs[n_x:]
    g_ref, wg_ref, wu_ref, wd_ref = refs[:4]
    refs = refs[4:]
    gfin_ref = refs.pop(0) if with_final_norm else None
    y_ref, wgb_ref, wub_ref, wdb_ref, x_ref, xn_ref, acc_ref = refs
    c = pl.program_id(0)

    @pl.when(c == 0)
    def _():
        if with_out_proj:
            x1_ref, yc_ref, ya_ref, wo_ref = x_refs
            x = (x1_ref[...]
                 + jnp.dot(yc_ref[...], wo_ref[0:CONV_DIM, :], preferred_element_type=F32)
                 + jnp.dot(ya_ref[...], wo_ref[CONV_DIM:, :], preferred_element_type=F32))
        else:
            x = x_refs[0][...]
        x_ref[...] = x
        xn_ref[...] = _rms(x, g_ref[...]).astype(BF16)
        acc_ref[...] = jnp.zeros(acc_ref.shape, F32)

    wg = wg_ref[...].astype(BF16)
    wu = wu_ref[...].astype(BF16)
    wd = wd_ref[...].astype(BF16)
    wgb_ref[...] = wg
    wub_ref[...] = wu
    wdb_ref[...] = wd
    xn = xn_ref[...]
    g = jnp.dot(xn, wg, preferred_element_type=F32)
    u = jnp.dot(xn, wu, preferred_element_type=F32)
    a = (g * jax.nn.sigmoid(g) * u).astype(BF16)
    acc_ref[...] += jnp.dot(a, wd, preferred_element_type=F32)

    @pl.when(c == pl.num_programs(0) - 1)
    def _():
        y = x_ref[...] + 0.5 * acc_ref[...]
        y_ref[...] = _rms(y, gfin_ref[...]) if with_final_norm else y


def _ffn_stream(x_parts, gain, wg, wu, wd, *, wo=None, gfin=None):
    m = x_parts[0].shape[0]
    const = lambda a: pl.BlockSpec(a.shape, lambda c: (0,) * a.ndim, pipeline_mode=pl.Buffered(1))
    col_chunk = pl.BlockSpec((D_MODEL, FF_CHUNK), lambda c: (0, c))
    row_chunk = pl.BlockSpec((FF_CHUNK, D_MODEL), lambda c: (c, 0))
    args = list(x_parts) + ([wo] if wo is not None else []) + [gain]
    in_specs = [const(a) for a in args] + [col_chunk, col_chunk, row_chunk]
    args += [wg, wu, wd]
    if gfin is not None:
        in_specs.append(const(gfin))
        args.append(gfin)
    return pl.pallas_call(
        functools.partial(_ffn_stream_body, wo is not None, gfin is not None),
        grid=(N_FF_CHUNKS,),
        in_specs=in_specs,
        out_specs=[pl.BlockSpec((m, D_MODEL), lambda c: (0, 0)), col_chunk, col_chunk, row_chunk],
        out_shape=[jax.ShapeDtypeStruct((m, D_MODEL), F32), jax.ShapeDtypeStruct(wg.shape, BF16),
                   jax.ShapeDtypeStruct(wu.shape, BF16), jax.ShapeDtypeStruct(wd.shape, BF16)],
        scratch_shapes=[pltpu.VMEM((m, D_MODEL), F32), pltpu.VMEM((m, D_MODEL), BF16),
                        pltpu.VMEM((m, D_MODEL), F32)],
        compiler_params=pltpu.CompilerParams(dimension_semantics=("arbitrary",), vmem_limit_bytes=VMEM_LIMIT),
        name="ffn_stream_out" if wo is not None else "ffn_stream_in",
    )(*args)


def _memkv_body(mem_ref, gmem_ref, w_ref, gk_ref, mk_ref, mv_ref, mkb_ref, mvb_ref):
    kv = jnp.dot(_rms(mem_ref[...], gmem_ref[...]).astype(BF16), w_ref[...], preferred_element_type=F32)
    for g in range(MEM_HEADS // 2):
        sl = slice(g * PAIR, (g + 1) * PAIR)
        mk = _pair_headnorm(kv[:, sl], gk_ref[:, sl])
        mk_ref[sl, :] = mk.T
        mkb_ref[:, sl] = mk.astype(BF16)
    mv = kv[:, MEM_DIM:]
    mv_ref[...] = mv.T
    mvb_ref[...] = mv.astype(BF16)


def _memkv(mem, wts):
    B = mem.shape[0]
    const = lambda shape: pl.BlockSpec(shape, lambda b: (0,) * len(shape))
    blk = lambda cols: pl.BlockSpec((None, N_MEM, cols), lambda b: (b, 0, 0))
    shp = lambda dt: jax.ShapeDtypeStruct((B, N_MEM, MEM_DIM), dt)
    names = ("gmem", "wmem", "gkm")
    return pl.pallas_call(
        _memkv_body,
        grid=(B,),
        in_specs=[blk(D_MODEL)] + [const(wts[n].shape) for n in names],
        out_specs=[blk(MEM_DIM)] * 4,
        out_shape=[shp(F32), shp(F32), shp(BF16), shp(BF16)],
        compiler_params=pltpu.CompilerParams(dimension_semantics=("arbitrary",)),
        name="memkv",
    )(mem, *[wts[n] for n in names])


def _head_masks():
    lane = lax.broadcasted_iota(jnp.int32, (1, LANES), 1)
    lo = lane < HEAD_DIM
    return lo, _idiv(lane, AUG_GROUP)


def _prompt_att_body(tq, tk, iq_ref, ik_ref, qa_ref, ka_ref, vb_ref, qm_ref, mk_ref, mv_ref,
                     o_ref, m_ref, acc_ref):
    p = pl.program_id(1)
    iq = iq_ref[p]
    ik = ik_ref[p]
    lo, aug_head = _head_masks()

    @pl.when(ik == 0)
    def _():
        m_ref[...] = jnp.full(m_ref.shape, NEG, F32)
        acc_ref[...] = jnp.zeros(acc_ref.shape, F32)

    def step(masked, r0, sub):
        rows = tq - r0
        keys = slice(sub * tk, (sub + 1) * tk)
        if masked:
            visible = (lax.broadcasted_iota(jnp.int32, (tk, tk), 1)
                       <= lax.broadcasted_iota(jnp.int32, (tk, tk), 0))
        q_aug = qa_ref[r0:, FOX_DIM:QK_COLS]
        k_aug = ka_ref[keys, FOX_DIM:QK_COLS]
        ones = jnp.ones((tk, LANES), BF16)
        for h in range(FOX_HEADS):
            sl = slice((h // 2) * PAIR, (h // 2 + 1) * PAIR)
            mine = lo if h % 2 == 0 else jnp.logical_not(lo)
            qh = jnp.concatenate([jnp.where(mine, qa_ref[r0:, sl], 0.0).astype(BF16),
                                  jnp.where(aug_head == h, q_aug, 0.0).astype(BF16)], axis=1)
            kh = jnp.concatenate([ka_ref[keys, sl], k_aug], axis=1)
            s = _dot_nt(qh, kh)
            if masked:
                top = jnp.where(visible, s[:tk], NEG)
                s = top if rows == tk else jnp.concatenate([top, s[tk:]], axis=0)
            blocks = [s[:, c * LANES:(c + 1) * LANES] for c in range(tk // LANES)]
            blk_max = functools.reduce(jnp.maximum, blocks)
            m_prev = m_ref[h, r0:, :]
            m_new = jnp.maximum(m_prev, jnp.max(blk_max, axis=-1, keepdims=True))
            alpha = jnp.exp2(m_prev - m_new)
            pexp = jnp.concatenate([jnp.exp2(b - m_new) for b in blocks], axis=1).astype(BF16)
            pv = jnp.dot(pexp, jnp.concatenate([vb_ref[keys, sl], ones], axis=1), preferred_element_type=F32)
            acc_ref[h, r0:, :] = jnp.concatenate([alpha, alpha], axis=1) * acc_ref[h, r0:, :] + pv
            m_ref[h, r0:, :] = m_new

    @pl.when(ik < iq)
    def _():
        for sub in range(tq // tk):
            step(False, 0, sub)

    @pl.when(ik == iq)
    def _():
        for sub in range(tq // tk):
            step(True, sub * tk, sub)
        for g in range(FOX_HEADS // 2):
            even = acc_ref[2 * g, :, 0:PAIR] / acc_ref[2 * g, :, PAIR:2 * PAIR]
            odd = acc_ref[2 * g + 1, :, 0:PAIR] / acc_ref[2 * g + 1, :, PAIR:2 * PAIR]
            o_ref[:, g * PAIR:(g + 1) * PAIR] = jnp.where(lo, even, odd).astype(BF16)
        for g in range(MEM_HEADS // 2):
            sl = slice(g * PAIR, (g + 1) * PAIR)
            outs = []
            for mine in (lo, jnp.logical_not(lo)):
                qh = jnp.where(mine, qm_ref[:, sl], 0.0).astype(BF16)
                pexp, lsum = _softmax_rows(_dot_nt(qh, mk_ref[:, sl]) * SCALE)
                outs.append(jnp.dot(pexp.astype(BF16), mv_ref[:, sl], preferred_element_type=F32) / lsum)
            o_ref[:, FOX_DIM + g * PAIR:FOX_DIM + (g + 1) * PAIR] = jnp.where(lo, outs[0], outs[1]).astype(BF16)


def _prompt_attention(qa, ka, vb, qm, mkb, mvb, *, tq, tk):
    B, T, _ = qa.shape
    assert tq % tk == 0 and T % tq == 0
    pairs = [(i, j) for i in range(T // tq) for j in range(i + 1)]
    iq_tab = jnp.asarray([p[0] for p in pairs], jnp.int32)
    ik_tab = jnp.asarray([p[1] for p in pairs], jnp.int32)
    grid_spec = pltpu.PrefetchScalarGridSpec(
        num_scalar_prefetch=2,
        grid=(B, len(pairs)),
        in_specs=[
            pl.BlockSpec((None, tq, QK_COLS), lambda b, p, iq, ik: (b, iq[p], 0)),
            pl.BlockSpec((None, tq, QK_COLS), lambda b, p, iq, ik: (b, ik[p], 0)),
            pl.BlockSpec((None, tq, FOX_DIM), lambda b, p, iq, ik: (b, ik[p], 0)),
            pl.BlockSpec((None, tq, MEM_DIM), lambda b, p, iq, ik: (b, iq[p], 0)),
            pl.BlockSpec((None, N_MEM, MEM_DIM), lambda b, p, iq, ik: (b, 0, 0)),
            pl.BlockSpec((None, N_MEM, MEM_DIM), lambda b, p, iq, ik: (b, 0, 0)),
        ],
        out_specs=pl.BlockSpec((None, tq, ATT_COLS), lambda b, p, iq, ik: (b, iq[p], 0)),
        scratch_shapes=[pltpu.VMEM((FOX_HEADS, tq, LANES), F32), pltpu.VMEM((FOX_HEADS, tq, 2 * PAIR), F32)],
    )
    return pl.pallas_call(
        functools.partial(_prompt_att_body, tq, tk),
        grid_spec=grid_spec,
        out_shape=jax.ShapeDtypeStruct((B, T, ATT_COLS), BF16),
        compiler_params=pltpu.CompilerParams(
            dimension_semantics=("arbitrary", "arbitrary"), vmem_limit_bytes=VMEM_LIMIT),
        name="prompt_attention",
    )(iq_tab, ik_tab, qa, ka, vb, qm, mkb, mvb)


def _diag_blocks(x, nh, rows, width):
    head = _idiv(lax.broadcasted_iota(jnp.int32, (1, nh * width), 1), width)
    out = None
    for h in range(nh):
        part = jnp.where(head == h, x[h * rows:(h + 1) * rows, :], 0.0)
        out = part if out is None else out + part
    return out


def _sample_att_body(seq, past, per_step, *refs):
    for i in range(per_step):
        _sample_att_one(seq, past, *[r.at[i] for r in refs])


def _sample_att_one(seq, past, qa_ref, ka_ref, vb_ref, qm_ref, kt_ref, vt_ref, lt_ref, mkt_ref, mvt_ref, o_ref):
    blk = LANES
    nblk = past // blk
    lt = lt_ref[...]
    x = jnp.concatenate([lt[:, b * blk:(b + 1) * blk] for b in range(nblk)], axis=0)
    n = x.shape[0]
    parts = jnp.concatenate(_split3(x), axis=0).astype(BF16)
    src = lax.broadcasted_iota(jnp.int32, (blk, blk), 0)
    dst = lax.broadcasted_iota(jnp.int32, (blk, blk), 1)
    later = jnp.where(src > dst, 1.0, 0.0).astype(BF16)
    loc = jnp.dot(parts, later, preferred_element_type=F32)
    tot = jnp.dot(parts, jnp.ones((blk, blk), BF16), preferred_element_type=F32)
    loc = loc[0:n] + loc[n:2 * n] + loc[2 * n:3 * n]
    tot = tot[0:n] + tot[n:2 * n] + tot[2 * n:3 * n]
    running = jnp.zeros((FOX_HEADS, blk), F32)
    suffix = [None] * nblk
    for b in reversed(range(nblk)):
        rows = slice(b * FOX_HEADS, (b + 1) * FOX_HEADS)
        suffix[b] = loc[rows] + running
        running = running + tot[rows]
    rt = jnp.concatenate(suffix, axis=1) * LOG2E
    bias = jnp.concatenate([jnp.broadcast_to(rt[h:h + 1, :], (seq, past)) for h in range(FOX_HEADS)], axis=0)

    nrow = FOX_HEADS * seq
    lane = lax.broadcasted_iota(jnp.int32, (nrow, QK_COLS), 1)
    lane_head = jnp.where(lane < FOX_DIM, _idiv(lane, HEAD_DIM), _idiv(lane - FOX_DIM, AUG_GROUP))
    row_head = _idiv(lax.broadcasted_iota(jnp.int32, (nrow, QK_COLS), 0), seq)
    qbd = jnp.where(lane_head == row_head, jnp.concatenate([qa_ref[...]] * FOX_HEADS, axis=0), 0.0).astype(BF16)
    aug_row = _imod(lax.broadcasted_iota(jnp.int32, (AUG, past), 0), AUG_GROUP)
    kt = jnp.concatenate([kt_ref[...].astype(BF16), jnp.where(aug_row < 3, 1.0, 0.0).astype(BF16)], axis=0)
    s_past = jnp.dot(qbd, kt, preferred_element_type=F32) + bias
    s_new = _dot_nt(qbd, ka_ref[...])
    qi = _imod(lax.broadcasted_iota(jnp.int32, (nrow, seq), 0), seq)
    kj = lax.broadcasted_iota(jnp.int32, (nrow, seq), 1)
    s_new = jnp.where(kj <= qi, s_new, NEG)
    m = jnp.maximum(jnp.max(s_past, axis=-1, keepdims=True), jnp.max(s_new, axis=-1, keepdims=True))
    p_past = jnp.exp2(s_past - m)
    p_new = jnp.exp2(s_new - m)
    lsum = jnp.sum(p_past, axis=-1, keepdims=True) + jnp.sum(p_new, axis=-1, keepdims=True)
    o = (_dot_nt(p_past.astype(BF16), vt_ref[...].astype(BF16))
         + jnp.dot(p_new.astype(BF16), vb_ref[...], preferred_element_type=F32)) / lsum
    o_ref[:, 0:FOX_DIM] = _diag_blocks(o, FOX_HEADS, seq, HEAD_DIM).astype(BF16)

    nrow_m = MEM_HEADS * seq
    lane_m = _idiv(lax.broadcasted_iota(jnp.int32, (nrow_m, MEM_DIM), 1), HEAD_DIM)
    row_m = _idiv(lax.broadcasted_iota(jnp.int32, (nrow_m, MEM_DIM), 0), seq)
    qbd_m = jnp.where(lane_m == row_m, jnp.concatenate([qm_ref[...]] * MEM_HEADS, axis=0), 0.0).astype(BF16)
    pm, lm = _softmax_rows(jnp.dot(qbd_m, mkt_ref[...].astype(BF16), preferred_element_type=F32) * SCALE)
    om = _dot_nt(pm.astype(BF16), mvt_ref[...].astype(BF16)) / lm
    o_ref[:, FOX_DIM:ATT_COLS] = _diag_blocks(om, MEM_HEADS, seq, HEAD_DIM).astype(BF16)


def _sample_attention(qa, ka, vb, qm, pk, pv, plf, mk, mv):
    nbatch, seq, _ = qa.shape
    past = pk.shape[2]
    per_step = 2
    assert nbatch % per_step == 0
    blk = lambda rows, cols: pl.BlockSpec((per_step, rows, cols), lambda b: (b, 0, 0))
    return pl.pallas_call(
        functools.partial(_sample_att_body, seq, past, per_step),
        grid=(nbatch // per_step,),
        in_specs=[blk(seq, QK_COLS), blk(seq, QK_COLS), blk(seq, FOX_DIM), blk(seq, MEM_DIM),
                  blk(FOX_DIM, past), blk(FOX_DIM, past), blk(FOX_HEADS, past),
                  blk(MEM_DIM, N_MEM), blk(MEM_DIM, N_MEM)],
        out_specs=blk(seq, ATT_COLS),
        out_shape=jax.ShapeDtypeStruct((nbatch, seq, ATT_COLS), BF16),
        compiler_params=pltpu.CompilerParams(
            dimension_semantics=("arbitrary",), vmem_limit_bytes=VMEM_LIMIT),
        name="sample_attention",
    )(qa, ka, vb, qm, pk, pv, plf, mk, mv)


def _prep_weights(norm_ffn1, norm_mix, w_in, b_forget, conv_w, conv_b,
                  q_norm_fox, k_norm_fox, q_norm_mem, k_norm_mem, norm_mem, w_mem_kv, w_out,
                  norm_ffn2, norm_final):
    w_in_t = w_in.T
    w_f_t = w_in_t[COL_F:COL_QM]
    wtail = jnp.concatenate(
        [w_in_t[COL_QM:], jnp.repeat(w_f_t, AUG_GROUP, axis=0), w_f_t,
         jnp.zeros((LANES - FOX_HEADS, D_MODEL), F32)], axis=0).T.astype(BF16)
    row = lambda v: v.reshape(1, -1).astype(F32)
    return {
        "g1": row(norm_ffn1), "gmix": row(norm_mix), "win32t": w_in_t, "wtail": wtail,
        "conv_w": conv_w.astype(F32), "conv_b": row(conv_b),
        "gq": row(jnp.tile(q_norm_fox, FOX_HEADS)) * (SCALE * LOG2E), "gk": row(jnp.tile(k_norm_fox, FOX_HEADS)),
        "gqm": row(jnp.tile(q_norm_mem, MEM_HEADS)),
        "bf_rep": row(jnp.repeat(b_forget, AUG_GROUP)),
        "bf": jnp.pad(row(b_forget), ((0, 0), (0, LANES - FOX_HEADS))),
        "gmem": row(norm_mem), "wmem": w_mem_kv.astype(BF16), "gkm": row(jnp.tile(k_norm_mem, MEM_HEADS)),
        "wo": w_out.astype(BF16), "g2": row(norm_ffn2), "gfin": row(norm_final),
    }


def kernel(x_prompt, x_sample, cache_fox_k, cache_fox_v, cache_fox_logf, state_conv, cache_mem_k, cache_mem_v, mem_prompt, norm_ffn1, w_ffn1_gate, w_ffn1_up, w_ffn1_down, norm_mix, w_in, b_forget, conv_w, conv_b, q_norm_fox, k_norm_fox, q_norm_mem, k_norm_mem, norm_mem, w_mem_kv, w_out, norm_ffn2, w_ffn2_gate, w_ffn2_up, w_ffn2_down, norm_final):
    depth = w_in.shape[0]
    assert depth == 1, "single-layer step"
    B, T, _ = x_prompt.shape
    nbs, seq_s, _ = x_sample.shape
    past = cache_fox_k.shape[2]
    l = 0
    wts = _prep_weights(norm_ffn1[l], norm_mix[l], w_in[l], b_forget[l], conv_w[l], conv_b[l], q_norm_fox[l],
                        k_norm_fox[l], q_norm_mem[l], k_norm_mem[l], norm_mem[l], w_mem_kv[l], w_out[l],
                        norm_ffn2[l], norm_final[l])
    tile = 512
    n_s = nbs * seq_s

    x1s, wts["wg1"], wts["wu1"], wts["wd1"] = _ffn_stream(
        (x_sample.reshape(n_s, D_MODEL),), wts["g1"], w_ffn1_gate[l], w_ffn1_up[l], w_ffn1_down[l])
    ycs, qas, kas, vbs, kfs, vfs, lfs, qms, ncs, wts["win"] = _stage1(
        x1s.reshape(1, n_s, D_MODEL), state_conv[l], wts, nb=tile // seq_s, seq=seq_s, carry=False)
    per_seq = lambda a: a.reshape(nbs, seq_s, a.shape[-1])
    feat_major = lambda a: a.reshape(a.shape[0], a.shape[1], -1).transpose(0, 2, 1)
    yas = _sample_attention(
        per_seq(qas), per_seq(kas), per_seq(vbs), per_seq(qms),
        feat_major(cache_fox_k[l]), feat_major(cache_fox_v[l]), feat_major(cache_fox_logf[l]),
        feat_major(cache_mem_k[l]), feat_major(cache_mem_v[l]))
    y_sample, wts["wg2"], wts["wu2"], wts["wd2"] = _ffn_stream(
        (x1s, ycs.reshape(n_s, CONV_DIM), yas.reshape(n_s, ATT_COLS)), wts["g2"],
        w_ffn2_gate[l], w_ffn2_up[l], w_ffn2_down[l], wo=wts["wo"], gfin=wts["gfin"])

    mk_p, mv_p, mkb, mvb = _memkv(mem_prompt, wts)
    x1, yc, qa, ka, vb, kf, vf, lf, qm, nc = _stage1(x_prompt, None, wts, nb=1, seq=tile, carry=True)
    ya = _prompt_attention(qa, ka, vb, qm, mkb, mvb, tq=2 * tile, tk=tile)
    y_prompt = _stage3(x1, yc, ya, wts, tm=tile)

    heads = lambda a, b, t, nh: a.reshape(1, b, t, nh, HEAD_DIM)
    token_major = lambda a, nh: a.reshape(a.shape[0], nh, -1, a.shape[2]).transpose(0, 3, 1, 2)[None]
    return (y_prompt, y_sample.reshape(nbs, seq_s, D_MODEL),
            token_major(kf, FOX_HEADS), token_major(vf, FOX_HEADS), lf.transpose(0, 2, 1)[None],
            nc.reshape(1, B, CONV_K - 1, CONV_DIM),
            token_major(mk_p, MEM_HEADS), token_major(mv_p, MEM_HEADS),
            heads(kfs, nbs, seq_s, FOX_HEADS), heads(vfs, nbs, seq_s, FOX_HEADS),
            lfs.reshape(1, nbs, seq_s, FOX_HEADS), ncs.reshape(1, nbs, CONV_K - 1, CONV_DIM))
```

```python
import functools

import jax
import jax.numpy as jnp
from jax import lax
from jax.experimental import pallas as pl
from jax.experimental.pallas import tpu as pltpu

F32 = jnp.float32
BF16 = jnp.bfloat16

D_MODEL = 1024
HEAD_DIM = 64
CONV_DIM = 256
CONV_K = 3
FOX_HEADS = 8
FOX_DIM = FOX_HEADS * HEAD_DIM
MEM_HEADS = 4
MEM_DIM = MEM_HEADS * HEAD_DIM
N_MEM = 256
D_FF = 2816
EPS = 1e-6

LANES = 128
PAIR = 2 * HEAD_DIM
FF_CHUNK = 256
N_FF_CHUNKS = D_FF // FF_CHUNK
AUG = LANES
AUG_GROUP = AUG // FOX_HEADS
N_SPLIT = 3
QK_COLS = FOX_DIM + AUG
ATT_COLS = FOX_DIM + MEM_DIM
COL_Q = 3 * CONV_DIM
COL_K = COL_Q + FOX_DIM
COL_V = COL_K + FOX_DIM
COL_F = COL_V + FOX_DIM
COL_QM = COL_F + FOX_HEADS
TAIL_COLS = MEM_DIM + AUG + LANES
NEG = -1e30
SCALE = HEAD_DIM ** -0.5
LOG2E = 1.4426950408889634
VMEM_LIMIT = 56 * 1024 * 1024
TOKEN_TILE = 512
ATT_KEY_TILE = 512
ATT_QUERY_TILE = 1024
SAMPLE_SEQS_PER_STEP = 2


def _idiv(x, d):
    if d & (d - 1) == 0:
        return lax.shift_right_logical(x, d.bit_length() - 1)
    return x // d


def _imod(x, d):
    if d & (d - 1) == 0:
        return x & (d - 1)
    return x % d


def _rms(x, g):
    ms = jnp.mean(x * x, axis=-1, keepdims=True)
    return x * lax.rsqrt(ms + EPS) * g


def _pair_headnorm(x, g):
    lo = lax.broadcasted_iota(jnp.int32, (1, PAIR), 1) < HEAD_DIM
    x2 = x * x
    s_lo = jnp.sum(jnp.where(lo, x2, 0.0), axis=-1, keepdims=True)
    s_hi = jnp.sum(jnp.where(lo, 0.0, x2), axis=-1, keepdims=True)
    r = jnp.where(lo, lax.rsqrt(s_lo * (1.0 / HEAD_DIM) + EPS),
                  lax.rsqrt(s_hi * (1.0 / HEAD_DIM) + EPS))
    return x * r * g


def _store_cols(ref, sl, x, transposed):
    if transposed:
        ref[sl, :] = x.T
    else:
        ref[:, sl] = x


def _split3(x):
    hi = x.astype(BF16).astype(F32)
    r = x - hi
    mid = r.astype(BF16).astype(F32)
    lo = (r - mid).astype(BF16).astype(F32)
    return hi, mid, lo


def _ffn(xn, wg_ref, wu_ref, wd_ref):
    acc = None
    for c in range(N_FF_CHUNKS):
        sl = slice(c * FF_CHUNK, (c + 1) * FF_CHUNK)
        g = jnp.dot(xn, wg_ref[:, sl], preferred_element_type=F32)
        u = jnp.dot(xn, wu_ref[:, sl], preferred_element_type=F32)
        a = (g * jax.nn.sigmoid(g) * u).astype(BF16)
        d = jnp.dot(a, wd_ref[sl, :], preferred_element_type=F32)
        acc = d if acc is None else acc + d
    return acc


def _log_sigmoid(x):
    return jnp.minimum(x, 0.0) - jnp.log1p(jnp.exp(-jnp.abs(x)))


def _softmax_rows(s):
    m = jnp.max(s, axis=-1, keepdims=True)
    p = jnp.exp(s - m)
    return p, jnp.sum(p, axis=-1, keepdims=True)


def _dot_nt(a, b):
    return lax.dot_general(a, b, (((1,), (1,)), ((), ())), preferred_element_type=F32)


_STAGE1_SHARED_IN = ("gmix", "conv_w", "conv_b", "gq", "gk", "gqm", "bf_rep", "bf")
_STAGE1_SHARED_OUT = ("yc", "qa", "ka", "vb", "kf", "vf", "lf", "qm", "nc")
_STAGE1_IN = {True: ("x", "g1", "wg1", "wu1", "wd1", "win", "wtail") + _STAGE1_SHARED_IN,
              False: ("x1", "state", "win32t") + _STAGE1_SHARED_IN}
_STAGE1_OUT = {True: ("x1",) + _STAGE1_SHARED_OUT, False: _STAGE1_SHARED_OUT + ("win", "wtail")}


def _stage1_body(nb, seq, carry, *refs):
    tm = nb * seq
    names = _STAGE1_IN[carry] + _STAGE1_OUT[carry]
    r = dict(zip(names, refs[:len(names)]))
    cs_ref, cum_ref = refs[len(names):]
    gmix_ref, cw_ref, cb_ref, gq_ref, gk_ref, gqm_ref, bfr_ref, bf_ref = (r[n] for n in _STAGE1_SHARED_IN)
    yc_ref, qa_ref, ka_ref, vb_ref, kf_ref, vf_ref, lf_ref, qm_ref, nc_ref = (r[n] for n in _STAGE1_SHARED_OUT)
    win_ref, wtail_ref = r["win"], r["wtail"]
    if carry:
        x = r["x"][...]
        y = _ffn(_rms(x, r["g1"][...]).astype(BF16), r["wg1"], r["wu1"], r["wd1"])
        x1 = x + 0.5 * y
        r["x1"][...] = x1
    else:
        x1 = r["x1"][...]
        st_ref = r["state"]

        @pl.when(pl.program_id(1) == 0)
        def _():
            w_t = r["win32t"]
            win_ref[...] = w_t[0:COL_F, :].T.astype(BF16)
            f_rows = w_t[COL_F:COL_QM, :]
            f_rep = jnp.concatenate([jnp.broadcast_to(f_rows[hd:hd + 1, :], (AUG_GROUP, D_MODEL))
                                     for hd in range(FOX_HEADS)], axis=0)
            tail_t = jnp.concatenate([w_t[COL_QM:COL_QM + MEM_DIM, :], f_rep, f_rows,
                                      jnp.zeros((LANES - FOX_HEADS, D_MODEL), F32)], axis=0)
            wtail_ref[...] = tail_t.T.astype(BF16)
    h = _rms(x1, gmix_ref[...]).astype(BF16)

    ucb = jnp.dot(h, win_ref[:, 0:COL_Q], preferred_element_type=F32)
    ci = ucb[:, CONV_DIM:2 * CONV_DIM] * ucb[:, 0:CONV_DIM]
    if carry:
        @pl.when(pl.program_id(1) == 0)
        def _():
            cs_ref[:, 0:8, :] = jnp.zeros((nb, 8, CONV_DIM), F32)
    else:
        cs_ref[:, 8 - (CONV_K - 1):8, :] = st_ref[...]
    cs_ref[:, 8:8 + seq, :] = ci.reshape(nb, seq, CONV_DIM)
    conv = cb_ref[...] + cw_ref[CONV_K - 1:CONV_K, :] * ci
    for i in range(CONV_K - 1):
        shifted = cs_ref[:, 8 - (CONV_K - 1) + i:8 - (CONV_K - 1) + i + seq, :]
        conv = conv + cw_ref[i:i + 1, :] * shifted.reshape(tm, CONV_DIM)
    yc_ref[...] = (ucb[:, 2 * CONV_DIM:3 * CONV_DIM] * conv).astype(BF16)
    tail = cs_ref[:, seq:seq + 8, :]
    nc_ref[...] = tail[:, 8 - (CONV_K - 1):, :].reshape(nc_ref.shape)
    if carry:
        cs_ref[:, 0:8, :] = tail

    for name, col, g_ref, out_ref in (("q", COL_Q, gq_ref, qa_ref), ("k", COL_K, gk_ref, ka_ref)):
        pr = jnp.dot(h, win_ref[:, col:col + FOX_DIM], preferred_element_type=F32)
        for g in range(FOX_HEADS // 2):
            sl = slice(g * PAIR, (g + 1) * PAIR)
            xn = _pair_headnorm(pr[:, sl], g_ref[:, sl])
            if name == "k":
                _store_cols(kf_ref, sl, xn, carry)
            out_ref[:, sl] = xn.astype(BF16)
    v = jnp.dot(h, win_ref[:, COL_V:COL_V + FOX_DIM], preferred_element_type=F32)
    for g in range(FOX_HEADS // 2):
        sl = slice(g * PAIR, (g + 1) * PAIR)
        _store_cols(vf_ref, sl, v[:, sl], carry)
    vb_ref[...] = v.astype(BF16)

    tail_cols = jnp.dot(h, wtail_ref[...], preferred_element_type=F32)
    for g in range(MEM_HEADS // 2):
        sl = slice(g * PAIR, (g + 1) * PAIR)
        qm_ref[:, sl] = _pair_headnorm(tail_cols[:, sl], gqm_ref[:, sl]).astype(BF16)
    fl = tail_cols[:, MEM_DIM:]
    lf = _log_sigmoid(fl[:, AUG:2 * AUG] + bf_ref[...])
    if carry:
        lf_ref[...] = lf.T[0:FOX_HEADS, :]
    else:
        lf_ref[...] = lf[:, 0:FOX_HEADS]
    logf = _log_sigmoid(fl[:, 0:AUG] + bfr_ref[...])

    row = lax.broadcasted_iota(jnp.int32, (tm, tm), 0)
    col = lax.broadcasted_iota(jnp.int32, (tm, tm), 1)
    tri = col <= row
    if nb > 1:
        tri = jnp.logical_and(tri, _idiv(row, seq) == _idiv(col, seq))
    tri = jnp.where(tri, 1.0, 0.0).astype(BF16)
    hi, mid, lo = _split3(logf)
    parts = jnp.concatenate([hi.astype(BF16), mid.astype(BF16), lo.astype(BF16)], axis=1)
    cs = jnp.dot(tri, parts, preferred_element_type=F32)
    cum = cs[:, 0:AUG] + cs[:, AUG:2 * AUG] + cs[:, 2 * AUG:3 * AUG]
    if carry:
        @pl.when(pl.program_id(1) == 0)
        def _():
            cum_ref[...] = jnp.zeros(cum_ref.shape, F32)
        cum = cum + cum_ref[0:1, :]
        cum_ref[0:1, :] = cum[tm - 1:tm, :]

    j = _imod(lax.broadcasted_iota(jnp.int32, (1, AUG), 1), AUG_GROUP)
    terms = _split3(cum * LOG2E)
    aq = jnp.where(j < 2 * N_SPLIT, 1.0, 0.0)
    ak = jnp.where(j < N_SPLIT, 1.0, 0.0)
    for i, term in enumerate(terms):
        aq = jnp.where(j == i, term, aq)
        ak = jnp.where(j == N_SPLIT + i, -term, ak)
    qa_ref[:, FOX_DIM:QK_COLS] = aq.astype(BF16)
    ka_ref[:, FOX_DIM:QK_COLS] = ak.astype(BF16)


def _stage1(x3, state, wts, *, nb, seq, carry):
    G, T, _ = x3.shape
    tm = nb * seq
    nt = T // tm
    const = lambda shape: pl.BlockSpec(shape, lambda g, t: (0,) * len(shape), pipeline_mode=pl.Buffered(1))
    tok = lambda cols: pl.BlockSpec((None, tm, cols), lambda g, t: (g, t, 0))
    if carry:
        feat = lambda rows: pl.BlockSpec((None, rows, tm), lambda g, t: (g, 0, t))
        feat_shape = lambda rows: jax.ShapeDtypeStruct((G, rows, T), F32)
    else:
        feat = lambda rows: pl.BlockSpec((None, tm, rows), lambda g, t: (g, t, 0))
        feat_shape = lambda rows: jax.ShapeDtypeStruct((G, T, rows), F32)
    shp = lambda cols, dt: jax.ShapeDtypeStruct((G, T, cols), dt)
    if carry:
        nc_spec = pl.BlockSpec((None, CONV_K - 1, CONV_DIM), lambda g, t: (g, 0, 0))
        nc_shape = jax.ShapeDtypeStruct((G, CONV_K - 1, CONV_DIM), F32)
    else:
        nc_spec = pl.BlockSpec((nb, CONV_K - 1, CONV_DIM), lambda g, t: (t, 0, 0))
        nc_shape = jax.ShapeDtypeStruct((nt * nb, CONV_K - 1, CONV_DIM), F32)
    tiled_in = {"x": (tok(D_MODEL), x3), "x1": (tok(D_MODEL), x3), "state": (nc_spec, state)}
    in_specs, args = [], []
    for n in _STAGE1_IN[carry]:
        spec, arg = tiled_in[n] if n in tiled_in else (const(wts[n].shape), wts[n])
        in_specs.append(spec)
        args.append(arg)
    outs = {"x1": (tok(D_MODEL), shp(D_MODEL, F32)), "yc": (tok(CONV_DIM), shp(CONV_DIM, BF16)),
            "qa": (tok(QK_COLS), shp(QK_COLS, BF16)), "ka": (tok(QK_COLS), shp(QK_COLS, BF16)),
            "vb": (tok(FOX_DIM), shp(FOX_DIM, BF16)), "kf": (feat(FOX_DIM), feat_shape(FOX_DIM)),
            "vf": (feat(FOX_DIM), feat_shape(FOX_DIM)), "lf": (feat(FOX_HEADS), feat_shape(FOX_HEADS)),
            "qm": (tok(MEM_DIM), shp(MEM_DIM, BF16)), "nc": (nc_spec, nc_shape)}
    if not carry:
        for n, w_shape in (("win", (D_MODEL, COL_F)), ("wtail", (D_MODEL, TAIL_COLS))):
            outs[n] = (pl.BlockSpec(w_shape, lambda g, t: (0, 0)), jax.ShapeDtypeStruct(w_shape, BF16))
    out_specs = [outs[n][0] for n in _STAGE1_OUT[carry]]
    out_shape = [outs[n][1] for n in _STAGE1_OUT[carry]]
    return pl.pallas_call(
        functools.partial(_stage1_body, nb, seq, carry),
        grid=(G, nt),
        in_specs=in_specs,
        out_specs=out_specs,
        out_shape=out_shape,
        scratch_shapes=[pltpu.VMEM((nb, seq + 8, CONV_DIM), F32), pltpu.VMEM((8, AUG), F32)],
        compiler_params=pltpu.CompilerParams(
            dimension_semantics=("arbitrary", "arbitrary"), vmem_limit_bytes=VMEM_LIMIT),
        name="stage1_carry" if carry else "stage1_batched",
    )(*args)


def _stage3_body(x1_ref, yc_ref, ya_ref, wo_ref, g2_ref, wg_ref, wu_ref, wd_ref, gfin_ref, y_ref):
    x2 = (x1_ref[...]
          + jnp.dot(yc_ref[...], wo_ref[0:CONV_DIM, :], preferred_element_type=F32)
          + jnp.dot(ya_ref[...], wo_ref[CONV_DIM:, :], preferred_element_type=F32))
    y = _ffn(_rms(x2, g2_ref[...]).astype(BF16), wg_ref, wu_ref, wd_ref)
    y_ref[...] = _rms(x2 + 0.5 * y, gfin_ref[...])


def _stage3(x1, yc, ya, wts, *, tm):
    G, T, _ = x1.shape
    const = lambda shape: pl.BlockSpec(shape, lambda g, t: (0,) * len(shape), pipeline_mode=pl.Buffered(1))
    tok = lambda cols: pl.BlockSpec((None, tm, cols), lambda g, t: (g, t, 0))
    names = ("wo", "g2", "wg2", "wu2", "wd2", "gfin")
    return pl.pallas_call(
        _stage3_body,
        grid=(G, T // tm),
        in_specs=[tok(D_MODEL), tok(CONV_DIM), tok(ATT_COLS)] + [const(wts[n].shape) for n in names],
        out_specs=tok(D_MODEL),
        out_shape=jax.ShapeDtypeStruct((G, T, D_MODEL), F32),
        compiler_params=pltpu.CompilerParams(
            dimension_semantics=("arbitrary", "arbitrary"), vmem_limit_bytes=VMEM_LIMIT),
        name="stage3",
    )(x1, yc, ya, *[wts[n] for n in names])


def _ffn_stream_body(with_out_proj, with_final_norm, *refs):
    refs = list(refs)
    n_x = 4 if with_out_proj else 1
    x_refs, refs = refs[:n_x], refs[n_x:]
    g_ref, wg_ref, wu_ref, wd_ref = refs[:4]
    refs = refs[4:]
    gfin_ref = refs.pop(0) if with_final_norm else None
    y_ref, wgb_ref, wub_ref, wdb_ref, x_ref, xn_ref, acc_ref = refs
    c = pl.program_id(0)

    @pl.when(c == 0)
    def _():
        if with_out_proj:
            x1_ref, yc_ref, ya_ref, wo_ref = x_refs
            x = (x1_ref[...]
                 + jnp.dot(yc_ref[...], wo_ref[0:CONV_DIM, :], preferred_element_type=F32)
                 + jnp.dot(ya_ref[...], wo_ref[CONV_DIM:, :], preferred_element_type=F32))
        else:
            x = x_refs[0][...]
        x_ref[...] = x
        xn_ref[...] = _rms(x, g_ref[...]).astype(BF16)
        acc_ref[...] = jnp.zeros(acc_ref.shape, F32)

    wg = wg_ref[...].astype(BF16)
    wu = wu_ref[...].astype(BF16)
    wd = wd_ref[...].astype(BF16)
    wgb_ref[...] = wg
    wub_ref[...] = wu
    wdb_ref[...] = wd
    xn = xn_ref[...]
    g = jnp.dot(xn, wg, preferred_element_type=F32)
    u = jnp.dot(xn, wu, preferred_element_type=F32)
    a = (g * jax.nn.sigmoid(g) * u).astype(BF16)
    acc_ref[...] += jnp.dot(a, wd, preferred_element_type=F32)

    @pl.when(c == pl.num_programs(0) - 1)
    def _():
        y = x_ref[...] + 0.5 * acc_ref[...]
        y_ref[...] = _rms(y, gfin_ref[...]) if with_final_norm else y


def _ffn_stream(x_parts, gain, wg, wu, wd, *, wo=None, gfin=None):
    m = x_parts[0].shape[0]
    const = lambda a: pl.BlockSpec(a.shape, lambda c: (0,) * a.ndim, pipeline_mode=pl.Buffered(1))
    col_chunk = pl.BlockSpec((D_MODEL, FF_CHUNK), lambda c: (0, c))
    row_chunk = pl.BlockSpec((FF_CHUNK, D_MODEL), lambda c: (c, 0))
    args = list(x_parts) + ([wo] if wo is not None else []) + [gain]
    in_specs = [const(a) for a in args] + [col_chunk, col_chunk, row_chunk]
    args += [wg, wu, wd]
    if gfin is not None:
        in_specs.append(const(gfin))
        args.append(gfin)
    return pl.pallas_call(
        functools.partial(_ffn_stream_body, wo is not None, gfin is not None),
        grid=(N_FF_CHUNKS,),
        in_specs=in_specs,
        out_specs=[pl.BlockSpec((m, D_MODEL), lambda c: (0, 0)), col_chunk, col_chunk, row_chunk],
        out_shape=[jax.ShapeDtypeStruct((m, D_MODEL), F32), jax.ShapeDtypeStruct(wg.shape, BF16),
                   jax.ShapeDtypeStruct(wu.shape, BF16), jax.ShapeDtypeStruct(wd.shape, BF16)],
        scratch_shapes=[pltpu.VMEM((m, D_MODEL), F32), pltpu.VMEM((m, D_MODEL), BF16),
                        pltpu.VMEM((m, D_MODEL), F32)],
        compiler_params=pltpu.CompilerParams(dimension_semantics=("arbitrary",), vmem_limit_bytes=VMEM_LIMIT),
        name="ffn_stream_out" if wo is not None else "ffn_stream_in",
    )(*args)


def _memkv_body(mem_ref, gmem_ref, w_ref, gk_ref, mk_ref, mv_ref, mkb_ref, mvb_ref):
    kv = jnp.dot(_rms(mem_ref[...], gmem_ref[...]).astype(BF16), w_ref[...], preferred_element_type=F32)
    for g in range(MEM_HEADS // 2):
        sl = slice(g * PAIR, (g + 1) * PAIR)
        mk = _pair_headnorm(kv[:, sl], gk_ref[:, sl])
        mk_ref[sl, :] = mk.T
        mkb_ref[:, sl] = mk.astype(BF16)
    mv = kv[:, MEM_DIM:]
    mv_ref[...] = mv.T
    mvb_ref[...] = mv.astype(BF16)


def _memkv(mem, wts):
    B = mem.shape[0]
    const = lambda shape: pl.BlockSpec(shape, lambda b: (0,) * len(shape))
    blk = lambda cols: pl.BlockSpec((None, N_MEM, cols), lambda b: (b, 0, 0))
    shp = lambda dt: jax.ShapeDtypeStruct((B, N_MEM, MEM_DIM), dt)
    names = ("gmem", "wmem", "gkm")
    return pl.pallas_call(
        _memkv_body,
        grid=(B,),
        in_specs=[blk(D_MODEL)] + [const(wts[n].shape) for n in names],
        out_specs=[blk(MEM_DIM)] * 4,
        out_shape=[shp(F32), shp(F32), shp(BF16), shp(BF16)],
        compiler_params=pltpu.CompilerParams(dimension_semantics=("arbitrary",)),
        name="memkv",
    )(mem, *[wts[n] for n in names])


def _head_masks():
    lane = lax.broadcasted_iota(jnp.int32, (1, LANES), 1)
    lo = lane < HEAD_DIM
    return lo, _idiv(lane, AUG_GROUP)


def _prompt_att_body(tq, tk, iq_ref, ik_ref, qa_ref, ka_ref, vb_ref, qm_ref, mk_ref, mv_ref,
                     o_ref, m_ref, acc_ref):
    p = pl.program_id(1)
    iq = iq_ref[p]
    ik = ik_ref[p]
    lo, aug_head = _head_masks()

    @pl.when(ik == 0)
    def _():
        m_ref[...] = jnp.full(m_ref.shape, NEG, F32)
        acc_ref[...] = jnp.zeros(acc_ref.shape, F32)

    def step(masked, r0, sub):
        rows = tq - r0
        keys = slice(sub * tk, (sub + 1) * tk)
        if masked:
            visible = (lax.broadcasted_iota(jnp.int32, (tk, tk), 1)
                       <= lax.broadcasted_iota(jnp.int32, (tk, tk), 0))
        q_aug = qa_ref[r0:, FOX_DIM:QK_COLS]
        k_aug = ka_ref[keys, FOX_DIM:QK_COLS]
        ones = jnp.ones((tk, LANES), BF16)
        for h in range(FOX_HEADS):
            sl = slice((h // 2) * PAIR, (h // 2 + 1) * PAIR)
            mine = lo if h % 2 == 0 else jnp.logical_not(lo)
            qh = jnp.concatenate([jnp.where(mine, qa_ref[r0:, sl], 0.0).astype(BF16),
                                  jnp.where(aug_head == h, q_aug, 0.0).astype(BF16)], axis=1)
            kh = jnp.concatenate([ka_ref[keys, sl], k_aug], axis=1)
            s = _dot_nt(qh, kh)
            if masked:
                top = jnp.where(visible, s[:tk], NEG)
                s = top if rows == tk else jnp.concatenate([top, s[tk:]], axis=0)
            blocks = [s[:, c * LANES:(c + 1) * LANES] for c in range(tk // LANES)]
            blk_max = functools.reduce(jnp.maximum, blocks)
            m_prev = m_ref[h, r0:, :]
            m_new = jnp.maximum(m_prev, jnp.max(blk_max, axis=-1, keepdims=True))
            alpha = jnp.exp2(m_prev - m_new)
            pexp = jnp.concatenate([jnp.exp2(b - m_new) for b in blocks], axis=1).astype(BF16)
            pv = jnp.dot(pexp, jnp.concatenate([vb_ref[keys, sl], ones], axis=1), preferred_element_type=F32)
            acc_ref[h, r0:, :] = jnp.concatenate([alpha, alpha], axis=1) * acc_ref[h, r0:, :] + pv
            m_ref[h, r0:, :] = m_new

    @pl.when(ik < iq)
    def _():
        for sub in range(tq // tk):
            step(False, 0, sub)

    @pl.when(ik == iq)
    def _():
        for sub in range(tq // tk):
            step(True, sub * tk, sub)
        for g in range(FOX_HEADS // 2):
            even = acc_ref[2 * g, :, 0:PAIR] / acc_ref[2 * g, :, PAIR:2 * PAIR]
            odd = acc_ref[2 * g + 1, :, 0:PAIR] / acc_ref[2 * g + 1, :, PAIR:2 * PAIR]
            o_ref[:, g * PAIR:(g + 1) * PAIR] = jnp.where(lo, even, odd).astype(BF16)
        for g in range(MEM_HEADS // 2):
            sl = slice(g * PAIR, (g + 1) * PAIR)
            outs = []
            for mine in (lo, jnp.logical_not(lo)):
                qh = jnp.where(mine, qm_ref[:, sl], 0.0).astype(BF16)
                pexp, lsum = _softmax_rows(_dot_nt(qh, mk_ref[:, sl]) * SCALE)
                outs.append(jnp.dot(pexp.astype(BF16), mv_ref[:, sl], preferred_element_type=F32) / lsum)
            o_ref[:, FOX_DIM + g * PAIR:FOX_DIM + (g + 1) * PAIR] = jnp.where(lo, outs[0], outs[1]).astype(BF16)


def _prompt_attention(qa, ka, vb, qm, mkb, mvb, *, tq, tk):
    B, T, _ = qa.shape
    assert tq % tk == 0 and T % tq == 0
    pairs = [(i, j) for i in range(T // tq) for j in range(i + 1)]
    iq_tab = jnp.asarray([p[0] for p in pairs], jnp.int32)
    ik_tab = jnp.asarray([p[1] for p in pairs], jnp.int32)
    grid_spec = pltpu.PrefetchScalarGridSpec(
        num_scalar_prefetch=2,
        grid=(B, len(pairs)),
        in_specs=[
            pl.BlockSpec((None, tq, QK_COLS), lambda b, p, iq, ik: (b, iq[p], 0)),
            pl.BlockSpec((None, tq, QK_COLS), lambda b, p, iq, ik: (b, ik[p], 0)),
            pl.BlockSpec((None, tq, FOX_DIM), lambda b, p, iq, ik: (b, ik[p], 0)),
            pl.BlockSpec((None, tq, MEM_DIM), lambda b, p, iq, ik: (b, iq[p], 0)),
            pl.BlockSpec((None, N_MEM, MEM_DIM), lambda b, p, iq, ik: (b, 0, 0)),
            pl.BlockSpec((None, N_MEM, MEM_DIM), lambda b, p, iq, ik: (b, 0, 0)),
        ],
        out_specs=pl.BlockSpec((None, tq, ATT_COLS), lambda b, p, iq, ik: (b, iq[p], 0)),
        scratch_shapes=[pltpu.VMEM((FOX_HEADS, tq, LANES), F32), pltpu.VMEM((FOX_HEADS, tq, 2 * PAIR), F32)],
    )
    return pl.pallas_call(
        functools.partial(_prompt_att_body, tq, tk),
        grid_spec=grid_spec,
        out_shape=jax.ShapeDtypeStruct((B, T, ATT_COLS), BF16),
        compiler_params=pltpu.CompilerParams(
            dimension_semantics=("arbitrary", "arbitrary"), vmem_limit_bytes=VMEM_LIMIT),
        name="prompt_attention",
    )(iq_tab, ik_tab, qa, ka, vb, qm, mkb, mvb)


def _diag_blocks(x, nh, rows, width):
    head = _idiv(lax.broadcasted_iota(jnp.int32, (1, nh * width), 1), width)
    out = None
    for h in range(nh):
        part = jnp.where(head == h, x[h * rows:(h + 1) * rows, :], 0.0)
        out = part if out is None else out + part
    return out


def _sample_att_body(seq, past, per_step, *refs):
    for i in range(per_step):
        _sample_att_one(seq, past, *[r.at[i] for r in refs])


def _sample_att_one(seq, past, qa_ref, ka_ref, vb_ref, qm_ref, kt_ref, vt_ref, lt_ref, mkt_ref, mvt_ref, o_ref):
    blk = LANES
    nblk = past // blk
    lt = lt_ref[...]
    x = jnp.concatenate([lt[:, b * blk:(b + 1) * blk] for b in range(nblk)], axis=0)
    n = x.shape[0]
    parts = jnp.concatenate(_split3(x), axis=0).astype(BF16)
    src = lax.broadcasted_iota(jnp.int32, (blk, blk), 0)
    dst = lax.broadcasted_iota(jnp.int32, (blk, blk), 1)
    later = jnp.where(src > dst, 1.0, 0.0).astype(BF16)
    loc = jnp.dot(parts, later, preferred_element_type=F32)
    tot = jnp.dot(parts, jnp.ones((blk, blk), BF16), preferred_element_type=F32)
    loc = loc[0:n] + loc[n:2 * n] + loc[2 * n:3 * n]
    tot = tot[0:n] + tot[n:2 * n] + tot[2 * n:3 * n]
    running = jnp.zeros((FOX_HEADS, blk), F32)
    suffix = [None] * nblk
    for b in reversed(range(nblk)):
        rows = slice(b * FOX_HEADS, (b + 1) * FOX_HEADS)
        suffix[b] = loc[rows] + running
        running = running + tot[rows]
    rt = jnp.concatenate(suffix, axis=1) * LOG2E
    bias = jnp.concatenate([jnp.broadcast_to(rt[h:h + 1, :], (seq, past)) for h in range(FOX_HEADS)], axis=0)

    nrow = FOX_HEADS * seq
    lane = lax.broadcasted_iota(jnp.int32, (nrow, QK_COLS), 1)
    lane_head = jnp.where(lane < FOX_DIM, _idiv(lane, HEAD_DIM), _idiv(lane - FOX_DIM, AUG_GROUP))
    row_head = _idiv(lax.broadcasted_iota(jnp.int32, (nrow, QK_COLS), 0), seq)
    qbd = jnp.where(lane_head == row_head, jnp.concatenate([qa_ref[...]] * FOX_HEADS, axis=0), 0.0).astype(BF16)
    aug_row = _imod(lax.broadcasted_iota(jnp.int32, (AUG, past), 0), AUG_GROUP)
    kt = jnp.concatenate([kt_ref[...].astype(BF16), jnp.where(aug_row < N_SPLIT, 1.0, 0.0).astype(BF16)], axis=0)
    s_past = jnp.dot(qbd, kt, preferred_element_type=F32) + bias
    s_new = _dot_nt(qbd, ka_ref[...])
    qi = _imod(lax.broadcasted_iota(jnp.int32, (nrow, seq), 0), seq)
    kj = lax.broadcasted_iota(jnp.int32, (nrow, seq), 1)
    s_new = jnp.where(kj <= qi, s_new, NEG)
    m = jnp.maximum(jnp.max(s_past, axis=-1, keepdims=True), jnp.max(s_new, axis=-1, keepdims=True))
    p_past = jnp.exp2(s_past - m)
    p_new = jnp.exp2(s_new - m)
    lsum = jnp.sum(p_past, axis=-1, keepdims=True) + jnp.sum(p_new, axis=-1, keepdims=True)
    o = (_dot_nt(p_past.astype(BF16), vt_ref[...].astype(BF16))
         + jnp.dot(p_new.astype(BF16), vb_ref[...], preferred_element_type=F32)) / lsum
    o_ref[:, 0:FOX_DIM] = _diag_blocks(o, FOX_HEADS, seq, HEAD_DIM).astype(BF16)

    nrow_m = MEM_HEADS * seq
    lane_m = _idiv(lax.broadcasted_iota(jnp.int32, (nrow_m, MEM_DIM), 1), HEAD_DIM)
    row_m = _idiv(lax.broadcasted_iota(jnp.int32, (nrow_m, MEM_DIM), 0), seq)
    qbd_m = jnp.where(lane_m == row_m, jnp.concatenate([qm_ref[...]] * MEM_HEADS, axis=0), 0.0).astype(BF16)
    pm, lm = _softmax_rows(jnp.dot(qbd_m, mkt_ref[...].astype(BF16), preferred_element_type=F32) * SCALE)
    om = _dot_nt(pm.astype(BF16), mvt_ref[...].astype(BF16)) / lm
    o_ref[:, FOX_DIM:ATT_COLS] = _diag_blocks(om, MEM_HEADS, seq, HEAD_DIM).astype(BF16)


def _sample_attention(qa, ka, vb, qm, pk, pv, plf, mk, mv):
    nbatch, seq, _ = qa.shape
    past = pk.shape[2]
    per_step = SAMPLE_SEQS_PER_STEP
    assert nbatch % per_step == 0
    blk = lambda rows, cols: pl.BlockSpec((per_step, rows, cols), lambda b: (b, 0, 0))
    return pl.pallas_call(
        functools.partial(_sample_att_body, seq, past, per_step),
        grid=(nbatch // per_step,),
        in_specs=[blk(seq, QK_COLS), blk(seq, QK_COLS), blk(seq, FOX_DIM), blk(seq, MEM_DIM),
                  blk(FOX_DIM, past), blk(FOX_DIM, past), blk(FOX_HEADS, past),
                  blk(MEM_DIM, N_MEM), blk(MEM_DIM, N_MEM)],
        out_specs=blk(seq, ATT_COLS),
        out_shape=jax.ShapeDtypeStruct((nbatch, seq, ATT_COLS), BF16),
        compiler_params=pltpu.CompilerParams(
            dimension_semantics=("arbitrary",), vmem_limit_bytes=VMEM_LIMIT),
        name="sample_attention",
    )(qa, ka, vb, qm, pk, pv, plf, mk, mv)


def _prep_weights(norm_ffn1, norm_mix, w_in, b_forget, conv_w, conv_b,
                  q_norm_fox, k_norm_fox, q_norm_mem, k_norm_mem, norm_mem, w_mem_kv, w_out,
                  norm_ffn2, norm_final):
    row = lambda v: v.reshape(1, -1).astype(F32)
    return {
        "g1": row(norm_ffn1), "gmix": row(norm_mix), "win32t": w_in.T,
        "conv_w": conv_w.astype(F32), "conv_b": row(conv_b),
        "gq": row(jnp.tile(q_norm_fox, FOX_HEADS)) * (SCALE * LOG2E), "gk": row(jnp.tile(k_norm_fox, FOX_HEADS)),
        "gqm": row(jnp.tile(q_norm_mem, MEM_HEADS)),
        "bf_rep": row(jnp.repeat(b_forget, AUG_GROUP)),
        "bf": jnp.pad(row(b_forget), ((0, 0), (0, LANES - FOX_HEADS))),
        "gmem": row(norm_mem), "wmem": w_mem_kv.astype(BF16), "gkm": row(jnp.tile(k_norm_mem, MEM_HEADS)),
        "wo": w_out.astype(BF16), "g2": row(norm_ffn2), "gfin": row(norm_final),
    }


def kernel(x_prompt, x_sample, cache_fox_k, cache_fox_v, cache_fox_logf, state_conv, cache_mem_k, cache_mem_v, mem_prompt, norm_ffn1, w_ffn1_gate, w_ffn1_up, w_ffn1_down, norm_mix, w_in, b_forget, conv_w, conv_b, q_norm_fox, k_norm_fox, q_norm_mem, k_norm_mem, norm_mem, w_mem_kv, w_out, norm_ffn2, w_ffn2_gate, w_ffn2_up, w_ffn2_down, norm_final):
    depth = w_in.shape[0]
    assert depth == 1, "single-layer step"
    B, T, _ = x_prompt.shape
    nbs, seq_s, _ = x_sample.shape
    past = cache_fox_k.shape[2]
    l = 0
    wts = _prep_weights(norm_ffn1[l], norm_mix[l], w_in[l], b_forget[l], conv_w[l], conv_b[l], q_norm_fox[l],
                        k_norm_fox[l], q_norm_mem[l], k_norm_mem[l], norm_mem[l], w_mem_kv[l], w_out[l],
                        norm_ffn2[l], norm_final[l])
    tile = TOKEN_TILE
    n_s = nbs * seq_s

    x1s, wts["wg1"], wts["wu1"], wts["wd1"] = _ffn_stream(
        (x_sample.reshape(n_s, D_MODEL),), wts["g1"], w_ffn1_gate[l], w_ffn1_up[l], w_ffn1_down[l])
    ycs, qas, kas, vbs, kfs, vfs, lfs, qms, ncs, wts["win"], wts["wtail"] = _stage1(
        x1s.reshape(1, n_s, D_MODEL), state_conv[l], wts, nb=tile // seq_s, seq=seq_s, carry=False)
    per_seq = lambda a: a.reshape(nbs, seq_s, a.shape[-1])
    feat_major = lambda a: a.reshape(a.shape[0], a.shape[1], -1).transpose(0, 2, 1)
    yas = _sample_attention(
        per_seq(qas), per_seq(kas), per_seq(vbs), per_seq(qms),
        feat_major(cache_fox_k[l]), feat_major(cache_fox_v[l]), feat_major(cache_fox_logf[l]),
        feat_major(cache_mem_k[l]), feat_major(cache_mem_v[l]))
    y_sample, wts["wg2"], wts["wu2"], wts["wd2"] = _ffn_stream(
        (x1s, ycs.reshape(n_s, CONV_DIM), yas.reshape(n_s, ATT_COLS)), wts["g2"],
        w_ffn2_gate[l], w_ffn2_up[l], w_ffn2_down[l], wo=wts["wo"], gfin=wts["gfin"])

    mk_p, mv_p, mkb, mvb = _memkv(mem_prompt, wts)
    x1, yc, qa, ka, vb, kf, vf, lf, qm, nc = _stage1(x_prompt, None, wts, nb=1, seq=tile, carry=True)
    ya = _prompt_attention(qa, ka, vb, qm, mkb, mvb, tq=ATT_QUERY_TILE, tk=ATT_KEY_TILE)
    y_prompt = _stage3(x1, yc, ya, wts, tm=tile)

    heads = lambda a, b, t, nh: a.reshape(1, b, t, nh, HEAD_DIM)
    token_major = lambda a, nh: a.reshape(a.shape[0], nh, -1, a.shape[2]).transpose(0, 3, 1, 2)[None]
    return (y_prompt, y_sample.reshape(nbs, seq_s, D_MODEL),
            token_major(kf, FOX_HEADS), token_major(vf, FOX_HEADS), lf.transpose(0, 2, 1)[None],
            nc.reshape(1, B, CONV_K - 1, CONV_DIM),
            token_major(mk_p, MEM_HEADS), token_major(mv_p, MEM_HEADS),
            heads(kfs, nbs, seq_s, FOX_HEADS), heads(vfs, nbs, seq_s, FOX_HEADS),
            lfs.reshape(1, nbs, seq_s, FOX_HEADS), ncs.reshape(1, nbs, CONV_K - 1, CONV_DIM))
```

```python
import functools

import jax
import jax.numpy as jnp
from jax import lax
from jax.experimental import pallas as pl
from jax.experimental.pallas import tpu as pltpu

F32 = jnp.float32
BF16 = jnp.bfloat16

D_MODEL = 1024
HEAD_DIM = 64
CONV_DIM = 256
CONV_K = 3
FOX_HEADS = 8
FOX_DIM = FOX_HEADS * HEAD_DIM
MEM_HEADS = 4
MEM_DIM = MEM_HEADS * HEAD_DIM
N_MEM = 256
D_FF = 2816
EPS = 1e-6

LANES = 128
PAIR = 2 * HEAD_DIM
FF_CHUNK = 256
N_FF_CHUNKS = D_FF // FF_CHUNK
AUG = LANES
AUG_GROUP = AUG // FOX_HEADS
N_SPLIT = 3
QK_COLS = FOX_DIM + AUG
ATT_COLS = FOX_DIM + MEM_DIM
COL_Q = 3 * CONV_DIM
COL_K = COL_Q + FOX_DIM
COL_V = COL_K + FOX_DIM
COL_F = COL_V + FOX_DIM
COL_QM = COL_F + FOX_HEADS
TAIL_COLS = MEM_DIM + AUG + LANES
NEG = -1e30
SCALE = HEAD_DIM ** -0.5
LOG2E = 1.4426950408889634
VMEM_LIMIT = 56 * 1024 * 1024
TOKEN_TILE = 512
ATT_KEY_TILE = 512
ATT_QUERY_TILE = 1024
SAMPLE_SEQS_PER_STEP = 4


def _idiv(x, d):
    if d & (d - 1) == 0:
        return lax.shift_right_logical(x, d.bit_length() - 1)
    return x // d


def _imod(x, d):
    if d & (d - 1) == 0:
        return x & (d - 1)
    return x % d


def _rms(x, g):
    ms = jnp.mean(x * x, axis=-1, keepdims=True)
    return x * lax.rsqrt(ms + EPS) * g


def _pair_headnorm(x, g):
    lo = lax.broadcasted_iota(jnp.int32, (1, PAIR), 1) < HEAD_DIM
    x2 = x * x
    s_lo = jnp.sum(jnp.where(lo, x2, 0.0), axis=-1, keepdims=True)
    s_hi = jnp.sum(jnp.where(lo, 0.0, x2), axis=-1, keepdims=True)
    r = jnp.where(lo, lax.rsqrt(s_lo * (1.0 / HEAD_DIM) + EPS),
                  lax.rsqrt(s_hi * (1.0 / HEAD_DIM) + EPS))
    return x * r * g


def _store_cols(ref, sl, x, transposed):
    if transposed:
        ref[sl, :] = x.T
    else:
        ref[:, sl] = x


def _split3(x):
    hi = x.astype(BF16).astype(F32)
    r = x - hi
    mid = r.astype(BF16).astype(F32)
    lo = (r - mid).astype(BF16).astype(F32)
    return hi, mid, lo


def _ffn(xn, wg_ref, wu_ref, wd_ref):
    acc = None
    for c in range(N_FF_CHUNKS):
        sl = slice(c * FF_CHUNK, (c + 1) * FF_CHUNK)
        g = jnp.dot(xn, wg_ref[:, sl], preferred_element_type=F32)
        u = jnp.dot(xn, wu_ref[:, sl], preferred_element_type=F32)
        a = (g * jax.nn.sigmoid(g) * u).astype(BF16)
        d = jnp.dot(a, wd_ref[sl, :], preferred_element_type=F32)
        acc = d if acc is None else acc + d
    return acc


def _log_sigmoid(x):
    return jnp.minimum(x, 0.0) - jnp.log1p(jnp.exp(-jnp.abs(x)))


def _softmax_rows(s):
    m = jnp.max(s, axis=-1, keepdims=True)
    p = jnp.exp(s - m)
    return p, jnp.sum(p, axis=-1, keepdims=True)


def _dot_nt(a, b):
    return lax.dot_general(a, b, (((1,), (1,)), ((), ())), preferred_element_type=F32)


_STAGE1_SHARED_IN = ("gmix", "conv_w", "conv_b", "gq", "gk", "gqm", "bf_rep", "bf")
_STAGE1_SHARED_OUT = ("yc", "qa", "ka", "vb", "kf", "vf", "lf", "qm", "nc")
_STAGE1_IN = {True: ("x", "g1", "wg1", "wu1", "wd1", "win", "wtail") + _STAGE1_SHARED_IN,
              False: ("x1", "state", "win32t") + _STAGE1_SHARED_IN}
_STAGE1_OUT = {True: ("x1",) + _STAGE1_SHARED_OUT, False: _STAGE1_SHARED_OUT + ("win", "wtail")}


def _stage1_body(nb, seq, carry, *refs):
    tm = nb * seq
    names = _STAGE1_IN[carry] + _STAGE1_OUT[carry]
    r = dict(zip(names, refs[:len(names)]))
    cs_ref, cum_ref = refs[len(names):]
    gmix_ref, cw_ref, cb_ref, gq_ref, gk_ref, gqm_ref, bfr_ref, bf_ref = (r[n] for n in _STAGE1_SHARED_IN)
    yc_ref, qa_ref, ka_ref, vb_ref, kf_ref, vf_ref, lf_ref, qm_ref, nc_ref = (r[n] for n in _STAGE1_SHARED_OUT)
    win_ref, wtail_ref = r["win"], r["wtail"]
    if carry:
        x = r["x"][...]
        y = _ffn(_rms(x, r["g1"][...]).astype(BF16), r["wg1"], r["wu1"], r["wd1"])
        x1 = x + 0.5 * y
        r["x1"][...] = x1
    else:
        x1 = r["x1"][...]
        st_ref = r["state"]

        @pl.when(pl.program_id(1) == 0)
        def _():
            w_t = r["win32t"]
            win_ref[...] = w_t[0:COL_F, :].T.astype(BF16)
            f_rows = w_t[COL_F:COL_QM, :]
            f_rep = jnp.concatenate([jnp.broadcast_to(f_rows[hd:hd + 1, :], (AUG_GROUP, D_MODEL))
                                     for hd in range(FOX_HEADS)], axis=0)
            tail_t = jnp.concatenate([w_t[COL_QM:COL_QM + MEM_DIM, :], f_rep, f_rows,
                                      jnp.zeros((LANES - FOX_HEADS, D_MODEL), F32)], axis=0)
            wtail_ref[...] = tail_t.T.astype(BF16)
    h = _rms(x1, gmix_ref[...]).astype(BF16)

    ucb = jnp.dot(h, win_ref[:, 0:COL_Q], preferred_element_type=F32)
    ci = ucb[:, CONV_DIM:2 * CONV_DIM] * ucb[:, 0:CONV_DIM]
    if carry:
        @pl.when(pl.program_id(1) == 0)
        def _():
            cs_ref[:, 0:8, :] = jnp.zeros((nb, 8, CONV_DIM), F32)
    else:
        cs_ref[:, 8 - (CONV_K - 1):8, :] = st_ref[...]
    cs_ref[:, 8:8 + seq, :] = ci.reshape(nb, seq, CONV_DIM)
    conv = cb_ref[...] + cw_ref[CONV_K - 1:CONV_K, :] * ci
    for i in range(CONV_K - 1):
        shifted = cs_ref[:, 8 - (CONV_K - 1) + i:8 - (CONV_K - 1) + i + seq, :]
        conv = conv + cw_ref[i:i + 1, :] * shifted.reshape(tm, CONV_DIM)
    yc_ref[...] = (ucb[:, 2 * CONV_DIM:3 * CONV_DIM] * conv).astype(BF16)
    tail = cs_ref[:, seq:seq + 8, :]
    nc_ref[...] = tail[:, 8 - (CONV_K - 1):, :].reshape(nc_ref.shape)
    if carry:
        cs_ref[:, 0:8, :] = tail

    for name, col, g_ref, out_ref in (("q", COL_Q, gq_ref, qa_ref), ("k", COL_K, gk_ref, ka_ref)):
        pr = jnp.dot(h, win_ref[:, col:col + FOX_DIM], preferred_element_type=F32)
        for g in range(FOX_HEADS // 2):
            sl = slice(g * PAIR, (g + 1) * PAIR)
            xn = _pair_headnorm(pr[:, sl], g_ref[:, sl])
            if name == "k":
                _store_cols(kf_ref, sl, xn, carry)
            out_ref[:, sl] = xn.astype(BF16)
    v = jnp.dot(h, win_ref[:, COL_V:COL_V + FOX_DIM], preferred_element_type=F32)
    for g in range(FOX_HEADS // 2):
        sl = slice(g * PAIR, (g + 1) * PAIR)
        _store_cols(vf_ref, sl, v[:, sl], carry)
    vb_ref[...] = v.astype(BF16)

    tail_cols = jnp.dot(h, wtail_ref[...], preferred_element_type=F32)
    for g in range(MEM_HEADS // 2):
        sl = slice(g * PAIR, (g + 1) * PAIR)
        qm_ref[:, sl] = _pair_headnorm(tail_cols[:, sl], gqm_ref[:, sl]).astype(BF16)
    fl = tail_cols[:, MEM_DIM:]
    lf = _log_sigmoid(fl[:, AUG:2 * AUG] + bf_ref[...])
    if carry:
        lf_ref[...] = lf.T[0:FOX_HEADS, :]
    else:
        lf_ref[...] = lf[:, 0:FOX_HEADS]
    logf = _log_sigmoid(fl[:, 0:AUG] + bfr_ref[...])

    row = lax.broadcasted_iota(jnp.int32, (tm, tm), 0)
    col = lax.broadcasted_iota(jnp.int32, (tm, tm), 1)
    tri = col <= row
    if nb > 1:
        tri = jnp.logical_and(tri, _idiv(row, seq) == _idiv(col, seq))
    tri = jnp.where(tri, 1.0, 0.0).astype(BF16)
    hi, mid, lo = _split3(logf)
    parts = jnp.concatenate([hi.astype(BF16), mid.astype(BF16), lo.astype(BF16)], axis=1)
    cs = jnp.dot(tri, parts, preferred_element_type=F32)
    cum = cs[:, 0:AUG] + cs[:, AUG:2 * AUG] + cs[:, 2 * AUG:3 * AUG]
    if carry:
        @pl.when(pl.program_id(1) == 0)
        def _():
            cum_ref[...] = jnp.zeros(cum_ref.shape, F32)
        cum = cum + cum_ref[0:1, :]
        cum_ref[0:1, :] = cum[tm - 1:tm, :]

    j = _imod(lax.broadcasted_iota(jnp.int32, (1, AUG), 1), AUG_GROUP)
    terms = _split3(cum * LOG2E)
    aq = jnp.where(j < 2 * N_SPLIT, 1.0, 0.0)
    ak = jnp.where(j < N_SPLIT, 1.0, 0.0)
    for i, term in enumerate(terms):
        aq = jnp.where(j == i, term, aq)
        ak = jnp.where(j == N_SPLIT + i, -term, ak)
    qa_ref[:, FOX_DIM:QK_COLS] = aq.astype(BF16)
    ka_ref[:, FOX_DIM:QK_COLS] = ak.astype(BF16)


def _stage1(x3, state, wts, *, nb, seq, carry):
    G, T, _ = x3.shape
    tm = nb * seq
    nt = T // tm
    const = lambda shape: pl.BlockSpec(shape, lambda g, t: (0,) * len(shape), pipeline_mode=pl.Buffered(1))
    tok = lambda cols: pl.BlockSpec((None, tm, cols), lambda g, t: (g, t, 0))
    if carry:
        feat = lambda rows: pl.BlockSpec((None, rows, tm), lambda g, t: (g, 0, t))
        feat_shape = lambda rows: jax.ShapeDtypeStruct((G, rows, T), F32)
    else:
        feat = lambda rows: pl.BlockSpec((None, tm, rows), lambda g, t: (g, t, 0))
        feat_shape = lambda rows: jax.ShapeDtypeStruct((G, T, rows), F32)
    shp = lambda cols, dt: jax.ShapeDtypeStruct((G, T, cols), dt)
    if carry:
        nc_spec = pl.BlockSpec((None, CONV_K - 1, CONV_DIM), lambda g, t: (g, 0, 0))
        nc_shape = jax.ShapeDtypeStruct((G, CONV_K - 1, CONV_DIM), F32)
    else:
        nc_spec = pl.BlockSpec((nb, CONV_K - 1, CONV_DIM), lambda g, t: (t, 0, 0))
        nc_shape = jax.ShapeDtypeStruct((nt * nb, CONV_K - 1, CONV_DIM), F32)
    tiled_in = {"x": (tok(D_MODEL), x3), "x1": (tok(D_MODEL), x3), "state": (nc_spec, state)}
    in_specs, args = [], []
    for n in _STAGE1_IN[carry]:
        spec, arg = tiled_in[n] if n in tiled_in else (const(wts[n].shape), wts[n])
        in_specs.append(spec)
        args.append(arg)
    outs = {"x1": (tok(D_MODEL), shp(D_MODEL, F32)), "yc": (tok(CONV_DIM), shp(CONV_DIM, BF16)),
            "qa": (tok(QK_COLS), shp(QK_COLS, BF16)), "ka": (tok(QK_COLS), shp(QK_COLS, BF16)),
            "vb": (tok(FOX_DIM), shp(FOX_DIM, BF16)), "kf": (feat(FOX_DIM), feat_shape(FOX_DIM)),
            "vf": (feat(FOX_DIM), feat_shape(FOX_DIM)), "lf": (feat(FOX_HEADS), feat_shape(FOX_HEADS)),
            "qm": (tok(MEM_DIM), shp(MEM_DIM, BF16)), "nc": (nc_spec, nc_shape)}
    if not carry:
        for n, w_shape in (("win", (D_MODEL, COL_F)), ("wtail", (D_MODEL, TAIL_COLS))):
            outs[n] = (pl.BlockSpec(w_shape, lambda g, t: (0, 0)), jax.ShapeDtypeStruct(w_shape, BF16))
    out_specs = [outs[n][0] for n in _STAGE1_OUT[carry]]
    out_shape = [outs[n][1] for n in _STAGE1_OUT[carry]]
    return pl.pallas_call(
        functools.partial(_stage1_body, nb, seq, carry),
        grid=(G, nt),
        in_specs=in_specs,
        out_specs=out_specs,
        out_shape=out_shape,
        scratch_shapes=[pltpu.VMEM((nb, seq + 8, CONV_DIM), F32), pltpu.VMEM((8, AUG), F32)],
        compiler_params=pltpu.CompilerParams(
            dimension_semantics=("arbitrary", "arbitrary"), vmem_limit_bytes=VMEM_LIMIT),
        name="stage1_carry" if carry else "stage1_batched",
    )(*args)


def _stage3_body(x1_ref, yc_ref, ya_ref, wo_ref, g2_ref, wg_ref, wu_ref, wd_ref, gfin_ref, y_ref):
    x2 = (x1_ref[...]
          + jnp.dot(yc_ref[...], wo_ref[0:CONV_DIM, :], preferred_element_type=F32)
          + jnp.dot(ya_ref[...], wo_ref[CONV_DIM:, :], preferred_element_type=F32))
    y = _ffn(_rms(x2, g2_ref[...]).astype(BF16), wg_ref, wu_ref, wd_ref)
    y_ref[...] = _rms(x2 + 0.5 * y, gfin_ref[...])


def _stage3(x1, yc, ya, wts, *, tm):
    G, T, _ = x1.shape
    const = lambda shape: pl.BlockSpec(shape, lambda g, t: (0,) * len(shape), pipeline_mode=pl.Buffered(1))
    tok = lambda cols: pl.BlockSpec((None, tm, cols), lambda g, t: (g, t, 0))
    names = ("wo", "g2", "wg2", "wu2", "wd2", "gfin")
    return pl.pallas_call(
        _stage3_body,
        grid=(G, T // tm),
        in_specs=[tok(D_MODEL), tok(CONV_DIM), tok(ATT_COLS)] + [const(wts[n].shape) for n in names],
        out_specs=tok(D_MODEL),
        out_shape=jax.ShapeDtypeStruct((G, T, D_MODEL), F32),
        compiler_params=pltpu.CompilerParams(
            dimension_semantics=("arbitrary", "arbitrary"), vmem_limit_bytes=VMEM_LIMIT),
        name="stage3",
    )(x1, yc, ya, *[wts[n] for n in names])


def _ffn_stream_body(with_out_proj, with_final_norm, *refs):
    refs = list(refs)
    n_x = 4 if with_out_proj else 1
    x_refs, refs = refs[:n_x], refs[n_x:]
    g_ref, wg_ref, wu_ref, wd_ref = refs[:4]
    refs = refs[4:]
    gfin_ref = refs.pop(0) if with_final_norm else None
    y_ref, wgb_ref, wub_ref, wdb_ref, x_ref, xn_ref, acc_ref = refs
    c = pl.program_id(0)

    @pl.when(c == 0)
    def _():
        if with_out_proj:
            x1_ref, yc_ref, ya_ref, wo_ref = x_refs
            x = (x1_ref[...]
                 + jnp.dot(yc_ref[...], wo_ref[0:CONV_DIM, :], preferred_element_type=F32)
                 + jnp.dot(ya_ref[...], wo_ref[CONV_DIM:, :], preferred_element_type=F32))
        else:
            x = x_refs[0][...]
        x_ref[...] = x
        xn_ref[...] = _rms(x, g_ref[...]).astype(BF16)
        acc_ref[...] = jnp.zeros(acc_ref.shape, F32)

    wg = wg_ref[...].astype(BF16)
    wu = wu_ref[...].astype(BF16)
    wd = wd_ref[...].astype(BF16)
    wgb_ref[...] = wg
    wub_ref[...] = wu
    wdb_ref[...] = wd
    xn = xn_ref[...]
    g = jnp.dot(xn, wg, preferred_element_type=F32)
    u = jnp.dot(xn, wu, preferred_element_type=F32)
    a = (g * jax.nn.sigmoid(g) * u).astype(BF16)
    acc_ref[...] += jnp.dot(a, wd, preferred_element_type=F32)

    @pl.when(c == pl.num_programs(0) - 1)
    def _():
        y = x_ref[...] + 0.5 * acc_ref[...]
        y_ref[...] = _rms(y, gfin_ref[...]) if with_final_norm else y


def _ffn_stream(x_parts, gain, wg, wu, wd, *, wo=None, gfin=None):
    m = x_parts[0].shape[0]
    const = lambda a: pl.BlockSpec(a.shape, lambda c: (0,) * a.ndim, pipeline_mode=pl.Buffered(1))
    col_chunk = pl.BlockSpec((D_MODEL, FF_CHUNK), lambda c: (0, c))
    row_chunk = pl.BlockSpec((FF_CHUNK, D_MODEL), lambda c: (c, 0))
    args = list(x_parts) + ([wo] if wo is not None else []) + [gain]
    in_specs = [const(a) for a in args] + [col_chunk, col_chunk, row_chunk]
    args += [wg, wu, wd]
    if gfin is not None:
        in_specs.append(const(gfin))
        args.append(gfin)
    return pl.pallas_call(
        functools.partial(_ffn_stream_body, wo is not None, gfin is not None),
        grid=(N_FF_CHUNKS,),
        in_specs=in_specs,
        out_specs=[pl.BlockSpec((m, D_MODEL), lambda c: (0, 0)), col_chunk, col_chunk, row_chunk],
        out_shape=[jax.ShapeDtypeStruct((m, D_MODEL), F32), jax.ShapeDtypeStruct(wg.shape, BF16),
                   jax.ShapeDtypeStruct(wu.shape, BF16), jax.ShapeDtypeStruct(wd.shape, BF16)],
        scratch_shapes=[pltpu.VMEM((m, D_MODEL), F32), pltpu.VMEM((m, D_MODEL), BF16),
                        pltpu.VMEM((m, D_MODEL), F32)],
        compiler_params=pltpu.CompilerParams(dimension_semantics=("arbitrary",), vmem_limit_bytes=VMEM_LIMIT),
        name="ffn_stream_out" if wo is not None else "ffn_stream_in",
    )(*args)


def _memkv_body(mem_ref, gmem_ref, w_ref, gk_ref, mk_ref, mv_ref, mkb_ref, mvb_ref):
    kv = jnp.dot(_rms(mem_ref[...], gmem_ref[...]).astype(BF16), w_ref[...], preferred_element_type=F32)
    for g in range(MEM_HEADS // 2):
        sl = slice(g * PAIR, (g + 1) * PAIR)
        mk = _pair_headnorm(kv[:, sl], gk_ref[:, sl])
        mk_ref[sl, :] = mk.T
        mkb_ref[:, sl] = mk.astype(BF16)
    mv = kv[:, MEM_DIM:]
    mv_ref[...] = mv.T
    mvb_ref[...] = mv.astype(BF16)


def _memkv(mem, wts):
    B = mem.shape[0]
    const = lambda shape: pl.BlockSpec(shape, lambda b: (0,) * len(shape))
    blk = lambda cols: pl.BlockSpec((None, N_MEM, cols), lambda b: (b, 0, 0))
    shp = lambda dt: jax.ShapeDtypeStruct((B, N_MEM, MEM_DIM), dt)
    names = ("gmem", "wmem", "gkm")
    return pl.pallas_call(
        _memkv_body,
        grid=(B,),
        in_specs=[blk(D_MODEL)] + [const(wts[n].shape) for n in names],
        out_specs=[blk(MEM_DIM)] * 4,
        out_shape=[shp(F32), shp(F32), shp(BF16), shp(BF16)],
        compiler_params=pltpu.CompilerParams(dimension_semantics=("arbitrary",)),
        name="memkv",
    )(mem, *[wts[n] for n in names])


def _head_masks():
    lane = lax.broadcasted_iota(jnp.int32, (1, LANES), 1)
    lo = lane < HEAD_DIM
    return lo, _idiv(lane, AUG_GROUP)


def _prompt_att_body(tq, tk, iq_ref, ik_ref, qa_ref, ka_ref, vb_ref, qm_ref, mk_ref, mv_ref,
                     o_ref, m_ref, acc_ref):
    p = pl.program_id(1)
    iq = iq_ref[p]
    ik = ik_ref[p]
    lo, aug_head = _head_masks()

    @pl.when(ik == 0)
    def _():
        m_ref[...] = jnp.full(m_ref.shape, NEG, F32)
        acc_ref[...] = jnp.zeros(acc_ref.shape, F32)

    def step(masked, r0, sub):
        rows = tq - r0
        keys = slice(sub * tk, (sub + 1) * tk)
        if masked:
            visible = (lax.broadcasted_iota(jnp.int32, (tk, tk), 1)
                       <= lax.broadcasted_iota(jnp.int32, (tk, tk), 0))
        q_aug = qa_ref[r0:, FOX_DIM:QK_COLS]
        k_aug = ka_ref[keys, FOX_DIM:QK_COLS]
        ones = jnp.ones((tk, LANES), BF16)
        for h in range(FOX_HEADS):
            sl = slice((h // 2) * PAIR, (h // 2 + 1) * PAIR)
            mine = lo if h % 2 == 0 else jnp.logical_not(lo)
            qh = jnp.concatenate([jnp.where(mine, qa_ref[r0:, sl], 0.0).astype(BF16),
                                  jnp.where(aug_head == h, q_aug, 0.0).astype(BF16)], axis=1)
            kh = jnp.concatenate([ka_ref[keys, sl], k_aug], axis=1)
            s = _dot_nt(qh, kh)
            if masked:
                top = jnp.where(visible, s[:tk], NEG)
                s = top if rows == tk else jnp.concatenate([top, s[tk:]], axis=0)
            blocks = [s[:, c * LANES:(c + 1) * LANES] for c in range(tk // LANES)]
            blk_max = functools.reduce(jnp.maximum, blocks)
            m_prev = m_ref[h, r0:, :]
            m_new = jnp.maximum(m_prev, jnp.max(blk_max, axis=-1, keepdims=True))
            alpha = jnp.exp2(m_prev - m_new)
            pexp = jnp.concatenate([jnp.exp2(b - m_new) for b in blocks], axis=1).astype(BF16)
            pv = jnp.dot(pexp, jnp.concatenate([vb_ref[keys, sl], ones], axis=1), preferred_element_type=F32)
            acc_ref[h, r0:, :] = jnp.concatenate([alpha, alpha], axis=1) * acc_ref[h, r0:, :] + pv
            m_ref[h, r0:, :] = m_new

    @pl.when(ik < iq)
    def _():
        for sub in range(tq // tk):
            step(False, 0, sub)

    @pl.when(ik == iq)
    def _():
        for sub in range(tq // tk):
            step(True, sub * tk, sub)
        for g in range(FOX_HEADS // 2):
            even = acc_ref[2 * g, :, 0:PAIR] / acc_ref[2 * g, :, PAIR:2 * PAIR]
            odd = acc_ref[2 * g + 1, :, 0:PAIR] / acc_ref[2 * g + 1, :, PAIR:2 * PAIR]
            o_ref[:, g * PAIR:(g + 1) * PAIR] = jnp.where(lo, even, odd).astype(BF16)
        for g in range(MEM_HEADS // 2):
            sl = slice(g * PAIR, (g + 1) * PAIR)
            outs = []
            for mine in (lo, jnp.logical_not(lo)):
                qh = jnp.where(mine, qm_ref[:, sl], 0.0).astype(BF16)
                pexp, lsum = _softmax_rows(_dot_nt(qh, mk_ref[:, sl]) * SCALE)
                outs.append(jnp.dot(pexp.astype(BF16), mv_ref[:, sl], preferred_element_type=F32) / lsum)
            o_ref[:, FOX_DIM + g * PAIR:FOX_DIM + (g + 1) * PAIR] = jnp.where(lo, outs[0], outs[1]).astype(BF16)


def _prompt_attention(qa, ka, vb, qm, mkb, mvb, *, tq, tk):
    B, T, _ = qa.shape
    assert tq % tk == 0 and T % tq == 0
    pairs = [(i, j) for i in range(T // tq) for j in range(i + 1)]
    iq_tab = jnp.asarray([p[0] for p in pairs], jnp.int32)
    ik_tab = jnp.asarray([p[1] for p in pairs], jnp.int32)
    grid_spec = pltpu.PrefetchScalarGridSpec(
        num_scalar_prefetch=2,
        grid=(B, len(pairs)),
        in_specs=[
            pl.BlockSpec((None, tq, QK_COLS), lambda b, p, iq, ik: (b, iq[p], 0)),
            pl.BlockSpec((None, tq, QK_COLS), lambda b, p, iq, ik: (b, ik[p], 0)),
            pl.BlockSpec((None, tq, FOX_DIM), lambda b, p, iq, ik: (b, ik[p], 0)),
            pl.BlockSpec((None, tq, MEM_DIM), lambda b, p, iq, ik: (b, iq[p], 0)),
            pl.BlockSpec((None, N_MEM, MEM_DIM), lambda b, p, iq, ik: (b, 0, 0)),
            pl.BlockSpec((None, N_MEM, MEM_DIM), lambda b, p, iq, ik: (b, 0, 0)),
        ],
        out_specs=pl.BlockSpec((None, tq, ATT_COLS), lambda b, p, iq, ik: (b, iq[p], 0)),
        scratch_shapes=[pltpu.VMEM((FOX_HEADS, tq, LANES), F32), pltpu.VMEM((FOX_HEADS, tq, 2 * PAIR), F32)],
    )
    return pl.pallas_call(
        functools.partial(_prompt_att_body, tq, tk),
        grid_spec=grid_spec,
        out_shape=jax.ShapeDtypeStruct((B, T, ATT_COLS), BF16),
        compiler_params=pltpu.CompilerParams(
            dimension_semantics=("arbitrary", "arbitrary"), vmem_limit_bytes=VMEM_LIMIT),
        name="prompt_attention",
    )(iq_tab, ik_tab, qa, ka, vb, qm, mkb, mvb)


def _diag_blocks(x, nh, rows, width):
    head = _idiv(lax.broadcasted_iota(jnp.int32, (1, nh * width), 1), width)
    out = None
    for h in range(nh):
        part = jnp.where(head == h, x[h * rows:(h + 1) * rows, :], 0.0)
        out = part if out is None else out + part
    return out


def _sample_att_body(seq, past, per_step, *refs):
    for i in range(per_step):
        _sample_att_one(seq, past, *[r.at[i] for r in refs])


def _sample_att_one(seq, past, qa_ref, ka_ref, vb_ref, qm_ref, kt_ref, vt_ref, lt_ref, mkt_ref, mvt_ref, o_ref):
    blk = LANES
    nblk = past // blk
    lt = lt_ref[...]
    x = jnp.concatenate([lt[:, b * blk:(b + 1) * blk] for b in range(nblk)], axis=0)
    n = x.shape[0]
    parts = jnp.concatenate(_split3(x), axis=0).astype(BF16)
    src = lax.broadcasted_iota(jnp.int32, (blk, blk), 0)
    dst = lax.broadcasted_iota(jnp.int32, (blk, blk), 1)
    later = jnp.where(src > dst, 1.0, 0.0).astype(BF16)
    loc = jnp.dot(parts, later, preferred_element_type=F32)
    tot = jnp.dot(parts, jnp.ones((blk, blk), BF16), preferred_element_type=F32)
    loc = loc[0:n] + loc[n:2 * n] + loc[2 * n:3 * n]
    tot = tot[0:n] + tot[n:2 * n] + tot[2 * n:3 * n]
    running = jnp.zeros((FOX_HEADS, blk), F32)
    suffix = [None] * nblk
    for b in reversed(range(nblk)):
        rows = slice(b * FOX_HEADS, (b + 1) * FOX_HEADS)
        suffix[b] = loc[rows] + running
        running = running + tot[rows]
    rt = jnp.concatenate(suffix, axis=1) * LOG2E
    bias = jnp.concatenate([jnp.broadcast_to(rt[h:h + 1, :], (seq, past)) for h in range(FOX_HEADS)], axis=0)

    nrow = FOX_HEADS * seq
    lane = lax.broadcasted_iota(jnp.int32, (nrow, QK_COLS), 1)
    lane_head = jnp.where(lane < FOX_DIM, _idiv(lane, HEAD_DIM), _idiv(lane - FOX_DIM, AUG_GROUP))
    row_head = _idiv(lax.broadcasted_iota(jnp.int32, (nrow, QK_COLS), 0), seq)
    qbd = jnp.where(lane_head == row_head, jnp.concatenate([qa_ref[...]] * FOX_HEADS, axis=0), 0.0).astype(BF16)
    aug_row = _imod(lax.broadcasted_iota(jnp.int32, (AUG, past), 0), AUG_GROUP)
    kt = jnp.concatenate([kt_ref[...].astype(BF16), jnp.where(aug_row < N_SPLIT, 1.0, 0.0).astype(BF16)], axis=0)
    s_past = jnp.dot(qbd, kt, preferred_element_type=F32) + bias
    s_new = _dot_nt(qbd, ka_ref[...])
    qi = _imod(lax.broadcasted_iota(jnp.int32, (nrow, seq), 0), seq)
    kj = lax.broadcasted_iota(jnp.int32, (nrow, seq), 1)
    s_new = jnp.where(kj <= qi, s_new, NEG)
    m = jnp.maximum(jnp.max(s_past, axis=-1, keepdims=True), jnp.max(s_new, axis=-1, keepdims=True))
    p_past = jnp.exp2(s_past - m)
    p_new = jnp.exp2(s_new - m)
    lsum = jnp.sum(p_past, axis=-1, keepdims=True) + jnp.sum(p_new, axis=-1, keepdims=True)
    o = (_dot_nt(p_past.astype(BF16), vt_ref[...].astype(BF16))
         + jnp.dot(p_new.astype(BF16), vb_ref[...], preferred_element_type=F32)) / lsum
    o_ref[:, 0:FOX_DIM] = _diag_blocks(o, FOX_HEADS, seq, HEAD_DIM).astype(BF16)

    nrow_m = MEM_HEADS * seq
    lane_m = _idiv(lax.broadcasted_iota(jnp.int32, (nrow_m, MEM_DIM), 1), HEAD_DIM)
    row_m = _idiv(lax.broadcasted_iota(jnp.int32, (nrow_m, MEM_DIM), 0), seq)
    qbd_m = jnp.where(lane_m == row_m, jnp.concatenate([qm_ref[...]] * MEM_HEADS, axis=0), 0.0).astype(BF16)
    pm, lm = _softmax_rows(jnp.dot(qbd_m, mkt_ref[...].astype(BF16), preferred_element_type=F32) * SCALE)
    om = _dot_nt(pm.astype(BF16), mvt_ref[...].astype(BF16)) / lm
    o_ref[:, FOX_DIM:ATT_COLS] = _diag_blocks(om, MEM_HEADS, seq, HEAD_DIM).astype(BF16)


def _sample_attention(qa, ka, vb, qm, pk, pv, plf, mk, mv):
    nbatch, seq, _ = qa.shape
    past = pk.shape[2]
    per_step = SAMPLE_SEQS_PER_STEP
    assert nbatch % per_step == 0
    blk = lambda rows, cols: pl.BlockSpec((per_step, rows, cols), lambda b: (b, 0, 0))
    return pl.pallas_call(
        functools.partial(_sample_att_body, seq, past, per_step),
        grid=(nbatch // per_step,),
        in_specs=[blk(seq, QK_COLS), blk(seq, QK_COLS), blk(seq, FOX_DIM), blk(seq, MEM_DIM),
                  blk(FOX_DIM, past), blk(FOX_DIM, past), blk(FOX_HEADS, past),
                  blk(MEM_DIM, N_MEM), blk(MEM_DIM, N_MEM)],
        out_specs=blk(seq, ATT_COLS),
        out_shape=jax.ShapeDtypeStruct((nbatch, seq, ATT_COLS), BF16),
        compiler_params=pltpu.CompilerParams(
            dimension_semantics=("arbitrary",), vmem_limit_bytes=VMEM_LIMIT),
        name="sample_attention",
    )(qa, ka, vb, qm, pk, pv, plf, mk, mv)


def _prep_weights(norm_ffn1, norm_mix, w_in, b_forget, conv_w, conv_b,
                  q_norm_fox, k_norm_fox, q_norm_mem, k_norm_mem, norm_mem, w_mem_kv, w_out,
                  norm_ffn2, norm_final):
    row = lambda v: v.reshape(1, -1).astype(F32)
    return {
        "g1": row(norm_ffn1), "gmix": row(norm_mix), "win32t": w_in.T,
        "conv_w": conv_w.astype(F32), "conv_b": row(conv_b),
        "gq": row(jnp.tile(q_norm_fox, FOX_HEADS)) * (SCALE * LOG2E), "gk": row(jnp.tile(k_norm_fox, FOX_HEADS)),
        "gqm": row(jnp.tile(q_norm_mem, MEM_HEADS)),
        "bf_rep": row(jnp.repeat(b_forget, AUG_GROUP)),
        "bf": jnp.pad(row(b_forget), ((0, 0), (0, LANES - FOX_HEADS))),
        "gmem": row(norm_mem), "wmem": w_mem_kv.astype(BF16), "gkm": row(jnp.tile(k_norm_mem, MEM_HEADS)),
        "wo": w_out.astype(BF16), "g2": row(norm_ffn2), "gfin": row(norm_final),
    }


def kernel(x_prompt, x_sample, cache_fox_k, cache_fox_v, cache_fox_logf, state_conv, cache_mem_k, cache_mem_v, mem_prompt, norm_ffn1, w_ffn1_gate, w_ffn1_up, w_ffn1_down, norm_mix, w_in, b_forget, conv_w, conv_b, q_norm_fox, k_norm_fox, q_norm_mem, k_norm_mem, norm_mem, w_mem_kv, w_out, norm_ffn2, w_ffn2_gate, w_ffn2_up, w_ffn2_down, norm_final):
    depth = w_in.shape[0]
    assert depth == 1, "single-layer step"
    B, T, _ = x_prompt.shape
    nbs, seq_s, _ = x_sample.shape
    past = cache_fox_k.shape[2]
    l = 0
    wts = _prep_weights(norm_ffn1[l], norm_mix[l], w_in[l], b_forget[l], conv_w[l], conv_b[l], q_norm_fox[l],
                        k_norm_fox[l], q_norm_mem[l], k_norm_mem[l], norm_mem[l], w_mem_kv[l], w_out[l],
                        norm_ffn2[l], norm_final[l])
    tile = TOKEN_TILE
    n_s = nbs * seq_s

    x1s, wts["wg1"], wts["wu1"], wts["wd1"] = _ffn_stream(
        (x_sample.reshape(n_s, D_MODEL),), wts["g1"], w_ffn1_gate[l], w_ffn1_up[l], w_ffn1_down[l])
    ycs, qas, kas, vbs, kfs, vfs, lfs, qms, ncs, wts["win"], wts["wtail"] = _stage1(
        x1s.reshape(1, n_s, D_MODEL), state_conv[l], wts, nb=tile // seq_s, seq=seq_s, carry=False)
    per_seq = lambda a: a.reshape(nbs, seq_s, a.shape[-1])
    feat_major = lambda a: a.reshape(a.shape[0], a.shape[1], -1).transpose(0, 2, 1)
    yas = _sample_attention(
        per_seq(qas), per_seq(kas), per_seq(vbs), per_seq(qms),
        feat_major(cache_fox_k[l]), feat_major(cache_fox_v[l]), feat_major(cache_fox_logf[l]),
        feat_major(cache_mem_k[l]), feat_major(cache_mem_v[l]))
    y_sample, wts["wg2"], wts["wu2"], wts["wd2"] = _ffn_stream(
        (x1s, ycs.reshape(n_s, CONV_DIM), yas.reshape(n_s, ATT_COLS)), wts["g2"],
        w_ffn2_gate[l], w_ffn2_up[l], w_ffn2_down[l], wo=wts["wo"], gfin=wts["gfin"])

    mk_p, mv_p, mkb, mvb = _memkv(mem_prompt, wts)
    x1, yc, qa, ka, vb, kf, vf, lf, qm, nc = _stage1(x_prompt, None, wts, nb=1, seq=tile, carry=True)
    ya = _prompt_attention(qa, ka, vb, qm, mkb, mvb, tq=ATT_QUERY_TILE, tk=ATT_KEY_TILE)
    y_prompt = _stage3(x1, yc, ya, wts, tm=tile)

    heads = lambda a, b, t, nh: a.reshape(1, b, t, nh, HEAD_DIM)
    token_major = lambda a, nh: a.reshape(a.shape[0], nh, -1, a.shape[2]).transpose(0, 3, 1, 2)[None]
    return (y_prompt, y_sample.reshape(nbs, seq_s, D_MODEL),
            token_major(kf, FOX_HEADS), token_major(vf, FOX_HEADS), lf.transpose(0, 2, 1)[None],
            nc.reshape(1, B, CONV_K - 1, CONV_DIM),
            token_major(mk_p, MEM_HEADS), token_major(mv_p, MEM_HEADS),
            heads(kfs, nbs, seq_s, FOX_HEADS), heads(vfs, nbs, seq_s, FOX_HEADS),
            lfs.reshape(1, nbs, seq_s, FOX_HEADS), ncs.reshape(1, nbs, CONV_K - 1, CONV_DIM))
```

```python
import functools

import jax
import jax.numpy as jnp
from jax import lax
from jax.experimental import pallas as pl
from jax.experimental.pallas import tpu as pltpu

F32 = jnp.float32
BF16 = jnp.bfloat16

D_MODEL = 1024
HEAD_DIM = 64
CONV_DIM = 256
CONV_K = 3
FOX_HEADS = 8
FOX_DIM = FOX_HEADS * HEAD_DIM
MEM_HEADS = 4
MEM_DIM = MEM_HEADS * HEAD_DIM
N_MEM = 256
D_FF = 2816
EPS = 1e-6

LANES = 128
PAIR = 2 * HEAD_DIM
FF_CHUNK = 256
N_FF_CHUNKS = D_FF // FF_CHUNK
AUG = LANES
AUG_GROUP = AUG // FOX_HEADS
N_SPLIT = 3
QK_COLS = FOX_DIM + AUG
ATT_COLS = FOX_DIM + MEM_DIM
COL_Q = 3 * CONV_DIM
COL_K = COL_Q + FOX_DIM
COL_V = COL_K + FOX_DIM
COL_F = COL_V + FOX_DIM
COL_QM = COL_F + FOX_HEADS
TAIL_COLS = MEM_DIM + AUG + LANES
NEG = -1e30
SCALE = HEAD_DIM ** -0.5
LOG2E = 1.4426950408889634
VMEM_LIMIT = 56 * 1024 * 1024
TOKEN_TILE = 512
ATT_KEY_TILE = 512
ATT_QUERY_TILE = 1024
SAMPLE_SEQS_PER_STEP = 4


def _idiv(x, d):
    if d & (d - 1) == 0:
        return lax.shift_right_logical(x, d.bit_length() - 1)
    return x // d


def _imod(x, d):
    if d & (d - 1) == 0:
        return x & (d - 1)
    return x % d


def _rms(x, g):
    ms = jnp.mean(x * x, axis=-1, keepdims=True)
    return x * lax.rsqrt(ms + EPS) * g


def _pair_headnorm(x, g):
    lo = lax.broadcasted_iota(jnp.int32, (1, PAIR), 1) < HEAD_DIM
    x2 = x * x
    s_lo = jnp.sum(jnp.where(lo, x2, 0.0), axis=-1, keepdims=True)
    s_hi = jnp.sum(jnp.where(lo, 0.0, x2), axis=-1, keepdims=True)
    r = jnp.where(lo, lax.rsqrt(s_lo * (1.0 / HEAD_DIM) + EPS),
                  lax.rsqrt(s_hi * (1.0 / HEAD_DIM) + EPS))
    return x * r * g


def _store_cols(ref, sl, x, transposed):
    if transposed:
        ref[sl, :] = x.T
    else:
        ref[:, sl] = x


def _split3(x):
    hi = x.astype(BF16).astype(F32)
    r = x - hi
    mid = r.astype(BF16).astype(F32)
    lo = (r - mid).astype(BF16).astype(F32)
    return hi, mid, lo


def _ffn(xn, wg_ref, wu_ref, wd_ref):
    acc = None
    for c in range(N_FF_CHUNKS):
        sl = slice(c * FF_CHUNK, (c + 1) * FF_CHUNK)
        g = jnp.dot(xn, wg_ref[:, sl], preferred_element_type=F32)
        u = jnp.dot(xn, wu_ref[:, sl], preferred_element_type=F32)
        a = (g * jax.nn.sigmoid(g) * u).astype(BF16)
        d = jnp.dot(a, wd_ref[sl, :], preferred_element_type=F32)
        acc = d if acc is None else acc + d
    return acc


def _log_sigmoid(x):
    return jnp.minimum(x, 0.0) - jnp.log1p(jnp.exp(-jnp.abs(x)))


def _softmax_rows(s):
    m = jnp.max(s, axis=-1, keepdims=True)
    p = jnp.exp(s - m)
    return p, jnp.sum(p, axis=-1, keepdims=True)


def _dot_nt(a, b):
    return lax.dot_general(a, b, (((1,), (1,)), ((), ())), preferred_element_type=F32)


_STAGE1_SHARED_IN = ("gmix", "conv_w", "conv_b", "gq", "gk", "gqm", "bf_rep", "bf")
_STAGE1_SHARED_OUT = ("yc", "qa", "ka", "vb", "kf", "vf", "lf", "qm", "nc")
_STAGE1_IN = {True: ("x", "g1", "wg1", "wu1", "wd1", "win", "wtail") + _STAGE1_SHARED_IN,
              False: ("x1", "state", "win32t") + _STAGE1_SHARED_IN}
_STAGE1_OUT = {True: ("x1",) + _STAGE1_SHARED_OUT, False: _STAGE1_SHARED_OUT + ("win", "wtail")}


def _stage1_body(nb, seq, carry, *refs):
    tm = nb * seq
    names = _STAGE1_IN[carry] + _STAGE1_OUT[carry]
    r = dict(zip(names, refs[:len(names)]))
    cs_ref, cum_ref = refs[len(names):]
    gmix_ref, cw_ref, cb_ref, gq_ref, gk_ref, gqm_ref, bfr_ref, bf_ref = (r[n] for n in _STAGE1_SHARED_IN)
    yc_ref, qa_ref, ka_ref, vb_ref, kf_ref, vf_ref, lf_ref, qm_ref, nc_ref = (r[n] for n in _STAGE1_SHARED_OUT)
    win_ref, wtail_ref = r["win"], r["wtail"]
    if carry:
        x = r["x"][...]
        y = _ffn(_rms(x, r["g1"][...]).astype(BF16), r["wg1"], r["wu1"], r["wd1"])
        x1 = x + 0.5 * y
        r["x1"][...] = x1
    else:
        x1 = r["x1"][...]
        st_ref = r["state"]

        @pl.when(pl.program_id(1) == 0)
        def _():
            w_t = r["win32t"]
            win_ref[...] = w_t[0:COL_F, :].T.astype(BF16)
            f_rows = w_t[COL_F:COL_QM, :]
            f_rep = jnp.concatenate([jnp.broadcast_to(f_rows[hd:hd + 1, :], (AUG_GROUP, D_MODEL))
                                     for hd in range(FOX_HEADS)], axis=0)
            tail_t = jnp.concatenate([w_t[COL_QM:COL_QM + MEM_DIM, :], f_rep, f_rows,
                                      jnp.zeros((LANES - FOX_HEADS, D_MODEL), F32)], axis=0)
            wtail_ref[...] = tail_t.T.astype(BF16)
    h = _rms(x1, gmix_ref[...]).astype(BF16)

    ucb = jnp.dot(h, win_ref[:, 0:COL_Q], preferred_element_type=F32)
    ci = ucb[:, CONV_DIM:2 * CONV_DIM] * ucb[:, 0:CONV_DIM]
    if carry:
        @pl.when(pl.program_id(1) == 0)
        def _():
            cs_ref[:, 0:8, :] = jnp.zeros((nb, 8, CONV_DIM), F32)
    else:
        cs_ref[:, 8 - (CONV_K - 1):8, :] = st_ref[...]
    cs_ref[:, 8:8 + seq, :] = ci.reshape(nb, seq, CONV_DIM)
    conv = cb_ref[...] + cw_ref[CONV_K - 1:CONV_K, :] * ci
    for i in range(CONV_K - 1):
        shifted = cs_ref[:, 8 - (CONV_K - 1) + i:8 - (CONV_K - 1) + i + seq, :]
        conv = conv + cw_ref[i:i + 1, :] * shifted.reshape(tm, CONV_DIM)
    yc_ref[...] = (ucb[:, 2 * CONV_DIM:3 * CONV_DIM] * conv).astype(BF16)
    tail = cs_ref[:, seq:seq + 8, :]
    nc_ref[...] = tail[:, 8 - (CONV_K - 1):, :].reshape(nc_ref.shape)
    if carry:
        cs_ref[:, 0:8, :] = tail

    for name, col, g_ref, out_ref in (("q", COL_Q, gq_ref, qa_ref), ("k", COL_K, gk_ref, ka_ref)):
        pr = jnp.dot(h, win_ref[:, col:col + FOX_DIM], preferred_element_type=F32)
        for g in range(FOX_HEADS // 2):
            sl = slice(g * PAIR, (g + 1) * PAIR)
            xn = _pair_headnorm(pr[:, sl], g_ref[:, sl])
            if name == "k":
                _store_cols(kf_ref, sl, xn, carry)
            out_ref[:, sl] = xn.astype(BF16)
    v = jnp.dot(h, win_ref[:, COL_V:COL_V + FOX_DIM], preferred_element_type=F32)
    for g in range(FOX_HEADS // 2):
        sl = slice(g * PAIR, (g + 1) * PAIR)
        _store_cols(vf_ref, sl, v[:, sl], carry)
    vb_ref[...] = v.astype(BF16)

    tail_cols = jnp.dot(h, wtail_ref[...], preferred_element_type=F32)
    for g in range(MEM_HEADS // 2):
        sl = slice(g * PAIR, (g + 1) * PAIR)
        qm_ref[:, sl] = _pair_headnorm(tail_cols[:, sl], gqm_ref[:, sl]).astype(BF16)
    fl = tail_cols[:, MEM_DIM:]
    lf = _log_sigmoid(fl[:, AUG:2 * AUG] + bf_ref[...])
    if carry:
        lf_ref[...] = lf.T[0:FOX_HEADS, :]
    else:
        lf_ref[...] = lf[:, 0:FOX_HEADS]
    logf = _log_sigmoid(fl[:, 0:AUG] + bfr_ref[...])

    row = lax.broadcasted_iota(jnp.int32, (tm, tm), 0)
    col = lax.broadcasted_iota(jnp.int32, (tm, tm), 1)
    tri = col <= row
    if nb > 1:
        tri = jnp.logical_and(tri, _idiv(row, seq) == _idiv(col, seq))
    tri = jnp.where(tri, 1.0, 0.0).astype(BF16)
    hi, mid, lo = _split3(logf)
    parts = jnp.concatenate([hi.astype(BF16), mid.astype(BF16), lo.astype(BF16)], axis=1)
    cs = jnp.dot(tri, parts, preferred_element_type=F32)
    cum = cs[:, 0:AUG] + cs[:, AUG:2 * AUG] + cs[:, 2 * AUG:3 * AUG]
    if carry:
        @pl.when(pl.program_id(1) == 0)
        def _():
            cum_ref[...] = jnp.zeros(cum_ref.shape, F32)
        cum = cum + cum_ref[0:1, :]
        cum_ref[0:1, :] = cum[tm - 1:tm, :]

    j = _imod(lax.broadcasted_iota(jnp.int32, (1, AUG), 1), AUG_GROUP)
    terms = _split3(cum * LOG2E)
    aq = jnp.where(j < 2 * N_SPLIT, 1.0, 0.0)
    ak = jnp.where(j < N_SPLIT, 1.0, 0.0)
    for i, term in enumerate(terms):
        aq = jnp.where(j == i, term, aq)
        ak = jnp.where(j == N_SPLIT + i, -term, ak)
    qa_ref[:, FOX_DIM:QK_COLS] = aq.astype(BF16)
    ka_ref[:, FOX_DIM:QK_COLS] = ak.astype(BF16)


def _stage1(x3, state, wts, *, nb, seq, carry):
    G, T, _ = x3.shape
    tm = nb * seq
    nt = T // tm
    const = lambda shape: pl.BlockSpec(shape, lambda g, t: (0,) * len(shape), pipeline_mode=pl.Buffered(1))
    tok = lambda cols: pl.BlockSpec((None, tm, cols), lambda g, t: (g, t, 0))
    if carry:
        feat = lambda rows: pl.BlockSpec((None, rows, tm), lambda g, t: (g, 0, t))
        feat_shape = lambda rows: jax.ShapeDtypeStruct((G, rows, T), F32)
    else:
        feat = lambda rows: pl.BlockSpec((None, tm, rows), lambda g, t: (g, t, 0))
        feat_shape = lambda rows: jax.ShapeDtypeStruct((G, T, rows), F32)
    shp = lambda cols, dt: jax.ShapeDtypeStruct((G, T, cols), dt)
    if carry:
        nc_spec = pl.BlockSpec((None, CONV_K - 1, CONV_DIM), lambda g, t: (g, 0, 0))
        nc_shape = jax.ShapeDtypeStruct((G, CONV_K - 1, CONV_DIM), F32)
    else:
        nc_spec = pl.BlockSpec((nb, CONV_K - 1, CONV_DIM), lambda g, t: (t, 0, 0))
        nc_shape = jax.ShapeDtypeStruct((nt * nb, CONV_K - 1, CONV_DIM), F32)
    tiled_in = {"x": (tok(D_MODEL), x3), "x1": (tok(D_MODEL), x3), "state": (nc_spec, state)}
    in_specs, args = [], []
    for n in _STAGE1_IN[carry]:
        spec, arg = tiled_in[n] if n in tiled_in else (const(wts[n].shape), wts[n])
        in_specs.append(spec)
        args.append(arg)
    outs = {"x1": (tok(D_MODEL), shp(D_MODEL, F32)), "yc": (tok(CONV_DIM), shp(CONV_DIM, BF16)),
            "qa": (tok(QK_COLS), shp(QK_COLS, BF16)), "ka": (tok(QK_COLS), shp(QK_COLS, BF16)),
            "vb": (tok(FOX_DIM), shp(FOX_DIM, BF16)), "kf": (feat(FOX_DIM), feat_shape(FOX_DIM)),
            "vf": (feat(FOX_DIM), feat_shape(FOX_DIM)), "lf": (feat(FOX_HEADS), feat_shape(FOX_HEADS)),
            "qm": (tok(MEM_DIM), shp(MEM_DIM, BF16)), "nc": (nc_spec, nc_shape)}
    if not carry:
        for n, w_shape in (("win", (D_MODEL, COL_F)), ("wtail", (D_MODEL, TAIL_COLS))):
            outs[n] = (pl.BlockSpec(w_shape, lambda g, t: (0, 0)), jax.ShapeDtypeStruct(w_shape, BF16))
    out_specs = [outs[n][0] for n in _STAGE1_OUT[carry]]
    out_shape = [outs[n][1] for n in _STAGE1_OUT[carry]]
    return pl.pallas_call(
        functools.partial(_stage1_body, nb, seq, carry),
        grid=(G, nt),
        in_specs=in_specs,
        out_specs=out_specs,
        out_shape=out_shape,
        scratch_shapes=[pltpu.VMEM((nb, seq + 8, CONV_DIM), F32), pltpu.VMEM((8, AUG), F32)],
        compiler_params=pltpu.CompilerParams(
            dimension_semantics=("arbitrary", "arbitrary"), vmem_limit_bytes=VMEM_LIMIT),
        name="stage1_carry" if carry else "stage1_batched",
    )(*args)


def _stage3_body(x1_ref, yc_ref, ya_ref, wo_ref, g2_ref, wg_ref, wu_ref, wd_ref, gfin_ref, y_ref):
    x2 = (x1_ref[...]
          + jnp.dot(yc_ref[...], wo_ref[0:CONV_DIM, :], preferred_element_type=F32)
          + jnp.dot(ya_ref[...], wo_ref[CONV_DIM:, :], preferred_element_type=F32))
    y = _ffn(_rms(x2, g2_ref[...]).astype(BF16), wg_ref, wu_ref, wd_ref)
    y_ref[...] = _rms(x2 + 0.5 * y, gfin_ref[...])


def _stage3(x1, yc, ya, wts, *, tm):
    G, T, _ = x1.shape
    const = lambda shape: pl.BlockSpec(shape, lambda g, t: (0,) * len(shape), pipeline_mode=pl.Buffered(1))
    tok = lambda cols: pl.BlockSpec((None, tm, cols), lambda g, t: (g, t, 0))
    names = ("wo", "g2", "wg2", "wu2", "wd2", "gfin")
    return pl.pallas_call(
        _stage3_body,
        grid=(G, T // tm),
        in_specs=[tok(D_MODEL), tok(CONV_DIM), tok(ATT_COLS)] + [const(wts[n].shape) for n in names],
        out_specs=tok(D_MODEL),
        out_shape=jax.ShapeDtypeStruct((G, T, D_MODEL), F32),
        compiler_params=pltpu.CompilerParams(
            dimension_semantics=("arbitrary", "arbitrary"), vmem_limit_bytes=VMEM_LIMIT),
        name="stage3",
    )(x1, yc, ya, *[wts[n] for n in names])


def _ffn_stream_body(with_out_proj, with_final_norm, *refs):
    refs = list(refs)
    n_x = 4 if with_out_proj else 1
    x_refs, refs = refs[:n_x], refs[n_x:]
    g_ref, wg_ref, wu_ref, wd_ref = refs[:4]
    refs = refs[4:]
    gfin_ref = refs.pop(0) if with_final_norm else None
    y_ref, wgb_ref, wub_ref, wdb_ref, x_ref, xn_ref, acc_ref = refs
    c = pl.program_id(0)

    @pl.when(c == 0)
    def _():
        if with_out_proj:
            x1_ref, yc_ref, ya_ref, wo_ref = x_refs
            x = (x1_ref[...]
                 + jnp.dot(yc_ref[...], wo_ref[0:CONV_DIM, :], preferred_element_type=F32)
                 + jnp.dot(ya_ref[...], wo_ref[CONV_DIM:, :], preferred_element_type=F32))
        else:
            x = x_refs[0][...]
        x_ref[...] = x
        xn_ref[...] = _rms(x, g_ref[...]).astype(BF16)
        acc_ref[...] = jnp.zeros(acc_ref.shape, F32)

    wg = wg_ref[...].astype(BF16)
    wu = wu_ref[...].astype(BF16)
    wd = wd_ref[...].astype(BF16)
    wgb_ref[...] = wg
    wub_ref[...] = wu
    wdb_ref[...] = wd
    xn = xn_ref[...]
    g = jnp.dot(xn, wg, preferred_element_type=F32)
    u = jnp.dot(xn, wu, preferred_element_type=F32)
    a = (g * jax.nn.sigmoid(g) * u).astype(BF16)
    acc_ref[...] += jnp.dot(a, wd, preferred_element_type=F32)

    @pl.when(c == pl.num_programs(0) - 1)
    def _():
        y = x_ref[...] + 0.5 * acc_ref[...]
        y_ref[...] = _rms(y, gfin_ref[...]) if with_final_norm else y


def _ffn_stream(x_parts, gain, wg, wu, wd, *, wo=None, gfin=None):
    m = x_parts[0].shape[0]
    const = lambda a: pl.BlockSpec(a.shape, lambda c: (0,) * a.ndim, pipeline_mode=pl.Buffered(1))
    col_chunk = pl.BlockSpec((D_MODEL, FF_CHUNK), lambda c: (0, c))
    row_chunk = pl.BlockSpec((FF_CHUNK, D_MODEL), lambda c: (c, 0))
    args = list(x_parts) + ([wo] if wo is not None else []) + [gain]
    in_specs = [const(a) for a in args] + [col_chunk, col_chunk, row_chunk]
    args += [wg, wu, wd]
    if gfin is not None:
        in_specs.append(const(gfin))
        args.append(gfin)
    return pl.pallas_call(
        functools.partial(_ffn_stream_body, wo is not None, gfin is not None),
        grid=(N_FF_CHUNKS,),
        in_specs=in_specs,
        out_specs=[pl.BlockSpec((m, D_MODEL), lambda c: (0, 0)), col_chunk, col_chunk, row_chunk],
        out_shape=[jax.ShapeDtypeStruct((m, D_MODEL), F32), jax.ShapeDtypeStruct(wg.shape, BF16),
                   jax.ShapeDtypeStruct(wu.shape, BF16), jax.ShapeDtypeStruct(wd.shape, BF16)],
        scratch_shapes=[pltpu.VMEM((m, D_MODEL), F32), pltpu.VMEM((m, D_MODEL), BF16),
                        pltpu.VMEM((m, D_MODEL), F32)],
        compiler_params=pltpu.CompilerParams(dimension_semantics=("arbitrary",), vmem_limit_bytes=VMEM_LIMIT),
        name="ffn_stream_out" if wo is not None else "ffn_stream_in",
    )(*args)


def _memkv_body(mem_ref, gmem_ref, w_ref, gk_ref, mk_ref, mv_ref, mkb_ref, mvb_ref):
    kv = jnp.dot(_rms(mem_ref[...], gmem_ref[...]).astype(BF16), w_ref[...], preferred_element_type=F32)
    for g in range(MEM_HEADS // 2):
        sl = slice(g * PAIR, (g + 1) * PAIR)
        mk = _pair_headnorm(kv[:, sl], gk_ref[:, sl])
        mk_ref[sl, :] = mk.T
        mkb_ref[:, sl] = mk.astype(BF16)
    mv = kv[:, MEM_DIM:]
    mv_ref[...] = mv.T
    mvb_ref[...] = mv.astype(BF16)


def _memkv(mem, wts):
    B = mem.shape[0]
    const = lambda shape: pl.BlockSpec(shape, lambda b: (0,) * len(shape))
    blk = lambda cols: pl.BlockSpec((None, N_MEM, cols), lambda b: (b, 0, 0))
    shp = lambda dt: jax.ShapeDtypeStruct((B, N_MEM, MEM_DIM), dt)
    names = ("gmem", "wmem", "gkm")
    return pl.pallas_call(
        _memkv_body,
        grid=(B,),
        in_specs=[blk(D_MODEL)] + [const(wts[n].shape) for n in names],
        out_specs=[blk(MEM_DIM)] * 4,
        out_shape=[shp(F32), shp(F32), shp(BF16), shp(BF16)],
        compiler_params=pltpu.CompilerParams(dimension_semantics=("arbitrary",)),
        name="memkv",
    )(mem, *[wts[n] for n in names])


def _head_masks():
    lane = lax.broadcasted_iota(jnp.int32, (1, LANES), 1)
    lo = lane < HEAD_DIM
    return lo, _idiv(lane, AUG_GROUP)


def _prompt_att_body(tq, tk, iq_ref, ik_ref, qa_ref, ka_ref, vb_ref, qm_ref, mk_ref, mv_ref,
                     o_ref, m_ref, acc_ref):
    p = pl.program_id(1)
    iq = iq_ref[p]
    ik = ik_ref[p]
    lo, aug_head = _head_masks()

    @pl.when(iq == 0)
    def _():
        m_ref[...] = jnp.full(m_ref.shape, NEG, F32)
        acc_ref[...] = jnp.zeros(acc_ref.shape, F32)

    def step(masked, r0, sub, fresh=False):
        rows = tq - r0
        keys = slice(sub * tk, (sub + 1) * tk)
        if masked:
            visible = (lax.broadcasted_iota(jnp.int32, (tk, tk), 1)
                       <= lax.broadcasted_iota(jnp.int32, (tk, tk), 0))
        q_aug = qa_ref[r0:, FOX_DIM:QK_COLS]
        k_aug = ka_ref[keys, FOX_DIM:QK_COLS]
        ones = jnp.ones((tk, LANES), BF16)
        for h in range(FOX_HEADS):
            sl = slice((h // 2) * PAIR, (h // 2 + 1) * PAIR)
            mine = lo if h % 2 == 0 else jnp.logical_not(lo)
            qh = jnp.concatenate([jnp.where(mine, qa_ref[r0:, sl], 0.0).astype(BF16),
                                  jnp.where(aug_head == h, q_aug, 0.0).astype(BF16)], axis=1)
            kh = jnp.concatenate([ka_ref[keys, sl], k_aug], axis=1)
            s = _dot_nt(qh, kh)
            if masked:
                top = jnp.where(visible, s[:tk], NEG)
                s = top if rows == tk else jnp.concatenate([top, s[tk:]], axis=0)
            blocks = [s[:, c * LANES:(c + 1) * LANES] for c in range(tk // LANES)]
            blk_max = functools.reduce(jnp.maximum, blocks)
            row_max = jnp.max(blk_max, axis=-1, keepdims=True)
            if fresh:
                m_new = jnp.broadcast_to(row_max, (rows, LANES))
            else:
                m_prev = m_ref[h, r0:, :]
                m_new = jnp.maximum(m_prev, row_max)
                alpha = jnp.exp2(m_prev - m_new)
            pexp = jnp.concatenate([jnp.exp2(b - m_new) for b in blocks], axis=1).astype(BF16)
            pv = jnp.dot(pexp, jnp.concatenate([vb_ref[keys, sl], ones], axis=1), preferred_element_type=F32)
            if fresh:
                acc_ref[h, r0:, :] = pv
            else:
                acc_ref[h, r0:, :] = jnp.concatenate([alpha, alpha], axis=1) * acc_ref[h, r0:, :] + pv
            m_ref[h, r0:, :] = m_new

    @pl.when(jnp.logical_and(ik == 0, iq > 0))
    def _():
        for sub in range(tq // tk):
            step(False, 0, sub, fresh=(sub == 0))

    @pl.when(jnp.logical_and(ik > 0, ik < iq))
    def _():
        for sub in range(tq // tk):
            step(False, 0, sub)

    @pl.when(ik == iq)
    def _():
        for sub in range(tq // tk):
            step(True, sub * tk, sub)
        for g in range(FOX_HEADS // 2):
            even = acc_ref[2 * g, :, 0:PAIR] / acc_ref[2 * g, :, PAIR:2 * PAIR]
            odd = acc_ref[2 * g + 1, :, 0:PAIR] / acc_ref[2 * g + 1, :, PAIR:2 * PAIR]
            o_ref[:, g * PAIR:(g + 1) * PAIR] = jnp.where(lo, even, odd).astype(BF16)
        for g in range(MEM_HEADS // 2):
            sl = slice(g * PAIR, (g + 1) * PAIR)
            outs = []
            for mine in (lo, jnp.logical_not(lo)):
                qh = jnp.where(mine, qm_ref[:, sl], 0.0).astype(BF16)
                pexp, lsum = _softmax_rows(_dot_nt(qh, mk_ref[:, sl]) * SCALE)
                outs.append(jnp.dot(pexp.astype(BF16), mv_ref[:, sl], preferred_element_type=F32) / lsum)
            o_ref[:, FOX_DIM + g * PAIR:FOX_DIM + (g + 1) * PAIR] = jnp.where(lo, outs[0], outs[1]).astype(BF16)


def _prompt_attention(qa, ka, vb, qm, mkb, mvb, *, tq, tk):
    B, T, _ = qa.shape
    assert tq % tk == 0 and T % tq == 0
    pairs = [(i, j) for i in range(T // tq) for j in range(i + 1)]
    iq_tab = jnp.asarray([p[0] for p in pairs], jnp.int32)
    ik_tab = jnp.asarray([p[1] for p in pairs], jnp.int32)
    grid_spec = pltpu.PrefetchScalarGridSpec(
        num_scalar_prefetch=2,
        grid=(B, len(pairs)),
        in_specs=[
            pl.BlockSpec((None, tq, QK_COLS), lambda b, p, iq, ik: (b, iq[p], 0)),
            pl.BlockSpec((None, tq, QK_COLS), lambda b, p, iq, ik: (b, ik[p], 0)),
            pl.BlockSpec((None, tq, FOX_DIM), lambda b, p, iq, ik: (b, ik[p], 0)),
            pl.BlockSpec((None, tq, MEM_DIM), lambda b, p, iq, ik: (b, iq[p], 0)),
            pl.BlockSpec((None, N_MEM, MEM_DIM), lambda b, p, iq, ik: (b, 0, 0)),
            pl.BlockSpec((None, N_MEM, MEM_DIM), lambda b, p, iq, ik: (b, 0, 0)),
        ],
        out_specs=pl.BlockSpec((None, tq, ATT_COLS), lambda b, p, iq, ik: (b, iq[p], 0)),
        scratch_shapes=[pltpu.VMEM((FOX_HEADS, tq, LANES), F32), pltpu.VMEM((FOX_HEADS, tq, 2 * PAIR), F32)],
    )
    return pl.pallas_call(
        functools.partial(_prompt_att_body, tq, tk),
        grid_spec=grid_spec,
        out_shape=jax.ShapeDtypeStruct((B, T, ATT_COLS), BF16),
        compiler_params=pltpu.CompilerParams(
            dimension_semantics=("arbitrary", "arbitrary"), vmem_limit_bytes=VMEM_LIMIT),
        name="prompt_attention",
    )(iq_tab, ik_tab, qa, ka, vb, qm, mkb, mvb)


def _diag_blocks(x, nh, rows, width):
    head = _idiv(lax.broadcasted_iota(jnp.int32, (1, nh * width), 1), width)
    out = None
    for h in range(nh):
        part = jnp.where(head == h, x[h * rows:(h + 1) * rows, :], 0.0)
        out = part if out is None else out + part
    return out


def _sample_att_body(seq, past, per_step, *refs):
    for i in range(per_step):
        _sample_att_one(seq, past, *[r.at[i] for r in refs])


def _sample_att_one(seq, past, qa_ref, ka_ref, vb_ref, qm_ref, kt_ref, vt_ref, lt_ref, mkt_ref, mvt_ref, o_ref):
    blk = LANES
    nblk = past // blk
    lt = lt_ref[...]
    x = jnp.concatenate([lt[:, b * blk:(b + 1) * blk] for b in range(nblk)], axis=0)
    n = x.shape[0]
    parts = jnp.concatenate(_split3(x), axis=0).astype(BF16)
    src = lax.broadcasted_iota(jnp.int32, (blk, blk), 0)
    dst = lax.broadcasted_iota(jnp.int32, (blk, blk), 1)
    later = jnp.where(src > dst, 1.0, 0.0).astype(BF16)
    loc = jnp.dot(parts, later, preferred_element_type=F32)
    tot = jnp.dot(parts, jnp.ones((blk, blk), BF16), preferred_element_type=F32)
    loc = loc[0:n] + loc[n:2 * n] + loc[2 * n:3 * n]
    tot = tot[0:n] + tot[n:2 * n] + tot[2 * n:3 * n]
    running = jnp.zeros((FOX_HEADS, blk), F32)
    suffix = [None] * nblk
    for b in reversed(range(nblk)):
        rows = slice(b * FOX_HEADS, (b + 1) * FOX_HEADS)
        suffix[b] = loc[rows] + running
        running = running + tot[rows]
    rt = jnp.concatenate(suffix, axis=1) * LOG2E
    bias = jnp.concatenate([jnp.broadcast_to(rt[h:h + 1, :], (seq, past)) for h in range(FOX_HEADS)], axis=0)

    nrow = FOX_HEADS * seq
    lane = lax.broadcasted_iota(jnp.int32, (nrow, QK_COLS), 1)
    lane_head = jnp.where(lane < FOX_DIM, _idiv(lane, HEAD_DIM), _idiv(lane - FOX_DIM, AUG_GROUP))
    row_head = _idiv(lax.broadcasted_iota(jnp.int32, (nrow, QK_COLS), 0), seq)
    qbd = jnp.where(lane_head == row_head, jnp.concatenate([qa_ref[...]] * FOX_HEADS, axis=0), 0.0).astype(BF16)
    aug_row = _imod(lax.broadcasted_iota(jnp.int32, (AUG, past), 0), AUG_GROUP)
    kt = jnp.concatenate([kt_ref[...].astype(BF16), jnp.where(aug_row < N_SPLIT, 1.0, 0.0).astype(BF16)], axis=0)
    s_past = jnp.dot(qbd, kt, preferred_element_type=F32) + bias
    s_new = _dot_nt(qbd, ka_ref[...])
    qi = _imod(lax.broadcasted_iota(jnp.int32, (nrow, seq), 0), seq)
    kj = lax.broadcasted_iota(jnp.int32, (nrow, seq), 1)
    s_new = jnp.where(kj <= qi, s_new, NEG)
    m = jnp.maximum(jnp.max(s_past, axis=-1, keepdims=True), jnp.max(s_new, axis=-1, keepdims=True))
    p_past = jnp.exp2(s_past - m)
    p_new = jnp.exp2(s_new - m)
    lsum = jnp.sum(p_past, axis=-1, keepdims=True) + jnp.sum(p_new, axis=-1, keepdims=True)
    o = (_dot_nt(p_past.astype(BF16), vt_ref[...].astype(BF16))
         + jnp.dot(p_new.astype(BF16), vb_ref[...], preferred_element_type=F32)) / lsum
    o_ref[:, 0:FOX_DIM] = _diag_blocks(o, FOX_HEADS, seq, HEAD_DIM).astype(BF16)

    nrow_m = MEM_HEADS * seq
    lane_m = _idiv(lax.broadcasted_iota(jnp.int32, (nrow_m, MEM_DIM), 1), HEAD_DIM)
    row_m = _idiv(lax.broadcasted_iota(jnp.int32, (nrow_m, MEM_DIM), 0), seq)
    qbd_m = jnp.where(lane_m == row_m, jnp.concatenate([qm_ref[...]] * MEM_HEADS, axis=0), 0.0).astype(BF16)
    pm, lm = _softmax_rows(jnp.dot(qbd_m, mkt_ref[...].astype(BF16), preferred_element_type=F32) * SCALE)
    om = _dot_nt(pm.astype(BF16), mvt_ref[...].astype(BF16)) / lm
    o_ref[:, FOX_DIM:ATT_COLS] = _diag_blocks(om, MEM_HEADS, seq, HEAD_DIM).astype(BF16)


def _sample_attention(qa, ka, vb, qm, pk, pv, plf, mk, mv):
    nbatch, seq, _ = qa.shape
    past = pk.shape[2]
    per_step = SAMPLE_SEQS_PER_STEP
    assert nbatch % per_step == 0
    blk = lambda rows, cols: pl.BlockSpec((per_step, rows, cols), lambda b: (b, 0, 0))
    return pl.pallas_call(
        functools.partial(_sample_att_body, seq, past, per_step),
        grid=(nbatch // per_step,),
        in_specs=[blk(seq, QK_COLS), blk(seq, QK_COLS), blk(seq, FOX_DIM), blk(seq, MEM_DIM),
                  blk(FOX_DIM, past), blk(FOX_DIM, past), blk(FOX_HEADS, past),
                  blk(MEM_DIM, N_MEM), blk(MEM_DIM, N_MEM)],
        out_specs=blk(seq, ATT_COLS),
        out_shape=jax.ShapeDtypeStruct((nbatch, seq, ATT_COLS), BF16),
        compiler_params=pltpu.CompilerParams(
            dimension_semantics=("arbitrary",), vmem_limit_bytes=VMEM_LIMIT),
        name="sample_attention",
    )(qa, ka, vb, qm, pk, pv, plf, mk, mv)


def _prep_weights(norm_ffn1, norm_mix, w_in, b_forget, conv_w, conv_b,
                  q_norm_fox, k_norm_fox, q_norm_mem, k_norm_mem, norm_mem, w_mem_kv, w_out,
                  norm_ffn2, norm_final):
    row = lambda v: v.reshape(1, -1).astype(F32)
    return {
        "g1": row(norm_ffn1), "gmix": row(norm_mix), "win32t": w_in.T,
        "conv_w": conv_w.astype(F32), "conv_b": row(conv_b),
        "gq": row(jnp.tile(q_norm_fox, FOX_HEADS)) * (SCALE * LOG2E), "gk": row(jnp.tile(k_norm_fox, FOX_HEADS)),
        "gqm": row(jnp.tile(q_norm_mem, MEM_HEADS)),
        "bf_rep": row(jnp.repeat(b_forget, AUG_GROUP)),
        "bf": jnp.pad(row(b_forget), ((0, 0), (0, LANES - FOX_HEADS))),
        "gmem": row(norm_mem), "wmem": w_mem_kv.astype(BF16), "gkm": row(jnp.tile(k_norm_mem, MEM_HEADS)),
        "wo": w_out.astype(BF16), "g2": row(norm_ffn2), "gfin": row(norm_final),
    }


def kernel(x_prompt, x_sample, cache_fox_k, cache_fox_v, cache_fox_logf, state_conv, cache_mem_k, cache_mem_v, mem_prompt, norm_ffn1, w_ffn1_gate, w_ffn1_up, w_ffn1_down, norm_mix, w_in, b_forget, conv_w, conv_b, q_norm_fox, k_norm_fox, q_norm_mem, k_norm_mem, norm_mem, w_mem_kv, w_out, norm_ffn2, w_ffn2_gate, w_ffn2_up, w_ffn2_down, norm_final):
    depth = w_in.shape[0]
    assert depth == 1, "single-layer step"
    B, T, _ = x_prompt.shape
    nbs, seq_s, _ = x_sample.shape
    past = cache_fox_k.shape[2]
    l = 0
    wts = _prep_weights(norm_ffn1[l], norm_mix[l], w_in[l], b_forget[l], conv_w[l], conv_b[l], q_norm_fox[l],
                        k_norm_fox[l], q_norm_mem[l], k_norm_mem[l], norm_mem[l], w_mem_kv[l], w_out[l],
                        norm_ffn2[l], norm_final[l])
    tile = TOKEN_TILE
    n_s = nbs * seq_s

    x1s, wts["wg1"], wts["wu1"], wts["wd1"] = _ffn_stream(
        (x_sample.reshape(n_s, D_MODEL),), wts["g1"], w_ffn1_gate[l], w_ffn1_up[l], w_ffn1_down[l])
    ycs, qas, kas, vbs, kfs, vfs, lfs, qms, ncs, wts["win"], wts["wtail"] = _stage1(
        x1s.reshape(1, n_s, D_MODEL), state_conv[l], wts, nb=tile // seq_s, seq=seq_s, carry=False)
    per_seq = lambda a: a.reshape(nbs, seq_s, a.shape[-1])
    feat_major = lambda a: a.reshape(a.shape[0], a.shape[1], -1).transpose(0, 2, 1)
    yas = _sample_attention(
        per_seq(qas), per_seq(kas), per_seq(vbs), per_seq(qms),
        feat_major(cache_fox_k[l]), feat_major(cache_fox_v[l]), feat_major(cache_fox_logf[l]),
        feat_major(cache_mem_k[l]), feat_major(cache_mem_v[l]))
    y_sample, wts["wg2"], wts["wu2"], wts["wd2"] = _ffn_stream(
        (x1s, ycs.reshape(n_s, CONV_DIM), yas.reshape(n_s, ATT_COLS)), wts["g2"],
        w_ffn2_gate[l], w_ffn2_up[l], w_ffn2_down[l], wo=wts["wo"], gfin=wts["gfin"])

    mk_p, mv_p, mkb, mvb = _memkv(mem_prompt, wts)
    x1, yc, qa, ka, vb, kf, vf, lf, qm, nc = _stage1(x_prompt, None, wts, nb=1, seq=tile, carry=True)
    ya = _prompt_attention(qa, ka, vb, qm, mkb, mvb, tq=ATT_QUERY_TILE, tk=ATT_KEY_TILE)
    y_prompt = _stage3(x1, yc, ya, wts, tm=tile)

    heads = lambda a, b, t, nh: a.reshape(1, b, t, nh, HEAD_DIM)
    token_major = lambda a, nh: a.reshape(a.shape[0], nh, -1, a.shape[2]).transpose(0, 3, 1, 2)[None]
    return (y_prompt, y_sample.reshape(nbs, seq_s, D_MODEL),
            token_major(kf, FOX_HEADS), token_major(vf, FOX_HEADS), lf.transpose(0, 2, 1)[None],
            nc.reshape(1, B, CONV_K - 1, CONV_DIM),
            token_major(mk_p, MEM_HEADS), token_major(mv_p, MEM_HEADS),
            heads(kfs, nbs, seq_s, FOX_HEADS), heads(vfs, nbs, seq_s, FOX_HEADS),
            lfs.reshape(1, nbs, seq_s, FOX_HEADS), ncs.reshape(1, nbs, CONV_K - 1, CONV_DIM))
```

```python
import functools

import jax
import jax.numpy as jnp
from jax import lax
from jax.experimental import pallas as pl
from jax.experimental.pallas import tpu as pltpu

F32 = jnp.float32
BF16 = jnp.bfloat16

D_MODEL = 1024
HEAD_DIM = 64
CONV_DIM = 256
CONV_K = 3
FOX_HEADS = 8
FOX_DIM = FOX_HEADS * HEAD_DIM
MEM_HEADS = 4
MEM_DIM = MEM_HEADS * HEAD_DIM
N_MEM = 256
D_FF = 2816
EPS = 1e-6

LANES = 128
PAIR = 2 * HEAD_DIM
FF_CHUNK = 256
N_FF_CHUNKS = D_FF // FF_CHUNK
AUG = LANES
AUG_GROUP = AUG // FOX_HEADS
N_SPLIT = 3
QK_COLS = FOX_DIM + AUG
ATT_COLS = FOX_DIM + MEM_DIM
COL_Q = 3 * CONV_DIM
COL_K = COL_Q + FOX_DIM
COL_V = COL_K + FOX_DIM
COL_F = COL_V + FOX_DIM
COL_QM = COL_F + FOX_HEADS
TAIL_COLS = MEM_DIM + AUG + LANES
NEG = -1e30
SCALE = HEAD_DIM ** -0.5
LOG2E = 1.4426950408889634
VMEM_LIMIT = 56 * 1024 * 1024
TOKEN_TILE = 512
ATT_KEY_TILE = 512
ATT_QUERY_TILE = 1024
SAMPLE_SEQS_PER_STEP = 4


def _idiv(x, d):
    if d & (d - 1) == 0:
        return lax.shift_right_logical(x, d.bit_length() - 1)
    return x // d


def _imod(x, d):
    if d & (d - 1) == 0:
        return x & (d - 1)
    return x % d


def _rms(x, g):
    ms = jnp.mean(x * x, axis=-1, keepdims=True)
    return x * lax.rsqrt(ms + EPS) * g


def _pair_headnorm(x, g):
    lo = lax.broadcasted_iota(jnp.int32, (1, PAIR), 1) < HEAD_DIM
    x2 = x * x
    s_lo = jnp.sum(jnp.where(lo, x2, 0.0), axis=-1, keepdims=True)
    s_hi = jnp.sum(jnp.where(lo, 0.0, x2), axis=-1, keepdims=True)
    r = jnp.where(lo, lax.rsqrt(s_lo * (1.0 / HEAD_DIM) + EPS),
                  lax.rsqrt(s_hi * (1.0 / HEAD_DIM) + EPS))
    return x * r * g


def _store_cols(ref, sl, x, transposed):
    if transposed:
        ref[sl, :] = x.T
    else:
        ref[:, sl] = x


def _split3(x):
    hi = x.astype(BF16).astype(F32)
    r = x - hi
    mid = r.astype(BF16).astype(F32)
    lo = (r - mid).astype(BF16).astype(F32)
    return hi, mid, lo


def _ffn(xn, wg_ref, wu_ref, wd_ref):
    acc = None
    for c in range(N_FF_CHUNKS):
        sl = slice(c * FF_CHUNK, (c + 1) * FF_CHUNK)
        g = jnp.dot(xn, wg_ref[:, sl], preferred_element_type=F32)
        u = jnp.dot(xn, wu_ref[:, sl], preferred_element_type=F32)
        a = (g * jax.nn.sigmoid(g) * u).astype(BF16)
        d = jnp.dot(a, wd_ref[sl, :], preferred_element_type=F32)
        acc = d if acc is None else acc + d
    return acc


def _log_sigmoid(x):
    return jnp.minimum(x, 0.0) - jnp.log1p(jnp.exp(-jnp.abs(x)))


def _softmax_rows(s):
    m = jnp.max(s, axis=-1, keepdims=True)
    p = jnp.exp(s - m)
    return p, jnp.sum(p, axis=-1, keepdims=True)


def _dot_nt(a, b):
    return lax.dot_general(a, b, (((1,), (1,)), ((), ())), preferred_element_type=F32)


_STAGE1_SHARED_IN = ("gmix", "conv_w", "conv_b", "gq", "gk", "gqm", "bf_rep", "bf")
_STAGE1_SHARED_OUT = ("yc", "qa", "ka", "vb", "kf", "vf", "lf", "qm", "nc")
_STAGE1_IN = {True: ("x", "g1", "wg1", "wu1", "wd1", "win", "wtail") + _STAGE1_SHARED_IN,
              False: ("x1", "state", "win32t") + _STAGE1_SHARED_IN}
_STAGE1_OUT = {True: ("x1",) + _STAGE1_SHARED_OUT, False: _STAGE1_SHARED_OUT + ("win", "wtail")}


def _stage1_body(nb, seq, carry, *refs):
    tm = nb * seq
    names = _STAGE1_IN[carry] + _STAGE1_OUT[carry]
    r = dict(zip(names, refs[:len(names)]))
    cs_ref, cum_ref = refs[len(names):]
    gmix_ref, cw_ref, cb_ref, gq_ref, gk_ref, gqm_ref, bfr_ref, bf_ref = (r[n] for n in _STAGE1_SHARED_IN)
    yc_ref, qa_ref, ka_ref, vb_ref, kf_ref, vf_ref, lf_ref, qm_ref, nc_ref = (r[n] for n in _STAGE1_SHARED_OUT)
    win_ref, wtail_ref = r["win"], r["wtail"]
    if carry:
        x = r["x"][...]
        y = _ffn(_rms(x, r["g1"][...]).astype(BF16), r["wg1"], r["wu1"], r["wd1"])
        x1 = x + 0.5 * y
        r["x1"][...] = x1
    else:
        x1 = r["x1"][...]
        st_ref = r["state"]

        @pl.when(pl.program_id(1) == 0)
        def _():
            w_t = r["win32t"]
            win_ref[...] = w_t[0:COL_F, :].T.astype(BF16)
            f_rows = w_t[COL_F:COL_QM, :]
            f_rep = jnp.concatenate([jnp.broadcast_to(f_rows[hd:hd + 1, :], (AUG_GROUP, D_MODEL))
                                     for hd in range(FOX_HEADS)], axis=0)
            tail_t = jnp.concatenate([w_t[COL_QM:COL_QM + MEM_DIM, :], f_rep, f_rows,
                                      jnp.zeros((LANES - FOX_HEADS, D_MODEL), F32)], axis=0)
            wtail_ref[...] = tail_t.T.astype(BF16)
    h = _rms(x1, gmix_ref[...]).astype(BF16)

    ucb = jnp.dot(h, win_ref[:, 0:COL_Q], preferred_element_type=F32)
    ci = ucb[:, CONV_DIM:2 * CONV_DIM] * ucb[:, 0:CONV_DIM]
    if carry:
        @pl.when(pl.program_id(1) == 0)
        def _():
            cs_ref[:, 0:8, :] = jnp.zeros((nb, 8, CONV_DIM), F32)
    else:
        cs_ref[:, 8 - (CONV_K - 1):8, :] = st_ref[...]
    cs_ref[:, 8:8 + seq, :] = ci.reshape(nb, seq, CONV_DIM)
    conv = cb_ref[...] + cw_ref[CONV_K - 1:CONV_K, :] * ci
    for i in range(CONV_K - 1):
        shifted = cs_ref[:, 8 - (CONV_K - 1) + i:8 - (CONV_K - 1) + i + seq, :]
        conv = conv + cw_ref[i:i + 1, :] * shifted.reshape(tm, CONV_DIM)
    yc_ref[...] = (ucb[:, 2 * CONV_DIM:3 * CONV_DIM] * conv).astype(BF16)
    tail = cs_ref[:, seq:seq + 8, :]
    nc_ref[...] = tail[:, 8 - (CONV_K - 1):, :].reshape(nc_ref.shape)
    if carry:
        cs_ref[:, 0:8, :] = tail

    for name, col, g_ref, out_ref in (("q", COL_Q, gq_ref, qa_ref), ("k", COL_K, gk_ref, ka_ref)):
        pr = jnp.dot(h, win_ref[:, col:col + FOX_DIM], preferred_element_type=F32)
        for g in range(FOX_HEADS // 2):
            sl = slice(g * PAIR, (g + 1) * PAIR)
            xn = _pair_headnorm(pr[:, sl], g_ref[:, sl])
            if name == "k":
                _store_cols(kf_ref, sl, xn, carry)
            out_ref[:, sl] = xn.astype(BF16)
    v = jnp.dot(h, win_ref[:, COL_V:COL_V + FOX_DIM], preferred_element_type=F32)
    for g in range(FOX_HEADS // 2):
        sl = slice(g * PAIR, (g + 1) * PAIR)
        _store_cols(vf_ref, sl, v[:, sl], carry)
    vb_ref[...] = v.astype(BF16)

    tail_cols = jnp.dot(h, wtail_ref[...], preferred_element_type=F32)
    for g in range(MEM_HEADS // 2):
        sl = slice(g * PAIR, (g + 1) * PAIR)
        qm_ref[:, sl] = _pair_headnorm(tail_cols[:, sl], gqm_ref[:, sl]).astype(BF16)
    fl = tail_cols[:, MEM_DIM:]
    lf = _log_sigmoid(fl[:, AUG:2 * AUG] + bf_ref[...])
    if carry:
        lf_ref[...] = lf.T[0:FOX_HEADS, :]
    else:
        lf_ref[...] = lf[:, 0:FOX_HEADS]
    logf = _log_sigmoid(fl[:, 0:AUG] + bfr_ref[...])

    row = lax.broadcasted_iota(jnp.int32, (tm, tm), 0)
    col = lax.broadcasted_iota(jnp.int32, (tm, tm), 1)
    tri = col <= row
    if nb > 1:
        tri = jnp.logical_and(tri, _idiv(row, seq) == _idiv(col, seq))
    tri = jnp.where(tri, 1.0, 0.0).astype(BF16)
    hi, mid, lo = _split3(logf)
    parts = jnp.concatenate([hi.astype(BF16), mid.astype(BF16), lo.astype(BF16)], axis=1)
    cs = jnp.dot(tri, parts, preferred_element_type=F32)
    cum = cs[:, 0:AUG] + cs[:, AUG:2 * AUG] + cs[:, 2 * AUG:3 * AUG]
    if carry:
        @pl.when(pl.program_id(1) == 0)
        def _():
            cum_ref[...] = jnp.zeros(cum_ref.shape, F32)
        cum = cum + cum_ref[0:1, :]
        cum_ref[0:1, :] = cum[tm - 1:tm, :]

    j = _imod(lax.broadcasted_iota(jnp.int32, (1, AUG), 1), AUG_GROUP)
    terms = _split3(cum * LOG2E)
    aq = jnp.where(j < 2 * N_SPLIT, 1.0, 0.0)
    ak = jnp.where(j < N_SPLIT, 1.0, 0.0)
    for i, term in enumerate(terms):
        aq = jnp.where(j == i, term, aq)
        ak = jnp.where(j == N_SPLIT + i, -term, ak)
    qa_ref[:, FOX_DIM:QK_COLS] = aq.astype(BF16)
    ka_ref[:, FOX_DIM:QK_COLS] = ak.astype(BF16)


def _stage1(x3, state, wts, *, nb, seq, carry):
    G, T, _ = x3.shape
    tm = nb * seq
    nt = T // tm
    const = lambda shape: pl.BlockSpec(shape, lambda g, t: (0,) * len(shape), pipeline_mode=pl.Buffered(1))
    tok = lambda cols: pl.BlockSpec((None, tm, cols), lambda g, t: (g, t, 0))
    if carry:
        feat = lambda rows: pl.BlockSpec((None, rows, tm), lambda g, t: (g, 0, t))
        feat_shape = lambda rows: jax.ShapeDtypeStruct((G, rows, T), F32)
    else:
        feat = lambda rows: pl.BlockSpec((None, tm, rows), lambda g, t: (g, t, 0))
        feat_shape = lambda rows: jax.ShapeDtypeStruct((G, T, rows), F32)
    shp = lambda cols, dt: jax.ShapeDtypeStruct((G, T, cols), dt)
    if carry:
        nc_spec = pl.BlockSpec((None, CONV_K - 1, CONV_DIM), lambda g, t: (g, 0, 0))
        nc_shape = jax.ShapeDtypeStruct((G, CONV_K - 1, CONV_DIM), F32)
    else:
        nc_spec = pl.BlockSpec((nb, CONV_K - 1, CONV_DIM), lambda g, t: (t, 0, 0))
        nc_shape = jax.ShapeDtypeStruct((nt * nb, CONV_K - 1, CONV_DIM), F32)
    tiled_in = {"x": (tok(D_MODEL), x3), "x1": (tok(D_MODEL), x3), "state": (nc_spec, state)}
    in_specs, args = [], []
    for n in _STAGE1_IN[carry]:
        spec, arg = tiled_in[n] if n in tiled_in else (const(wts[n].shape), wts[n])
        in_specs.append(spec)
        args.append(arg)
    outs = {"x1": (tok(D_MODEL), shp(D_MODEL, F32)), "yc": (tok(CONV_DIM), shp(CONV_DIM, BF16)),
            "qa": (tok(QK_COLS), shp(QK_COLS, BF16)), "ka": (tok(QK_COLS), shp(QK_COLS, BF16)),
            "vb": (tok(FOX_DIM), shp(FOX_DIM, BF16)), "kf": (feat(FOX_DIM), feat_shape(FOX_DIM)),
            "vf": (feat(FOX_DIM), feat_shape(FOX_DIM)), "lf": (feat(FOX_HEADS), feat_shape(FOX_HEADS)),
            "qm": (tok(MEM_DIM), shp(MEM_DIM, BF16)), "nc": (nc_spec, nc_shape)}
    if not carry:
        for n, w_shape in (("win", (D_MODEL, COL_F)), ("wtail", (D_MODEL, TAIL_COLS))):
            outs[n] = (pl.BlockSpec(w_shape, lambda g, t: (0, 0)), jax.ShapeDtypeStruct(w_shape, BF16))
    out_specs = [outs[n][0] for n in _STAGE1_OUT[carry]]
    out_shape = [outs[n][1] for n in _STAGE1_OUT[carry]]
    return pl.pallas_call(
        functools.partial(_stage1_body, nb, seq, carry),
        grid=(G, nt),
        in_specs=in_specs,
        out_specs=out_specs,
        out_shape=out_shape,
        scratch_shapes=[pltpu.VMEM((nb, seq + 8, CONV_DIM), F32), pltpu.VMEM((8, AUG), F32)],
        compiler_params=pltpu.CompilerParams(
            dimension_semantics=("arbitrary", "arbitrary"), vmem_limit_bytes=VMEM_LIMIT),
        name="stage1_carry" if carry else "stage1_batched",
    )(*args)


def _stage3_body(x1_ref, yc_ref, ya_ref, wo_ref, g2_ref, wg_ref, wu_ref, wd_ref, gfin_ref, y_ref):
    x2 = (x1_ref[...]
          + jnp.dot(yc_ref[...], wo_ref[0:CONV_DIM, :], preferred_element_type=F32)
          + jnp.dot(ya_ref[...], wo_ref[CONV_DIM:, :], preferred_element_type=F32))
    y = _ffn(_rms(x2, g2_ref[...]).astype(BF16), wg_ref, wu_ref, wd_ref)
    y_ref[...] = _rms(x2 + 0.5 * y, gfin_ref[...])


def _stage3(x1, yc, ya, wts, *, tm):
    G, T, _ = x1.shape
    const = lambda shape: pl.BlockSpec(shape, lambda g, t: (0,) * len(shape), pipeline_mode=pl.Buffered(1))
    tok = lambda cols: pl.BlockSpec((None, tm, cols), lambda g, t: (g, t, 0))
    names = ("wo", "g2", "wg2", "wu2", "wd2", "gfin")
    return pl.pallas_call(
        _stage3_body,
        grid=(G, T // tm),
        in_specs=[tok(D_MODEL), tok(CONV_DIM), tok(ATT_COLS)] + [const(wts[n].shape) for n in names],
        out_specs=tok(D_MODEL),
        out_shape=jax.ShapeDtypeStruct((G, T, D_MODEL), F32),
        compiler_params=pltpu.CompilerParams(
            dimension_semantics=("arbitrary", "arbitrary"), vmem_limit_bytes=VMEM_LIMIT),
        name="stage3",
    )(x1, yc, ya, *[wts[n] for n in names])


def _ffn_stream_body(with_out_proj, with_final_norm, *refs):
    refs = list(refs)
    n_x = 4 if with_out_proj else 1
    x_refs, refs = refs[:n_x], refs[n_x:]
    g_ref, wg_ref, wu_ref, wd_ref = refs[:4]
    refs = refs[4:]
    gfin_ref = refs.pop(0) if with_final_norm else None
    y_ref, wgb_ref, wub_ref, wdb_ref, x_ref, xn_ref, acc_ref = refs
    c = pl.program_id(0)

    @pl.when(c == 0)
    def _():
        if with_out_proj:
            x1_ref, yc_ref, ya_ref, wo_ref = x_refs
            x = (x1_ref[...]
                 + jnp.dot(yc_ref[...], wo_ref[0:CONV_DIM, :], preferred_element_type=F32)
                 + jnp.dot(ya_ref[...], wo_ref[CONV_DIM:, :], preferred_element_type=F32))
        else:
            x = x_refs[0][...]
        x_ref[...] = x
        xn_ref[...] = _rms(x, g_ref[...]).astype(BF16)
        acc_ref[...] = jnp.zeros(acc_ref.shape, F32)

    wg = wg_ref[...].astype(BF16)
    wu = wu_ref[...].astype(BF16)
    wd = wd_ref[...].astype(BF16)
    wgb_ref[...] = wg
    wub_ref[...] = wu
    wdb_ref[...] = wd
    xn = xn_ref[...]
    g = jnp.dot(xn, wg, preferred_element_type=F32)
    u = jnp.dot(xn, wu, preferred_element_type=F32)
    a = (g * jax.nn.sigmoid(g) * u).astype(BF16)
    acc_ref[...] += jnp.dot(a, wd, preferred_element_type=F32)

    @pl.when(c == pl.num_programs(0) - 1)
    def _():
        y = x_ref[...] + 0.5 * acc_ref[...]
        y_ref[...] = _rms(y, gfin_ref[...]) if with_final_norm else y


def _ffn_stream(x_parts, gain, wg, wu, wd, *, wo=None, gfin=None):
    m = x_parts[0].shape[0]
    const = lambda a: pl.BlockSpec(a.shape, lambda c: (0,) * a.ndim, pipeline_mode=pl.Buffered(1))
    col_chunk = pl.BlockSpec((D_MODEL, FF_CHUNK), lambda c: (0, c))
    row_chunk = pl.BlockSpec((FF_CHUNK, D_MODEL), lambda c: (c, 0))
    args = list(x_parts) + ([wo] if wo is not None else []) + [gain]
    in_specs = [const(a) for a in args] + [col_chunk, col_chunk, row_chunk]
    args += [wg, wu, wd]
    if gfin is not None:
        in_specs.append(const(gfin))
        args.append(gfin)
    return pl.pallas_call(
        functools.partial(_ffn_stream_body, wo is not None, gfin is not None),
        grid=(N_FF_CHUNKS,),
        in_specs=in_specs,
        out_specs=[pl.BlockSpec((m, D_MODEL), lambda c: (0, 0)), col_chunk, col_chunk, row_chunk],
        out_shape=[jax.ShapeDtypeStruct((m, D_MODEL), F32), jax.ShapeDtypeStruct(wg.shape, BF16),
                   jax.ShapeDtypeStruct(wu.shape, BF16), jax.ShapeDtypeStruct(wd.shape, BF16)],
        scratch_shapes=[pltpu.VMEM((m, D_MODEL), F32), pltpu.VMEM((m, D_MODEL), BF16),
                        pltpu.VMEM((m, D_MODEL), F32)],
        compiler_params=pltpu.CompilerParams(dimension_semantics=("arbitrary",), vmem_limit_bytes=VMEM_LIMIT),
        name="ffn_stream_out" if wo is not None else "ffn_stream_in",
    )(*args)


def _memkv_body(mem_ref, gmem_ref, w_ref, gk_ref, mk_ref, mv_ref, mkb_ref, mvb_ref):
    kv = jnp.dot(_rms(mem_ref[...], gmem_ref[...]).astype(BF16), w_ref[...], preferred_element_type=F32)
    for g in range(MEM_HEADS // 2):
        sl = slice(g * PAIR, (g + 1) * PAIR)
        mk = _pair_headnorm(kv[:, sl], gk_ref[:, sl])
        mk_ref[sl, :] = mk.T
        mkb_ref[:, sl] = mk.astype(BF16)
    mv = kv[:, MEM_DIM:]
    mv_ref[...] = mv.T
    mvb_ref[...] = mv.astype(BF16)


def _memkv(mem, wts):
    B = mem.shape[0]
    const = lambda shape: pl.BlockSpec(shape, lambda b: (0,) * len(shape))
    blk = lambda cols: pl.BlockSpec((None, N_MEM, cols), lambda b: (b, 0, 0))
    shp = lambda dt: jax.ShapeDtypeStruct((B, N_MEM, MEM_DIM), dt)
    names = ("gmem", "wmem", "gkm")
    return pl.pallas_call(
        _memkv_body,
        grid=(B,),
        in_specs=[blk(D_MODEL)] + [const(wts[n].shape) for n in names],
        out_specs=[blk(MEM_DIM)] * 4,
        out_shape=[shp(F32), shp(F32), shp(BF16), shp(BF16)],
        compiler_params=pltpu.CompilerParams(dimension_semantics=("arbitrary",)),
        name="memkv",
    )(mem, *[wts[n] for n in names])


def _head_masks():
    lane = lax.broadcasted_iota(jnp.int32, (1, LANES), 1)
    lo = lane < HEAD_DIM
    return lo, _idiv(lane, AUG_GROUP)


def _prompt_att_body(tq, tk, iq_ref, ik_ref, qa_ref, ka_ref, vb_ref, qm_ref, mk_ref, mv_ref,
                     o_ref, m_ref, acc_ref):
    p = pl.program_id(1)
    iq = iq_ref[p]
    ik = ik_ref[p]
    lo, aug_head = _head_masks()

    @pl.when(ik == 0)
    def _():
        m_ref[...] = jnp.full(m_ref.shape, NEG, F32)
        acc_ref[...] = jnp.zeros(acc_ref.shape, F32)

    def step(masked, r0, sub):
        rows = tq - r0
        keys = slice(sub * tk, (sub + 1) * tk)
        if masked:
            visible = (lax.broadcasted_iota(jnp.int32, (tk, tk), 1)
                       <= lax.broadcasted_iota(jnp.int32, (tk, tk), 0))
        q_aug = qa_ref[r0:, FOX_DIM:QK_COLS]
        k_aug = ka_ref[keys, FOX_DIM:QK_COLS]
        ones = jnp.ones((tk, LANES), BF16)
        for h in range(FOX_HEADS):
            sl = slice((h // 2) * PAIR, (h // 2 + 1) * PAIR)
            mine = lo if h % 2 == 0 else jnp.logical_not(lo)
            qh = jnp.concatenate([jnp.where(mine, qa_ref[r0:, sl], 0.0).astype(BF16),
                                  jnp.where(aug_head == h, q_aug, 0.0).astype(BF16)], axis=1)
            kh = jnp.concatenate([ka_ref[keys, sl], k_aug], axis=1)
            s = _dot_nt(qh, kh)
            if masked:
                top = jnp.where(visible, s[:tk], NEG)
                s = top if rows == tk else jnp.concatenate([top, s[tk:]], axis=0)
            blocks = [s[:, c * LANES:(c + 1) * LANES] for c in range(tk // LANES)]
            blk_max = functools.reduce(jnp.maximum, blocks)
            m_prev = m_ref[h, r0:, :]
            m_new = jnp.maximum(m_prev, jnp.max(blk_max, axis=-1, keepdims=True))
            alpha = jnp.exp2(m_prev - m_new)
            pexp = jnp.concatenate([jnp.exp2(b - m_new) for b in blocks], axis=1).astype(BF16)
            pv = jnp.dot(pexp, jnp.concatenate([vb_ref[keys, sl], ones], axis=1), preferred_element_type=F32)
            acc_ref[h, r0:, :] = jnp.concatenate([alpha, alpha], axis=1) * acc_ref[h, r0:, :] + pv
            m_ref[h, r0:, :] = m_new

    finish = functools.partial(_attention_finish, lo, qm_ref, mk_ref, mv_ref, o_ref, acc_ref)
    n_sub = tq // tk
    last_block = lax.shift_right_logical(iq, 1)
    odd_tile = (iq & 1) == 1

    @pl.when(ik < last_block)
    def _():
        for sub in range(2 * n_sub):
            step(False, 0, sub)

    @pl.when(jnp.logical_and(ik == last_block, odd_tile))
    def _():
        for sub in range(n_sub):
            step(False, 0, sub)
        for sub in range(n_sub):
            step(True, sub * tk, n_sub + sub)
        finish()

    @pl.when(jnp.logical_and(ik == last_block, jnp.logical_not(odd_tile)))
    def _():
        for sub in range(n_sub):
            step(True, sub * tk, sub)
        finish()


def _attention_finish(lo, qm_ref, mk_ref, mv_ref, o_ref, acc_ref):
    for g in range(FOX_HEADS // 2):
        even = acc_ref[2 * g, :, 0:PAIR] / acc_ref[2 * g, :, PAIR:2 * PAIR]
        odd = acc_ref[2 * g + 1, :, 0:PAIR] / acc_ref[2 * g + 1, :, PAIR:2 * PAIR]
        o_ref[:, g * PAIR:(g + 1) * PAIR] = jnp.where(lo, even, odd).astype(BF16)
    for g in range(MEM_HEADS // 2):
        sl = slice(g * PAIR, (g + 1) * PAIR)
        outs = []
        for mine in (lo, jnp.logical_not(lo)):
            qh = jnp.where(mine, qm_ref[:, sl], 0.0).astype(BF16)
            pexp, lsum = _softmax_rows(_dot_nt(qh, mk_ref[:, sl]) * SCALE)
            outs.append(jnp.dot(pexp.astype(BF16), mv_ref[:, sl], preferred_element_type=F32) / lsum)
        o_ref[:, FOX_DIM + g * PAIR:FOX_DIM + (g + 1) * PAIR] = jnp.where(lo, outs[0], outs[1]).astype(BF16)


def _prompt_attention(qa, ka, vb, qm, mkb, mvb, *, tq, tk):
    B, T, _ = qa.shape
    kb = 2 * tq
    assert tq % tk == 0 and T % kb == 0
    pairs = [(i, j) for i in range(T // tq) for j in range(i // 2 + 1)]
    iq_tab = jnp.asarray([p[0] for p in pairs], jnp.int32)
    ik_tab = jnp.asarray([p[1] for p in pairs], jnp.int32)
    grid_spec = pltpu.PrefetchScalarGridSpec(
        num_scalar_prefetch=2,
        grid=(B, len(pairs)),
        in_specs=[
            pl.BlockSpec((None, tq, QK_COLS), lambda b, p, iq, ik: (b, iq[p], 0)),
            pl.BlockSpec((None, kb, QK_COLS), lambda b, p, iq, ik: (b, ik[p], 0)),
            pl.BlockSpec((None, kb, FOX_DIM), lambda b, p, iq, ik: (b, ik[p], 0)),
            pl.BlockSpec((None, tq, MEM_DIM), lambda b, p, iq, ik: (b, iq[p], 0)),
            pl.BlockSpec((None, N_MEM, MEM_DIM), lambda b, p, iq, ik: (b, 0, 0)),
            pl.BlockSpec((None, N_MEM, MEM_DIM), lambda b, p, iq, ik: (b, 0, 0)),
        ],
        out_specs=pl.BlockSpec((None, tq, ATT_COLS), lambda b, p, iq, ik: (b, iq[p], 0)),
        scratch_shapes=[pltpu.VMEM((FOX_HEADS, tq, LANES), F32), pltpu.VMEM((FOX_HEADS, tq, 2 * PAIR), F32)],
    )
    return pl.pallas_call(
        functools.partial(_prompt_att_body, tq, tk),
        grid_spec=grid_spec,
        out_shape=jax.ShapeDtypeStruct((B, T, ATT_COLS), BF16),
        compiler_params=pltpu.CompilerParams(
            dimension_semantics=("arbitrary", "arbitrary"), vmem_limit_bytes=VMEM_LIMIT),
        name="prompt_attention",
    )(iq_tab, ik_tab, qa, ka, vb, qm, mkb, mvb)


def _diag_blocks(x, nh, rows, width):
    head = _idiv(lax.broadcasted_iota(jnp.int32, (1, nh * width), 1), width)
    out = None
    for h in range(nh):
        part = jnp.where(head == h, x[h * rows:(h + 1) * rows, :], 0.0)
        out = part if out is None else out + part
    return out


def _sample_att_body(seq, past, per_step, *refs):
    for i in range(per_step):
        _sample_att_one(seq, past, *[r.at[i] for r in refs])


def _sample_att_one(seq, past, qa_ref, ka_ref, vb_ref, qm_ref, kt_ref, vt_ref, lt_ref, mkt_ref, mvt_ref, o_ref):
    blk = LANES
    nblk = past // blk
    lt = lt_ref[...]
    x = jnp.concatenate([lt[:, b * blk:(b + 1) * blk] for b in range(nblk)], axis=0)
    n = x.shape[0]
    parts = jnp.concatenate(_split3(x), axis=0).astype(BF16)
    src = lax.broadcasted_iota(jnp.int32, (blk, blk), 0)
    dst = lax.broadcasted_iota(jnp.int32, (blk, blk), 1)
    later = jnp.where(src > dst, 1.0, 0.0).astype(BF16)
    loc = jnp.dot(parts, later, preferred_element_type=F32)
    tot = jnp.dot(parts, jnp.ones((blk, blk), BF16), preferred_element_type=F32)
    loc = loc[0:n] + loc[n:2 * n] + loc[2 * n:3 * n]
    tot = tot[0:n] + tot[n:2 * n] + tot[2 * n:3 * n]
    running = jnp.zeros((FOX_HEADS, blk), F32)
    suffix = [None] * nblk
    for b in reversed(range(nblk)):
        rows = slice(b * FOX_HEADS, (b + 1) * FOX_HEADS)
        suffix[b] = loc[rows] + running
        running = running + tot[rows]
    rt = jnp.concatenate(suffix, axis=1) * LOG2E
    bias = jnp.concatenate([jnp.broadcast_to(rt[h:h + 1, :], (seq, past)) for h in range(FOX_HEADS)], axis=0)

    nrow = FOX_HEADS * seq
    lane = lax.broadcasted_iota(jnp.int32, (nrow, QK_COLS), 1)
    lane_head = jnp.where(lane < FOX_DIM, _idiv(lane, HEAD_DIM), _idiv(lane - FOX_DIM, AUG_GROUP))
    row_head = _idiv(lax.broadcasted_iota(jnp.int32, (nrow, QK_COLS), 0), seq)
    qbd = jnp.where(lane_head == row_head, jnp.concatenate([qa_ref[...]] * FOX_HEADS, axis=0), 0.0).astype(BF16)
    aug_row = _imod(lax.broadcasted_iota(jnp.int32, (AUG, past), 0), AUG_GROUP)
    kt = jnp.concatenate([kt_ref[...].astype(BF16), jnp.where(aug_row < N_SPLIT, 1.0, 0.0).astype(BF16)], axis=0)
    s_past = jnp.dot(qbd, kt, preferred_element_type=F32) + bias
    s_new = _dot_nt(qbd, ka_ref[...])
    qi = _imod(lax.broadcasted_iota(jnp.int32, (nrow, seq), 0), seq)
    kj = lax.broadcasted_iota(jnp.int32, (nrow, seq), 1)
    s_new = jnp.where(kj <= qi, s_new, NEG)
    m = jnp.maximum(jnp.max(s_past, axis=-1, keepdims=True), jnp.max(s_new, axis=-1, keepdims=True))
    p_past = jnp.exp2(s_past - m)
    p_new = jnp.exp2(s_new - m)
    lsum = jnp.sum(p_past, axis=-1, keepdims=True) + jnp.sum(p_new, axis=-1, keepdims=True)
    o = (_dot_nt(p_past.astype(BF16), vt_ref[...].astype(BF16))
         + jnp.dot(p_new.astype(BF16), vb_ref[...], preferred_element_type=F32)) / lsum
    o_ref[:, 0:FOX_DIM] = _diag_blocks(o, FOX_HEADS, seq, HEAD_DIM).astype(BF16)

    nrow_m = MEM_HEADS * seq
    lane_m = _idiv(lax.broadcasted_iota(jnp.int32, (nrow_m, MEM_DIM), 1), HEAD_DIM)
    row_m = _idiv(lax.broadcasted_iota(jnp.int32, (nrow_m, MEM_DIM), 0), seq)
    qbd_m = jnp.where(lane_m == row_m, jnp.concatenate([qm_ref[...]] * MEM_HEADS, axis=0), 0.0).astype(BF16)
    pm, lm = _softmax_rows(jnp.dot(qbd_m, mkt_ref[...].astype(BF16), preferred_element_type=F32) * SCALE)
    om = _dot_nt(pm.astype(BF16), mvt_ref[...].astype(BF16)) / lm
    o_ref[:, FOX_DIM:ATT_COLS] = _diag_blocks(om, MEM_HEADS, seq, HEAD_DIM).astype(BF16)


def _sample_attention(qa, ka, vb, qm, pk, pv, plf, mk, mv):
    nbatch, seq, _ = qa.shape
    past = pk.shape[2]
    per_step = SAMPLE_SEQS_PER_STEP
    assert nbatch % per_step == 0
    blk = lambda rows, cols: pl.BlockSpec((per_step, rows, cols), lambda b: (b, 0, 0))
    return pl.pallas_call(
        functools.partial(_sample_att_body, seq, past, per_step),
        grid=(nbatch // per_step,),
        in_specs=[blk(seq, QK_COLS), blk(seq, QK_COLS), blk(seq, FOX_DIM), blk(seq, MEM_DIM),
                  blk(FOX_DIM, past), blk(FOX_DIM, past), blk(FOX_HEADS, past),
                  blk(MEM_DIM, N_MEM), blk(MEM_DIM, N_MEM)],
        out_specs=blk(seq, ATT_COLS),
        out_shape=jax.ShapeDtypeStruct((nbatch, seq, ATT_COLS), BF16),
        compiler_params=pltpu.CompilerParams(
            dimension_semantics=("arbitrary",), vmem_limit_bytes=VMEM_LIMIT),
        name="sample_attention",
    )(qa, ka, vb, qm, pk, pv, plf, mk, mv)


def _prep_weights(norm_ffn1, norm_mix, w_in, b_forget, conv_w, conv_b,
                  q_norm_fox, k_norm_fox, q_norm_mem, k_norm_mem, norm_mem, w_mem_kv, w_out,
                  norm_ffn2, norm_final):
    row = lambda v: v.reshape(1, -1).astype(F32)
    return {
        "g1": row(norm_ffn1), "gmix": row(norm_mix), "win32t": w_in.T,
        "conv_w": conv_w.astype(F32), "conv_b": row(conv_b),
        "gq": row(jnp.tile(q_norm_fox, FOX_HEADS)) * (SCALE * LOG2E), "gk": row(jnp.tile(k_norm_fox, FOX_HEADS)),
        "gqm": row(jnp.tile(q_norm_mem, MEM_HEADS)),
        "bf_rep": row(jnp.repeat(b_forget, AUG_GROUP)),
        "bf": jnp.pad(row(b_forget), ((0, 0), (0, LANES - FOX_HEADS))),
        "gmem": row(norm_mem), "wmem": w_mem_kv.astype(BF16), "gkm": row(jnp.tile(k_norm_mem, MEM_HEADS)),
        "wo": w_out.astype(BF16), "g2": row(norm_ffn2), "gfin": row(norm_final),
    }


def kernel(x_prompt, x_sample, cache_fox_k, cache_fox_v, cache_fox_logf, state_conv, cache_mem_k, cache_mem_v, mem_prompt, norm_ffn1, w_ffn1_gate, w_ffn1_up, w_ffn1_down, norm_mix, w_in, b_forget, conv_w, conv_b, q_norm_fox, k_norm_fox, q_norm_mem, k_norm_mem, norm_mem, w_mem_kv, w_out, norm_ffn2, w_ffn2_gate, w_ffn2_up, w_ffn2_down, norm_final):
    depth = w_in.shape[0]
    assert depth == 1, "single-layer step"
    B, T, _ = x_prompt.shape
    nbs, seq_s, _ = x_sample.shape
    past = cache_fox_k.shape[2]
    l = 0
    wts = _prep_weights(norm_ffn1[l], norm_mix[l], w_in[l], b_forget[l], conv_w[l], conv_b[l], q_norm_fox[l],
                        k_norm_fox[l], q_norm_mem[l], k_norm_mem[l], norm_mem[l], w_mem_kv[l], w_out[l],
                        norm_ffn2[l], norm_final[l])
    tile = TOKEN_TILE
    n_s = nbs * seq_s

    x1s, wts["wg1"], wts["wu1"], wts["wd1"] = _ffn_stream(
        (x_sample.reshape(n_s, D_MODEL),), wts["g1"], w_ffn1_gate[l], w_ffn1_up[l], w_ffn1_down[l])
    ycs, qas, kas, vbs, kfs, vfs, lfs, qms, ncs, wts["win"], wts["wtail"] = _stage1(
        x1s.reshape(1, n_s, D_MODEL), state_conv[l], wts, nb=tile // seq_s, seq=seq_s, carry=False)
    per_seq = lambda a: a.reshape(nbs, seq_s, a.shape[-1])
    feat_major = lambda a: a.reshape(a.shape[0], a.shape[1], -1).transpose(0, 2, 1)
    yas = _sample_attention(
        per_seq(qas), per_seq(kas), per_seq(vbs), per_seq(qms),
        feat_major(cache_fox_k[l]), feat_major(cache_fox_v[l]), feat_major(cache_fox_logf[l]),
        feat_major(cache_mem_k[l]), feat_major(cache_mem_v[l]))
    y_sample, wts["wg2"], wts["wu2"], wts["wd2"] = _ffn_stream(
        (x1s, ycs.reshape(n_s, CONV_DIM), yas.reshape(n_s, ATT_COLS)), wts["g2"],
        w_ffn2_gate[l], w_ffn2_up[l], w_ffn2_down[l], wo=wts["wo"], gfin=wts["gfin"])

    mk_p, mv_p, mkb, mvb = _memkv(mem_prompt, wts)
    x1, yc, qa, ka, vb, kf, vf, lf, qm, nc = _stage1(x_prompt, None, wts, nb=1, seq=tile, carry=True)
    ya = _prompt_attention(qa, ka, vb, qm, mkb, mvb, tq=ATT_QUERY_TILE, tk=ATT_KEY_TILE)
    y_prompt = _stage3(x1, yc, ya, wts, tm=tile)

    heads = lambda a, b, t, nh: a.reshape(1, b, t, nh, HEAD_DIM)
    token_major = lambda a, nh: a.reshape(a.shape[0], nh, -1, a.shape[2]).transpose(0, 3, 1, 2)[None]
    return (y_prompt, y_sample.reshape(nbs, seq_s, D_MODEL),
            token_major(kf, FOX_HEADS), token_major(vf, FOX_HEADS), lf.transpose(0, 2, 1)[None],
            nc.reshape(1, B, CONV_K - 1, CONV_DIM),
            token_major(mk_p, MEM_HEADS), token_major(mv_p, MEM_HEADS),
            heads(kfs, nbs, seq_s, FOX_HEADS), heads(vfs, nbs, seq_s, FOX_HEADS),
            lfs.reshape(1, nbs, seq_s, FOX_HEADS), ncs.reshape(1, nbs, CONV_K - 1, CONV_DIM))
```

```python
import functools

import jax
import jax.numpy as jnp
from jax import lax
from jax.experimental import pallas as pl
from jax.experimental.pallas import tpu as pltpu

F32 = jnp.float32
BF16 = jnp.bfloat16

D_MODEL = 1024
HEAD_DIM = 64
CONV_DIM = 256
CONV_K = 3
FOX_HEADS = 8
FOX_DIM = FOX_HEADS * HEAD_DIM
MEM_HEADS = 4
MEM_DIM = MEM_HEADS * HEAD_DIM
N_MEM = 256
D_FF = 2816
EPS = 1e-6

LANES = 128
PAIR = 2 * HEAD_DIM
FF_CHUNK = 256
N_FF_CHUNKS = D_FF // FF_CHUNK
AUG = LANES
AUG_GROUP = AUG // FOX_HEADS
N_SPLIT = 3
QK_COLS = FOX_DIM + AUG
ATT_COLS = FOX_DIM + MEM_DIM
COL_Q = 3 * CONV_DIM
COL_K = COL_Q + FOX_DIM
COL_V = COL_K + FOX_DIM
COL_F = COL_V + FOX_DIM
COL_QM = COL_F + FOX_HEADS
TAIL_COLS = MEM_DIM + AUG + LANES
NEG = -1e30
SCALE = HEAD_DIM ** -0.5
LOG2E = 1.4426950408889634
VMEM_LIMIT = 56 * 1024 * 1024
TOKEN_TILE = 512
ATT_KEY_TILE = 512
ATT_QUERY_TILE = 1024
SAMPLE_SEQS_PER_STEP = 4
WEIGHT_RING = 3


def _idiv(x, d):
    if d & (d - 1) == 0:
        return lax.shift_right_logical(x, d.bit_length() - 1)
    return x // d


def _imod(x, d):
    if d & (d - 1) == 0:
        return x & (d - 1)
    return x % d


def _rms(x, g):
    ms = jnp.mean(x * x, axis=-1, keepdims=True)
    return x * lax.rsqrt(ms + EPS) * g


def _pair_headnorm(x, g):
    lo = lax.broadcasted_iota(jnp.int32, (1, PAIR), 1) < HEAD_DIM
    x2 = x * x
    s_lo = jnp.sum(jnp.where(lo, x2, 0.0), axis=-1, keepdims=True)
    s_hi = jnp.sum(jnp.where(lo, 0.0, x2), axis=-1, keepdims=True)
    r = jnp.where(lo, lax.rsqrt(s_lo * (1.0 / HEAD_DIM) + EPS),
                  lax.rsqrt(s_hi * (1.0 / HEAD_DIM) + EPS))
    return x * r * g


def _store_cols(ref, sl, x, transposed):
    if transposed:
        ref[sl, :] = x.T
    else:
        ref[:, sl] = x


def _split3(x):
    hi = x.astype(BF16).astype(F32)
    r = x - hi
    mid = r.astype(BF16).astype(F32)
    lo = (r - mid).astype(BF16).astype(F32)
    return hi, mid, lo


def _ffn(xn, wg_ref, wu_ref, wd_ref):
    acc = None
    for c in range(N_FF_CHUNKS):
        sl = slice(c * FF_CHUNK, (c + 1) * FF_CHUNK)
        g = jnp.dot(xn, wg_ref[:, sl], preferred_element_type=F32)
        u = jnp.dot(xn, wu_ref[:, sl], preferred_element_type=F32)
        a = (g * jax.nn.sigmoid(g) * u).astype(BF16)
        d = jnp.dot(a, wd_ref[sl, :], preferred_element_type=F32)
        acc = d if acc is None else acc + d
    return acc


def _log_sigmoid(x):
    return jnp.minimum(x, 0.0) - jnp.log1p(jnp.exp(-jnp.abs(x)))


def _softmax_rows(s):
    m = jnp.max(s, axis=-1, keepdims=True)
    p = jnp.exp(s - m)
    return p, jnp.sum(p, axis=-1, keepdims=True)


def _dot_nt(a, b):
    return lax.dot_general(a, b, (((1,), (1,)), ((), ())), preferred_element_type=F32)


_STAGE1_SHARED_IN = ("gmix", "conv_w", "conv_b", "gq", "gk", "gqm", "bf_rep", "bf")
_STAGE1_SHARED_OUT = ("yc", "qa", "ka", "vb", "kf", "vf", "lf", "qm", "nc")
_STAGE1_IN = {True: ("x", "g1", "wg1", "wu1", "wd1", "win", "wtail") + _STAGE1_SHARED_IN,
              False: ("x1", "state", "win32t") + _STAGE1_SHARED_IN}
_STAGE1_OUT = {True: ("x1",) + _STAGE1_SHARED_OUT, False: _STAGE1_SHARED_OUT + ("win", "wtail")}


def _stage1_body(nb, seq, carry, *refs):
    tm = nb * seq
    names = _STAGE1_IN[carry] + _STAGE1_OUT[carry]
    r = dict(zip(names, refs[:len(names)]))
    cs_ref, cum_ref = refs[len(names):]
    gmix_ref, cw_ref, cb_ref, gq_ref, gk_ref, gqm_ref, bfr_ref, bf_ref = (r[n] for n in _STAGE1_SHARED_IN)
    yc_ref, qa_ref, ka_ref, vb_ref, kf_ref, vf_ref, lf_ref, qm_ref, nc_ref = (r[n] for n in _STAGE1_SHARED_OUT)
    win_ref, wtail_ref = r["win"], r["wtail"]
    if carry:
        x = r["x"][...]
        y = _ffn(_rms(x, r["g1"][...]).astype(BF16), r["wg1"], r["wu1"], r["wd1"])
        x1 = x + 0.5 * y
        r["x1"][...] = x1
    else:
        x1 = r["x1"][...]
        st_ref = r["state"]

        @pl.when(pl.program_id(1) == 0)
        def _():
            w_t = r["win32t"]
            win_ref[...] = w_t[0:COL_F, :].T.astype(BF16)
            f_rows = w_t[COL_F:COL_QM, :]
            f_rep = jnp.concatenate([jnp.broadcast_to(f_rows[hd:hd + 1, :], (AUG_GROUP, D_MODEL))
                                     for hd in range(FOX_HEADS)], axis=0)
            tail_t = jnp.concatenate([w_t[COL_QM:COL_QM + MEM_DIM, :], f_rep, f_rows,
                                      jnp.zeros((LANES - FOX_HEADS, D_MODEL), F32)], axis=0)
            wtail_ref[...] = tail_t.T.astype(BF16)
    h = _rms(x1, gmix_ref[...]).astype(BF16)

    ucb = jnp.dot(h, win_ref[:, 0:COL_Q], preferred_element_type=F32)
    ci = ucb[:, CONV_DIM:2 * CONV_DIM] * ucb[:, 0:CONV_DIM]
    if carry:
        @pl.when(pl.program_id(1) == 0)
        def _():
            cs_ref[:, 0:8, :] = jnp.zeros((nb, 8, CONV_DIM), F32)
    else:
        cs_ref[:, 8 - (CONV_K - 1):8, :] = st_ref[...]
    cs_ref[:, 8:8 + seq, :] = ci.reshape(nb, seq, CONV_DIM)
    conv = cb_ref[...] + cw_ref[CONV_K - 1:CONV_K, :] * ci
    for i in range(CONV_K - 1):
        shifted = cs_ref[:, 8 - (CONV_K - 1) + i:8 - (CONV_K - 1) + i + seq, :]
        conv = conv + cw_ref[i:i + 1, :] * shifted.reshape(tm, CONV_DIM)
    yc_ref[...] = (ucb[:, 2 * CONV_DIM:3 * CONV_DIM] * conv).astype(BF16)
    tail = cs_ref[:, seq:seq + 8, :]
    nc_ref[...] = tail[:, 8 - (CONV_K - 1):, :].reshape(nc_ref.shape)
    if carry:
        cs_ref[:, 0:8, :] = tail

    for name, col, g_ref, out_ref in (("q", COL_Q, gq_ref, qa_ref), ("k", COL_K, gk_ref, ka_ref)):
        pr = jnp.dot(h, win_ref[:, col:col + FOX_DIM], preferred_element_type=F32)
        for g in range(FOX_HEADS // 2):
            sl = slice(g * PAIR, (g + 1) * PAIR)
            xn = _pair_headnorm(pr[:, sl], g_ref[:, sl])
            if name == "k":
                _store_cols(kf_ref, sl, xn, carry)
            out_ref[:, sl] = xn.astype(BF16)
    v = jnp.dot(h, win_ref[:, COL_V:COL_V + FOX_DIM], preferred_element_type=F32)
    for g in range(FOX_HEADS // 2):
        sl = slice(g * PAIR, (g + 1) * PAIR)
        _store_cols(vf_ref, sl, v[:, sl], carry)
    vb_ref[...] = v.astype(BF16)

    tail_cols = jnp.dot(h, wtail_ref[...], preferred_element_type=F32)
    for g in range(MEM_HEADS // 2):
        sl = slice(g * PAIR, (g + 1) * PAIR)
        qm_ref[:, sl] = _pair_headnorm(tail_cols[:, sl], gqm_ref[:, sl]).astype(BF16)
    fl = tail_cols[:, MEM_DIM:]
    lf = _log_sigmoid(fl[:, AUG:2 * AUG] + bf_ref[...])
    if carry:
        lf_ref[...] = lf.T[0:FOX_HEADS, :]
    else:
        lf_ref[...] = lf[:, 0:FOX_HEADS]
    logf = _log_sigmoid(fl[:, 0:AUG] + bfr_ref[...])

    row = lax.broadcasted_iota(jnp.int32, (tm, tm), 0)
    col = lax.broadcasted_iota(jnp.int32, (tm, tm), 1)
    tri = col <= row
    if nb > 1:
        tri = jnp.logical_and(tri, _idiv(row, seq) == _idiv(col, seq))
    tri = jnp.where(tri, 1.0, 0.0).astype(BF16)
    hi, mid, lo = _split3(logf)
    parts = jnp.concatenate([hi.astype(BF16), mid.astype(BF16), lo.astype(BF16)], axis=1)
    cs = jnp.dot(tri, parts, preferred_element_type=F32)
    cum = cs[:, 0:AUG] + cs[:, AUG:2 * AUG] + cs[:, 2 * AUG:3 * AUG]
    if carry:
        @pl.when(pl.program_id(1) == 0)
        def _():
            cum_ref[...] = jnp.zeros(cum_ref.shape, F32)
        cum = cum + cum_ref[0:1, :]
        cum_ref[0:1, :] = cum[tm - 1:tm, :]

    j = _imod(lax.broadcasted_iota(jnp.int32, (1, AUG), 1), AUG_GROUP)
    terms = _split3(cum * LOG2E)
    aq = jnp.where(j < 2 * N_SPLIT, 1.0, 0.0)
    ak = jnp.where(j < N_SPLIT, 1.0, 0.0)
    for i, term in enumerate(terms):
        aq = jnp.where(j == i, term, aq)
        ak = jnp.where(j == N_SPLIT + i, -term, ak)
    qa_ref[:, FOX_DIM:QK_COLS] = aq.astype(BF16)
    ka_ref[:, FOX_DIM:QK_COLS] = ak.astype(BF16)


def _stage1(x3, state, wts, *, nb, seq, carry):
    G, T, _ = x3.shape
    tm = nb * seq
    nt = T // tm
    const = lambda shape: pl.BlockSpec(shape, lambda g, t: (0,) * len(shape), pipeline_mode=pl.Buffered(1))
    tok = lambda cols: pl.BlockSpec((None, tm, cols), lambda g, t: (g, t, 0))
    if carry:
        feat = lambda rows: pl.BlockSpec((None, rows, tm), lambda g, t: (g, 0, t))
        feat_shape = lambda rows: jax.ShapeDtypeStruct((G, rows, T), F32)
    else:
        feat = lambda rows: pl.BlockSpec((None, tm, rows), lambda g, t: (g, t, 0))
        feat_shape = lambda rows: jax.ShapeDtypeStruct((G, T, rows), F32)
    shp = lambda cols, dt: jax.ShapeDtypeStruct((G, T, cols), dt)
    if carry:
        nc_spec = pl.BlockSpec((None, CONV_K - 1, CONV_DIM), lambda g, t: (g, 0, 0))
        nc_shape = jax.ShapeDtypeStruct((G, CONV_K - 1, CONV_DIM), F32)
    else:
        nc_spec = pl.BlockSpec((nb, CONV_K - 1, CONV_DIM), lambda g, t: (t, 0, 0))
        nc_shape = jax.ShapeDtypeStruct((nt * nb, CONV_K - 1, CONV_DIM), F32)
    tiled_in = {"x": (tok(D_MODEL), x3), "x1": (tok(D_MODEL), x3), "state": (nc_spec, state)}
    in_specs, args = [], []
    for n in _STAGE1_IN[carry]:
        spec, arg = tiled_in[n] if n in tiled_in else (const(wts[n].shape), wts[n])
        in_specs.append(spec)
        args.append(arg)
    outs = {"x1": (tok(D_MODEL), shp(D_MODEL, F32)), "yc": (tok(CONV_DIM), shp(CONV_DIM, BF16)),
            "qa": (tok(QK_COLS), shp(QK_COLS, BF16)), "ka": (tok(QK_COLS), shp(QK_COLS, BF16)),
            "vb": (tok(FOX_DIM), shp(FOX_DIM, BF16)), "kf": (feat(FOX_DIM), feat_shape(FOX_DIM)),
            "vf": (feat(FOX_DIM), feat_shape(FOX_DIM)), "lf": (feat(FOX_HEADS), feat_shape(FOX_HEADS)),
            "qm": (tok(MEM_DIM), shp(MEM_DIM, BF16)), "nc": (nc_spec, nc_shape)}
    if not carry:
        for n, w_shape in (("win", (D_MODEL, COL_F)), ("wtail", (D_MODEL, TAIL_COLS))):
            outs[n] = (pl.BlockSpec(w_shape, lambda g, t: (0, 0)), jax.ShapeDtypeStruct(w_shape, BF16))
    out_specs = [outs[n][0] for n in _STAGE1_OUT[carry]]
    out_shape = [outs[n][1] for n in _STAGE1_OUT[carry]]
    return pl.pallas_call(
        functools.partial(_stage1_body, nb, seq, carry),
        grid=(G, nt),
        in_specs=in_specs,
        out_specs=out_specs,
        out_shape=out_shape,
        scratch_shapes=[pltpu.VMEM((nb, seq + 8, CONV_DIM), F32), pltpu.VMEM((8, AUG), F32)],
        compiler_params=pltpu.CompilerParams(
            dimension_semantics=("arbitrary", "arbitrary"), vmem_limit_bytes=VMEM_LIMIT),
        name="stage1_carry" if carry else "stage1_batched",
    )(*args)


def _stage3_body(x1_ref, yc_ref, ya_ref, wo_ref, g2_ref, wg_ref, wu_ref, wd_ref, gfin_ref, y_ref):
    x2 = (x1_ref[...]
          + jnp.dot(yc_ref[...], wo_ref[0:CONV_DIM, :], preferred_element_type=F32)
          + jnp.dot(ya_ref[...], wo_ref[CONV_DIM:, :], preferred_element_type=F32))
    y = _ffn(_rms(x2, g2_ref[...]).astype(BF16), wg_ref, wu_ref, wd_ref)
    y_ref[...] = _rms(x2 + 0.5 * y, gfin_ref[...])


def _stage3(x1, yc, ya, wts, *, tm):
    G, T, _ = x1.shape
    const = lambda shape: pl.BlockSpec(shape, lambda g, t: (0,) * len(shape), pipeline_mode=pl.Buffered(1))
    tok = lambda cols: pl.BlockSpec((None, tm, cols), lambda g, t: (g, t, 0))
    names = ("wo", "g2", "wg2", "wu2", "wd2", "gfin")
    return pl.pallas_call(
        _stage3_body,
        grid=(G, T // tm),
        in_specs=[tok(D_MODEL), tok(CONV_DIM), tok(ATT_COLS)] + [const(wts[n].shape) for n in names],
        out_specs=tok(D_MODEL),
        out_shape=jax.ShapeDtypeStruct((G, T, D_MODEL), F32),
        compiler_params=pltpu.CompilerParams(
            dimension_semantics=("arbitrary", "arbitrary"), vmem_limit_bytes=VMEM_LIMIT),
        name="stage3",
    )(x1, yc, ya, *[wts[n] for n in names])


def _ffn_stream_body(with_out_proj, with_final_norm, *refs):
    refs = list(refs)
    n_x = 4 if with_out_proj else 1
    x_refs, refs = refs[:n_x], refs[n_x:]
    g_ref, wg_hbm, wu_hbm, wd_hbm = refs[:4]
    refs = refs[4:]
    gfin_ref = refs.pop(0) if with_final_norm else None
    y_ref, wgb_ref, wub_ref, wdb_ref, x_ref, xn_ref, acc_ref, wg_buf, wu_buf, wd_buf, sem = refs
    c = pl.program_id(0)

    def chunk_copies(k):
        slot = lax.rem(k, WEIGHT_RING)
        off = pl.multiple_of(k * FF_CHUNK, FF_CHUNK)
        return (pltpu.make_async_copy(wg_hbm.at[:, pl.ds(off, FF_CHUNK)], wg_buf.at[slot], sem.at[0, slot]),
                pltpu.make_async_copy(wu_hbm.at[:, pl.ds(off, FF_CHUNK)], wu_buf.at[slot], sem.at[1, slot]),
                pltpu.make_async_copy(wd_hbm.at[pl.ds(off, FF_CHUNK), :], wd_buf.at[slot], sem.at[2, slot]))

    @pl.when(c == 0)
    def _():
        for k in range(WEIGHT_RING - 1):
            for cp in chunk_copies(jnp.int32(k)):
                cp.start()

    @pl.when(c + (WEIGHT_RING - 1) < pl.num_programs(0))
    def _():
        for cp in chunk_copies(c + (WEIGHT_RING - 1)):
            cp.start()

    @pl.when(c == 0)
    def _():
        if with_out_proj:
            x1_ref, yc_ref, ya_ref, wo_ref = x_refs
            x = (x1_ref[...]
                 + jnp.dot(yc_ref[...], wo_ref[0:CONV_DIM, :], preferred_element_type=F32)
                 + jnp.dot(ya_ref[...], wo_ref[CONV_DIM:, :], preferred_element_type=F32))
        else:
            x = x_refs[0][...]
        x_ref[...] = x
        xn_ref[...] = _rms(x, g_ref[...]).astype(BF16)
        acc_ref[...] = jnp.zeros(acc_ref.shape, F32)

    for cp in chunk_copies(c):
        cp.wait()
    slot = lax.rem(c, WEIGHT_RING)
    wg = wg_buf[slot].astype(BF16)
    wu = wu_buf[slot].astype(BF16)
    wd = wd_buf[slot].astype(BF16)
    wgb_ref[...] = wg
    wub_ref[...] = wu
    wdb_ref[...] = wd
    xn = xn_ref[...]
    g = jnp.dot(xn, wg, preferred_element_type=F32)
    u = jnp.dot(xn, wu, preferred_element_type=F32)
    a = (g * jax.nn.sigmoid(g) * u).astype(BF16)
    acc_ref[...] += jnp.dot(a, wd, preferred_element_type=F32)

    @pl.when(c == pl.num_programs(0) - 1)
    def _():
        y = x_ref[...] + 0.5 * acc_ref[...]
        y_ref[...] = _rms(y, gfin_ref[...]) if with_final_norm else y


def _ffn_stream(x_parts, gain, wg, wu, wd, *, wo=None, gfin=None):
    m = x_parts[0].shape[0]
    const = lambda a: pl.BlockSpec(a.shape, lambda c: (0,) * a.ndim, pipeline_mode=pl.Buffered(1))
    col_chunk = pl.BlockSpec((D_MODEL, FF_CHUNK), lambda c: (0, c))
    row_chunk = pl.BlockSpec((FF_CHUNK, D_MODEL), lambda c: (c, 0))
    args = list(x_parts) + ([wo] if wo is not None else []) + [gain]
    in_specs = [const(a) for a in args] + [pl.BlockSpec(memory_space=pl.ANY)] * 3
    args += [wg, wu, wd]
    if gfin is not None:
        in_specs.append(const(gfin))
        args.append(gfin)
    return pl.pallas_call(
        functools.partial(_ffn_stream_body, wo is not None, gfin is not None),
        grid=(N_FF_CHUNKS,),
        in_specs=in_specs,
        out_specs=[pl.BlockSpec((m, D_MODEL), lambda c: (0, 0)), col_chunk, col_chunk, row_chunk],
        out_shape=[jax.ShapeDtypeStruct((m, D_MODEL), F32), jax.ShapeDtypeStruct(wg.shape, BF16),
                   jax.ShapeDtypeStruct(wu.shape, BF16), jax.ShapeDtypeStruct(wd.shape, BF16)],
        scratch_shapes=[pltpu.VMEM((m, D_MODEL), F32), pltpu.VMEM((m, D_MODEL), BF16),
                        pltpu.VMEM((m, D_MODEL), F32),
                        pltpu.VMEM((WEIGHT_RING, D_MODEL, FF_CHUNK), F32),
                        pltpu.VMEM((WEIGHT_RING, D_MODEL, FF_CHUNK), F32),
                        pltpu.VMEM((WEIGHT_RING, FF_CHUNK, D_MODEL), F32),
                        pltpu.SemaphoreType.DMA((3, WEIGHT_RING))],
        compiler_params=pltpu.CompilerParams(dimension_semantics=("arbitrary",), vmem_limit_bytes=VMEM_LIMIT),
        name="ffn_stream_out" if wo is not None else "ffn_stream_in",
    )(*args)


def _memkv_body(mem_ref, gmem_ref, w_ref, gk_ref, mk_ref, mv_ref, mkb_ref, mvb_ref):
    kv = jnp.dot(_rms(mem_ref[...], gmem_ref[...]).astype(BF16), w_ref[...], preferred_element_type=F32)
    for g in range(MEM_HEADS // 2):
        sl = slice(g * PAIR, (g + 1) * PAIR)
        mk = _pair_headnorm(kv[:, sl], gk_ref[:, sl])
        mk_ref[sl, :] = mk.T
        mkb_ref[:, sl] = mk.astype(BF16)
    mv = kv[:, MEM_DIM:]
    mv_ref[...] = mv.T
    mvb_ref[...] = mv.astype(BF16)


def _memkv(mem, wts):
    B = mem.shape[0]
    const = lambda shape: pl.BlockSpec(shape, lambda b: (0,) * len(shape))
    blk = lambda cols: pl.BlockSpec((None, N_MEM, cols), lambda b: (b, 0, 0))
    shp = lambda dt: jax.ShapeDtypeStruct((B, N_MEM, MEM_DIM), dt)
    names = ("gmem", "wmem", "gkm")
    return pl.pallas_call(
        _memkv_body,
        grid=(B,),
        in_specs=[blk(D_MODEL)] + [const(wts[n].shape) for n in names],
        out_specs=[blk(MEM_DIM)] * 4,
        out_shape=[shp(F32), shp(F32), shp(BF16), shp(BF16)],
        compiler_params=pltpu.CompilerParams(dimension_semantics=("arbitrary",)),
        name="memkv",
    )(mem, *[wts[n] for n in names])


def _head_masks():
    lane = lax.broadcasted_iota(jnp.int32, (1, LANES), 1)
    lo = lane < HEAD_DIM
    return lo, _idiv(lane, AUG_GROUP)


def _prompt_att_body(tq, tk, iq_ref, ik_ref, qa_ref, ka_ref, vb_ref, qm_ref, mk_ref, mv_ref,
                     o_ref, m_ref, acc_ref):
    p = pl.program_id(1)
    iq = iq_ref[p]
    ik = ik_ref[p]
    lo, aug_head = _head_masks()

    @pl.when(ik == 0)
    def _():
        m_ref[...] = jnp.full(m_ref.shape, NEG, F32)
        acc_ref[...] = jnp.zeros(acc_ref.shape, F32)

    def step(masked, r0, sub):
        rows = tq - r0
        keys = slice(sub * tk, (sub + 1) * tk)
        if masked:
            visible = (lax.broadcasted_iota(jnp.int32, (tk, tk), 1)
                       <= lax.broadcasted_iota(jnp.int32, (tk, tk), 0))
        q_aug = qa_ref[r0:, FOX_DIM:QK_COLS]
        k_aug = ka_ref[keys, FOX_DIM:QK_COLS]
        ones = jnp.ones((tk, LANES), BF16)
        for h in range(FOX_HEADS):
            sl = slice((h // 2) * PAIR, (h // 2 + 1) * PAIR)
            mine = lo if h % 2 == 0 else jnp.logical_not(lo)
            qh = jnp.concatenate([jnp.where(mine, qa_ref[r0:, sl], 0.0).astype(BF16),
                                  jnp.where(aug_head == h, q_aug, 0.0).astype(BF16)], axis=1)
            kh = jnp.concatenate([ka_ref[keys, sl], k_aug], axis=1)
            s = _dot_nt(qh, kh)
            if masked:
                top = jnp.where(visible, s[:tk], NEG)
                s = top if rows == tk else jnp.concatenate([top, s[tk:]], axis=0)
            blocks = [s[:, c * LANES:(c + 1) * LANES] for c in range(tk // LANES)]
            blk_max = functools.reduce(jnp.maximum, blocks)
            m_prev = m_ref[h, r0:, :]
            m_new = jnp.maximum(m_prev, jnp.max(blk_max, axis=-1, keepdims=True))
            alpha = jnp.exp2(m_prev - m_new)
            pexp = jnp.concatenate([jnp.exp2(b - m_new) for b in blocks], axis=1).astype(BF16)
            pv = jnp.dot(pexp, jnp.concatenate([vb_ref[keys, sl], ones], axis=1), preferred_element_type=F32)
            acc_ref[h, r0:, :] = jnp.concatenate([alpha, alpha], axis=1) * acc_ref[h, r0:, :] + pv
            m_ref[h, r0:, :] = m_new

    @pl.when(ik < iq)
    def _():
        for sub in range(tq // tk):
            step(False, 0, sub)

    @pl.when(ik == iq)
    def _():
        for sub in range(tq // tk):
            step(True, sub * tk, sub)
        for g in range(FOX_HEADS // 2):
            even = acc_ref[2 * g, :, 0:PAIR] / acc_ref[2 * g, :, PAIR:2 * PAIR]
            odd = acc_ref[2 * g + 1, :, 0:PAIR] / acc_ref[2 * g + 1, :, PAIR:2 * PAIR]
            o_ref[:, g * PAIR:(g + 1) * PAIR] = jnp.where(lo, even, odd).astype(BF16)
        for g in range(MEM_HEADS // 2):
            sl = slice(g * PAIR, (g + 1) * PAIR)
            outs = []
            for mine in (lo, jnp.logical_not(lo)):
                qh = jnp.where(mine, qm_ref[:, sl], 0.0).astype(BF16)
                pexp, lsum = _softmax_rows(_dot_nt(qh, mk_ref[:, sl]) * SCALE)
                outs.append(jnp.dot(pexp.astype(BF16), mv_ref[:, sl], preferred_element_type=F32) / lsum)
            o_ref[:, FOX_DIM + g * PAIR:FOX_DIM + (g + 1) * PAIR] = jnp.where(lo, outs[0], outs[1]).astype(BF16)


def _prompt_attention(qa, ka, vb, qm, mkb, mvb, *, tq, tk):
    B, T, _ = qa.shape
    assert tq % tk == 0 and T % tq == 0
    pairs = [(i, j) for i in range(T // tq) for j in range(i + 1)]
    iq_tab = jnp.asarray([p[0] for p in pairs], jnp.int32)
    ik_tab = jnp.asarray([p[1] for p in pairs], jnp.int32)
    grid_spec = pltpu.PrefetchScalarGridSpec(
        num_scalar_prefetch=2,
        grid=(B, len(pairs)),
        in_specs=[
            pl.BlockSpec((None, tq, QK_COLS), lambda b, p, iq, ik: (b, iq[p], 0)),
            pl.BlockSpec((None, tq, QK_COLS), lambda b, p, iq, ik: (b, ik[p], 0)),
            pl.BlockSpec((None, tq, FOX_DIM), lambda b, p, iq, ik: (b, ik[p], 0)),
            pl.BlockSpec((None, tq, MEM_DIM), lambda b, p, iq, ik: (b, iq[p], 0)),
            pl.BlockSpec((None, N_MEM, MEM_DIM), lambda b, p, iq, ik: (b, 0, 0)),
            pl.BlockSpec((None, N_MEM, MEM_DIM), lambda b, p, iq, ik: (b, 0, 0)),
        ],
        out_specs=pl.BlockSpec((None, tq, ATT_COLS), lambda b, p, iq, ik: (b, iq[p], 0)),
        scratch_shapes=[pltpu.VMEM((FOX_HEADS, tq, LANES), F32), pltpu.VMEM((FOX_HEADS, tq, 2 * PAIR), F32)],
    )
    return pl.pallas_call(
        functools.partial(_prompt_att_body, tq, tk),
        grid_spec=grid_spec,
        out_shape=jax.ShapeDtypeStruct((B, T, ATT_COLS), BF16),
        compiler_params=pltpu.CompilerParams(
            dimension_semantics=("arbitrary", "arbitrary"), vmem_limit_bytes=VMEM_LIMIT),
        name="prompt_attention",
    )(iq_tab, ik_tab, qa, ka, vb, qm, mkb, mvb)


def _diag_blocks(x, nh, rows, width):
    head = _idiv(lax.broadcasted_iota(jnp.int32, (1, nh * width), 1), width)
    out = None
    for h in range(nh):
        part = jnp.where(head == h, x[h * rows:(h + 1) * rows, :], 0.0)
        out = part if out is None else out + part
    return out


def _sample_att_body(seq, past, per_step, *refs):
    for i in range(per_step):
        _sample_att_one(seq, past, *[r.at[i] for r in refs])


def _sample_att_one(seq, past, qa_ref, ka_ref, vb_ref, qm_ref, kt_ref, vt_ref, lt_ref, mkt_ref, mvt_ref, o_ref):
    blk = LANES
    nblk = past // blk
    lt = lt_ref[...]
    x = jnp.concatenate([lt[:, b * blk:(b + 1) * blk] for b in range(nblk)], axis=0)
    n = x.shape[0]
    parts = jnp.concatenate(_split3(x), axis=0).astype(BF16)
    src = lax.broadcasted_iota(jnp.int32, (blk, blk), 0)
    dst = lax.broadcasted_iota(jnp.int32, (blk, blk), 1)
    later = jnp.where(src > dst, 1.0, 0.0).astype(BF16)
    loc = jnp.dot(parts, later, preferred_element_type=F32)
    tot = jnp.dot(parts, jnp.ones((blk, blk), BF16), preferred_element_type=F32)
    loc = loc[0:n] + loc[n:2 * n] + loc[2 * n:3 * n]
    tot = tot[0:n] + tot[n:2 * n] + tot[2 * n:3 * n]
    running = jnp.zeros((FOX_HEADS, blk), F32)
    suffix = [None] * nblk
    for b in reversed(range(nblk)):
        rows = slice(b * FOX_HEADS, (b + 1) * FOX_HEADS)
        suffix[b] = loc[rows] + running
        running = running + tot[rows]
    rt = jnp.concatenate(suffix, axis=1) * LOG2E
    bias = jnp.concatenate([jnp.broadcast_to(rt[h:h + 1, :], (seq, past)) for h in range(FOX_HEADS)], axis=0)

    nrow = FOX_HEADS * seq
    lane = lax.broadcasted_iota(jnp.int32, (nrow, QK_COLS), 1)
    lane_head = jnp.where(lane < FOX_DIM, _idiv(lane, HEAD_DIM), _idiv(lane - FOX_DIM, AUG_GROUP))
    row_head = _idiv(lax.broadcasted_iota(jnp.int32, (nrow, QK_COLS), 0), seq)
    qbd = jnp.where(lane_head == row_head, jnp.concatenate([qa_ref[...]] * FOX_HEADS, axis=0), 0.0).astype(BF16)
    aug_row = _imod(lax.broadcasted_iota(jnp.int32, (AUG, past), 0), AUG_GROUP)
    kt = jnp.concatenate([kt_ref[...].astype(BF16), jnp.where(aug_row < N_SPLIT, 1.0, 0.0).astype(BF16)], axis=0)
    s_past = jnp.dot(qbd, kt, preferred_element_type=F32) + bias
    s_new = _dot_nt(qbd, ka_ref[...])
    qi = _imod(lax.broadcasted_iota(jnp.int32, (nrow, seq), 0), seq)
    kj = lax.broadcasted_iota(jnp.int32, (nrow, seq), 1)
    s_new = jnp.where(kj <= qi, s_new, NEG)
    m = jnp.maximum(jnp.max(s_past, axis=-1, keepdims=True), jnp.max(s_new, axis=-1, keepdims=True))
    p_past = jnp.exp2(s_past - m)
    p_new = jnp.exp2(s_new - m)
    lsum = jnp.sum(p_past, axis=-1, keepdims=True) + jnp.sum(p_new, axis=-1, keepdims=True)
    o = (_dot_nt(p_past.astype(BF16), vt_ref[...].astype(BF16))
         + jnp.dot(p_new.astype(BF16), vb_ref[...], preferred_element_type=F32)) / lsum
    o_ref[:, 0:FOX_DIM] = _diag_blocks(o, FOX_HEADS, seq, HEAD_DIM).astype(BF16)

    nrow_m = MEM_HEADS * seq
    lane_m = _idiv(lax.broadcasted_iota(jnp.int32, (nrow_m, MEM_DIM), 1), HEAD_DIM)
    row_m = _idiv(lax.broadcasted_iota(jnp.int32, (nrow_m, MEM_DIM), 0), seq)
    qbd_m = jnp.where(lane_m == row_m, jnp.concatenate([qm_ref[...]] * MEM_HEADS, axis=0), 0.0).astype(BF16)
    pm, lm = _softmax_rows(jnp.dot(qbd_m, mkt_ref[...].astype(BF16), preferred_element_type=F32) * SCALE)
    om = _dot_nt(pm.astype(BF16), mvt_ref[...].astype(BF16)) / lm
    o_ref[:, FOX_DIM:ATT_COLS] = _diag_blocks(om, MEM_HEADS, seq, HEAD_DIM).astype(BF16)


def _sample_attention(qa, ka, vb, qm, pk, pv, plf, mk, mv):
    nbatch, seq, _ = qa.shape
    past = pk.shape[2]
    per_step = SAMPLE_SEQS_PER_STEP
    assert nbatch % per_step == 0
    blk = lambda rows, cols: pl.BlockSpec((per_step, rows, cols), lambda b: (b, 0, 0))
    return pl.pallas_call(
        functools.partial(_sample_att_body, seq, past, per_step),
        grid=(nbatch // per_step,),
        in_specs=[blk(seq, QK_COLS), blk(seq, QK_COLS), blk(seq, FOX_DIM), blk(seq, MEM_DIM),
                  blk(FOX_DIM, past), blk(FOX_DIM, past), blk(FOX_HEADS, past),
                  blk(MEM_DIM, N_MEM), blk(MEM_DIM, N_MEM)],
        out_specs=blk(seq, ATT_COLS),
        out_shape=jax.ShapeDtypeStruct((nbatch, seq, ATT_COLS), BF16),
        compiler_params=pltpu.CompilerParams(
            dimension_semantics=("arbitrary",), vmem_limit_bytes=VMEM_LIMIT),
        name="sample_attention",
    )(qa, ka, vb, qm, pk, pv, plf, mk, mv)


def _prep_weights(norm_ffn1, norm_mix, w_in, b_forget, conv_w, conv_b,
                  q_norm_fox, k_norm_fox, q_norm_mem, k_norm_mem, norm_mem, w_mem_kv, w_out,
                  norm_ffn2, norm_final):
    row = lambda v: v.reshape(1, -1).astype(F32)
    return {
        "g1": row(norm_ffn1), "gmix": row(norm_mix), "win32t": w_in.T,
        "conv_w": conv_w.astype(F32), "conv_b": row(conv_b),
        "gq": row(jnp.tile(q_norm_fox, FOX_HEADS)) * (SCALE * LOG2E), "gk": row(jnp.tile(k_norm_fox, FOX_HEADS)),
        "gqm": row(jnp.tile(q_norm_mem, MEM_HEADS)),
        "bf_rep": row(jnp.repeat(b_forget, AUG_GROUP)),
        "bf": jnp.pad(row(b_forget), ((0, 0), (0, LANES - FOX_HEADS))),
        "gmem": row(norm_mem), "wmem": w_mem_kv.astype(BF16), "gkm": row(jnp.tile(k_norm_mem, MEM_HEADS)),
        "wo": w_out.astype(BF16), "g2": row(norm_ffn2), "gfin": row(norm_final),
    }


def kernel(x_prompt, x_sample, cache_fox_k, cache_fox_v, cache_fox_logf, state_conv, cache_mem_k, cache_mem_v, mem_prompt, norm_ffn1, w_ffn1_gate, w_ffn1_up, w_ffn1_down, norm_mix, w_in, b_forget, conv_w, conv_b, q_norm_fox, k_norm_fox, q_norm_mem, k_norm_mem, norm_mem, w_mem_kv, w_out, norm_ffn2, w_ffn2_gate, w_ffn2_up, w_ffn2_down, norm_final):
    depth = w_in.shape[0]
    assert depth == 1, "single-layer step"
    B, T, _ = x_prompt.shape
    nbs, seq_s, _ = x_sample.shape
    past = cache_fox_k.shape[2]
    l = 0
    wts = _prep_weights(norm_ffn1[l], norm_mix[l], w_in[l], b_forget[l], conv_w[l], conv_b[l], q_norm_fox[l],
                        k_norm_fox[l], q_norm_mem[l], k_norm_mem[l], norm_mem[l], w_mem_kv[l], w_out[l],
                        norm_ffn2[l], norm_final[l])
    tile = TOKEN_TILE
    n_s = nbs * seq_s

    x1s, wts["wg1"], wts["wu1"], wts["wd1"] = _ffn_stream(
        (x_sample.reshape(n_s, D_MODEL),), wts["g1"], w_ffn1_gate[l], w_ffn1_up[l], w_ffn1_down[l])
    ycs, qas, kas, vbs, kfs, vfs, lfs, qms, ncs, wts["win"], wts["wtail"] = _stage1(
        x1s.reshape(1, n_s, D_MODEL), state_conv[l], wts, nb=tile // seq_s, seq=seq_s, carry=False)
    per_seq = lambda a: a.reshape(nbs, seq_s, a.shape[-1])
    feat_major = lambda a: a.reshape(a.shape[0], a.shape[1], -1).transpose(0, 2, 1)
    yas = _sample_attention(
        per_seq(qas), per_seq(kas), per_seq(vbs), per_seq(qms),
        feat_major(cache_fox_k[l]), feat_major(cache_fox_v[l]), feat_major(cache_fox_logf[l]),
        feat_major(cache_mem_k[l]), feat_major(cache_mem_v[l]))
    y_sample, wts["wg2"], wts["wu2"], wts["wd2"] = _ffn_stream(
        (x1s, ycs.reshape(n_s, CONV_DIM), yas.reshape(n_s, ATT_COLS)), wts["g2"],
        w_ffn2_gate[l], w_ffn2_up[l], w_ffn2_down[l], wo=wts["wo"], gfin=wts["gfin"])

    mk_p, mv_p, mkb, mvb = _memkv(mem_prompt, wts)
    x1, yc, qa, ka, vb, kf, vf, lf, qm, nc = _stage1(x_prompt, None, wts, nb=1, seq=tile, carry=True)
    ya = _prompt_attention(qa, ka, vb, qm, mkb, mvb, tq=ATT_QUERY_TILE, tk=ATT_KEY_TILE)
    y_prompt = _stage3(x1, yc, ya, wts, tm=tile)

    heads = lambda a, b, t, nh: a.reshape(1, b, t, nh, HEAD_DIM)
    token_major = lambda a, nh: a.reshape(a.shape[0], nh, -1, a.shape[2]).transpose(0, 3, 1, 2)[None]
    return (y_prompt, y_sample.reshape(nbs, seq_s, D_MODEL),
            token_major(kf, FOX_HEADS), token_major(vf, FOX_HEADS), lf.transpose(0, 2, 1)[None],
            nc.reshape(1, B, CONV_K - 1, CONV_DIM),
            token_major(mk_p, MEM_HEADS), token_major(mv_p, MEM_HEADS),
            heads(kfs, nbs, seq_s, FOX_HEADS), heads(vfs, nbs, seq_s, FOX_HEADS),
            lfs.reshape(1, nbs, seq_s, FOX_HEADS), ncs.reshape(1, nbs, CONV_K - 1, CONV_DIM))
```

```python
import functools

import jax
import jax.numpy as jnp
from jax import lax
from jax.experimental import pallas as pl
from jax.experimental.pallas import tpu as pltpu

F32 = jnp.float32
BF16 = jnp.bfloat16

D_MODEL = 1024
HEAD_DIM = 64
CONV_DIM = 256
CONV_K = 3
FOX_HEADS = 8
FOX_DIM = FOX_HEADS * HEAD_DIM
MEM_HEADS = 4
MEM_DIM = MEM_HEADS * HEAD_DIM
N_MEM = 256
D_FF = 2816
EPS = 1e-6

LANES = 128
PAIR = 2 * HEAD_DIM
FF_CHUNK = 256
N_FF_CHUNKS = D_FF // FF_CHUNK
AUG = LANES
AUG_GROUP = AUG // FOX_HEADS
N_SPLIT = 3
QK_COLS = FOX_DIM + AUG
ATT_COLS = FOX_DIM + MEM_DIM
COL_Q = 3 * CONV_DIM
COL_K = COL_Q + FOX_DIM
COL_V = COL_K + FOX_DIM
COL_F = COL_V + FOX_DIM
COL_QM = COL_F + FOX_HEADS
TAIL_COLS = MEM_DIM + AUG + LANES
NEG = -1e30
SCALE = HEAD_DIM ** -0.5
LOG2E = 1.4426950408889634
VMEM_LIMIT = 56 * 1024 * 1024
TOKEN_TILE = 512
ATT_KEY_TILE = 512
ATT_QUERY_TILE = 1024
SAMPLE_SEQS_PER_STEP = 4
WEIGHT_RING = 3
KV_RING = 3


def _idiv(x, d):
    if d & (d - 1) == 0:
        return lax.shift_right_logical(x, d.bit_length() - 1)
    return x // d


def _imod(x, d):
    if d & (d - 1) == 0:
        return x & (d - 1)
    return x % d


def _rms(x, g):
    ms = jnp.mean(x * x, axis=-1, keepdims=True)
    return x * lax.rsqrt(ms + EPS) * g


def _pair_headnorm(x, g):
    lo = lax.broadcasted_iota(jnp.int32, (1, PAIR), 1) < HEAD_DIM
    x2 = x * x
    s_lo = jnp.sum(jnp.where(lo, x2, 0.0), axis=-1, keepdims=True)
    s_hi = jnp.sum(jnp.where(lo, 0.0, x2), axis=-1, keepdims=True)
    r = jnp.where(lo, lax.rsqrt(s_lo * (1.0 / HEAD_DIM) + EPS),
                  lax.rsqrt(s_hi * (1.0 / HEAD_DIM) + EPS))
    return x * r * g


def _store_cols(ref, sl, x, transposed):
    if transposed:
        ref[sl, :] = x.T
    else:
        ref[:, sl] = x


def _split3(x):
    hi = x.astype(BF16).astype(F32)
    r = x - hi
    mid = r.astype(BF16).astype(F32)
    lo = (r - mid).astype(BF16).astype(F32)
    return hi, mid, lo


def _ffn(xn, wg_ref, wu_ref, wd_ref):
    acc = None
    for c in range(N_FF_CHUNKS):
        sl = slice(c * FF_CHUNK, (c + 1) * FF_CHUNK)
        g = jnp.dot(xn, wg_ref[:, sl], preferred_element_type=F32)
        u = jnp.dot(xn, wu_ref[:, sl], preferred_element_type=F32)
        a = (g * jax.nn.sigmoid(g) * u).astype(BF16)
        d = jnp.dot(a, wd_ref[sl, :], preferred_element_type=F32)
        acc = d if acc is None else acc + d
    return acc


def _log_sigmoid(x):
    return jnp.minimum(x, 0.0) - jnp.log1p(jnp.exp(-jnp.abs(x)))


def _softmax_rows(s):
    m = jnp.max(s, axis=-1, keepdims=True)
    p = jnp.exp(s - m)
    return p, jnp.sum(p, axis=-1, keepdims=True)


def _dot_nt(a, b):
    return lax.dot_general(a, b, (((1,), (1,)), ((), ())), preferred_element_type=F32)


_STAGE1_SHARED_IN = ("gmix", "conv_w", "conv_b", "gq", "gk", "gqm", "bf_rep", "bf")
_STAGE1_SHARED_OUT = ("yc", "qa", "ka", "vb", "kf", "vf", "lf", "qm", "nc")
_STAGE1_IN = {True: ("x", "g1", "wg1", "wu1", "wd1", "win", "wtail") + _STAGE1_SHARED_IN,
              False: ("x1", "state", "win32t") + _STAGE1_SHARED_IN}
_STAGE1_OUT = {True: ("x1",) + _STAGE1_SHARED_OUT, False: _STAGE1_SHARED_OUT + ("win", "wtail")}


def _stage1_body(nb, seq, carry, *refs):
    tm = nb * seq
    names = _STAGE1_IN[carry] + _STAGE1_OUT[carry]
    r = dict(zip(names, refs[:len(names)]))
    cs_ref, cum_ref = refs[len(names):]
    gmix_ref, cw_ref, cb_ref, gq_ref, gk_ref, gqm_ref, bfr_ref, bf_ref = (r[n] for n in _STAGE1_SHARED_IN)
    yc_ref, qa_ref, ka_ref, vb_ref, kf_ref, vf_ref, lf_ref, qm_ref, nc_ref = (r[n] for n in _STAGE1_SHARED_OUT)
    win_ref, wtail_ref = r["win"], r["wtail"]
    if carry:
        x = r["x"][...]
        y = _ffn(_rms(x, r["g1"][...]).astype(BF16), r["wg1"], r["wu1"], r["wd1"])
        x1 = x + 0.5 * y
        r["x1"][...] = x1
    else:
        x1 = r["x1"][...]
        st_ref = r["state"]

        @pl.when(pl.program_id(1) == 0)
        def _():
            w_t = r["win32t"]
            win_ref[...] = w_t[0:COL_F, :].T.astype(BF16)
            f_rows = w_t[COL_F:COL_QM, :]
            f_rep = jnp.concatenate([jnp.broadcast_to(f_rows[hd:hd + 1, :], (AUG_GROUP, D_MODEL))
                                     for hd in range(FOX_HEADS)], axis=0)
            tail_t = jnp.concatenate([w_t[COL_QM:COL_QM + MEM_DIM, :], f_rep, f_rows,
                                      jnp.zeros((LANES - FOX_HEADS, D_MODEL), F32)], axis=0)
            wtail_ref[...] = tail_t.T.astype(BF16)
    h = _rms(x1, gmix_ref[...]).astype(BF16)

    ucb = jnp.dot(h, win_ref[:, 0:COL_Q], preferred_element_type=F32)
    ci = ucb[:, CONV_DIM:2 * CONV_DIM] * ucb[:, 0:CONV_DIM]
    if carry:
        @pl.when(pl.program_id(1) == 0)
        def _():
            cs_ref[:, 0:8, :] = jnp.zeros((nb, 8, CONV_DIM), F32)
    else:
        cs_ref[:, 8 - (CONV_K - 1):8, :] = st_ref[...]
    cs_ref[:, 8:8 + seq, :] = ci.reshape(nb, seq, CONV_DIM)
    conv = cb_ref[...] + cw_ref[CONV_K - 1:CONV_K, :] * ci
    for i in range(CONV_K - 1):
        shifted = cs_ref[:, 8 - (CONV_K - 1) + i:8 - (CONV_K - 1) + i + seq, :]
        conv = conv + cw_ref[i:i + 1, :] * shifted.reshape(tm, CONV_DIM)
    yc_ref[...] = (ucb[:, 2 * CONV_DIM:3 * CONV_DIM] * conv).astype(BF16)
    tail = cs_ref[:, seq:seq + 8, :]
    nc_ref[...] = tail[:, 8 - (CONV_K - 1):, :].reshape(nc_ref.shape)
    if carry:
        cs_ref[:, 0:8, :] = tail

    for name, col, g_ref, out_ref in (("q", COL_Q, gq_ref, qa_ref), ("k", COL_K, gk_ref, ka_ref)):
        pr = jnp.dot(h, win_ref[:, col:col + FOX_DIM], preferred_element_type=F32)
        for g in range(FOX_HEADS // 2):
            sl = slice(g * PAIR, (g + 1) * PAIR)
            xn = _pair_headnorm(pr[:, sl], g_ref[:, sl])
            if name == "k":
                _store_cols(kf_ref, sl, xn, carry)
            out_ref[:, sl] = xn.astype(BF16)
    v = jnp.dot(h, win_ref[:, COL_V:COL_V + FOX_DIM], preferred_element_type=F32)
    for g in range(FOX_HEADS // 2):
        sl = slice(g * PAIR, (g + 1) * PAIR)
        _store_cols(vf_ref, sl, v[:, sl], carry)
    vb_ref[...] = v.astype(BF16)

    tail_cols = jnp.dot(h, wtail_ref[...], preferred_element_type=F32)
    for g in range(MEM_HEADS // 2):
        sl = slice(g * PAIR, (g + 1) * PAIR)
        qm_ref[:, sl] = _pair_headnorm(tail_cols[:, sl], gqm_ref[:, sl]).astype(BF16)
    fl = tail_cols[:, MEM_DIM:]
    lf = _log_sigmoid(fl[:, AUG:2 * AUG] + bf_ref[...])
    if carry:
        lf_ref[...] = lf.T[0:FOX_HEADS, :]
    else:
        lf_ref[...] = lf[:, 0:FOX_HEADS]
    logf = _log_sigmoid(fl[:, 0:AUG] + bfr_ref[...])

    row = lax.broadcasted_iota(jnp.int32, (tm, tm), 0)
    col = lax.broadcasted_iota(jnp.int32, (tm, tm), 1)
    tri = col <= row
    if nb > 1:
        tri = jnp.logical_and(tri, _idiv(row, seq) == _idiv(col, seq))
    tri = jnp.where(tri, 1.0, 0.0).astype(BF16)
    hi, mid, lo = _split3(logf)
    parts = jnp.concatenate([hi.astype(BF16), mid.astype(BF16), lo.astype(BF16)], axis=1)
    cs = jnp.dot(tri, parts, preferred_element_type=F32)
    cum = cs[:, 0:AUG] + cs[:, AUG:2 * AUG] + cs[:, 2 * AUG:3 * AUG]
    if carry:
        @pl.when(pl.program_id(1) == 0)
        def _():
            cum_ref[...] = jnp.zeros(cum_ref.shape, F32)
        cum = cum + cum_ref[0:1, :]
        cum_ref[0:1, :] = cum[tm - 1:tm, :]

    j = _imod(lax.broadcasted_iota(jnp.int32, (1, AUG), 1), AUG_GROUP)
    terms = _split3(cum * LOG2E)
    aq = jnp.where(j < 2 * N_SPLIT, 1.0, 0.0)
    ak = jnp.where(j < N_SPLIT, 1.0, 0.0)
    for i, term in enumerate(terms):
        aq = jnp.where(j == i, term, aq)
        ak = jnp.where(j == N_SPLIT + i, -term, ak)
    qa_ref[:, FOX_DIM:QK_COLS] = aq.astype(BF16)
    ka_ref[:, FOX_DIM:QK_COLS] = ak.astype(BF16)


def _stage1(x3, state, wts, *, nb, seq, carry):
    G, T, _ = x3.shape
    tm = nb * seq
    nt = T // tm
    const = lambda shape: pl.BlockSpec(shape, lambda g, t: (0,) * len(shape), pipeline_mode=pl.Buffered(1))
    tok = lambda cols: pl.BlockSpec((None, tm, cols), lambda g, t: (g, t, 0))
    if carry:
        feat = lambda rows: pl.BlockSpec((None, rows, tm), lambda g, t: (g, 0, t))
        feat_shape = lambda rows: jax.ShapeDtypeStruct((G, rows, T), F32)
    else:
        feat = lambda rows: pl.BlockSpec((None, tm, rows), lambda g, t: (g, t, 0))
        feat_shape = lambda rows: jax.ShapeDtypeStruct((G, T, rows), F32)
    shp = lambda cols, dt: jax.ShapeDtypeStruct((G, T, cols), dt)
    if carry:
        nc_spec = pl.BlockSpec((None, CONV_K - 1, CONV_DIM), lambda g, t: (g, 0, 0))
        nc_shape = jax.ShapeDtypeStruct((G, CONV_K - 1, CONV_DIM), F32)
    else:
        nc_spec = pl.BlockSpec((nb, CONV_K - 1, CONV_DIM), lambda g, t: (t, 0, 0))
        nc_shape = jax.ShapeDtypeStruct((nt * nb, CONV_K - 1, CONV_DIM), F32)
    tiled_in = {"x": (tok(D_MODEL), x3), "x1": (tok(D_MODEL), x3), "state": (nc_spec, state)}
    in_specs, args = [], []
    for n in _STAGE1_IN[carry]:
        spec, arg = tiled_in[n] if n in tiled_in else (const(wts[n].shape), wts[n])
        in_specs.append(spec)
        args.append(arg)
    outs = {"x1": (tok(D_MODEL), shp(D_MODEL, F32)), "yc": (tok(CONV_DIM), shp(CONV_DIM, BF16)),
            "qa": (tok(QK_COLS), shp(QK_COLS, BF16)), "ka": (tok(QK_COLS), shp(QK_COLS, BF16)),
            "vb": (tok(FOX_DIM), shp(FOX_DIM, BF16)), "kf": (feat(FOX_DIM), feat_shape(FOX_DIM)),
            "vf": (feat(FOX_DIM), feat_shape(FOX_DIM)), "lf": (feat(FOX_HEADS), feat_shape(FOX_HEADS)),
            "qm": (tok(MEM_DIM), shp(MEM_DIM, BF16)), "nc": (nc_spec, nc_shape)}
    if not carry:
        for n, w_shape in (("win", (D_MODEL, COL_F)), ("wtail", (D_MODEL, TAIL_COLS))):
            outs[n] = (pl.BlockSpec(w_shape, lambda g, t: (0, 0)), jax.ShapeDtypeStruct(w_shape, BF16))
    out_specs = [outs[n][0] for n in _STAGE1_OUT[carry]]
    out_shape = [outs[n][1] for n in _STAGE1_OUT[carry]]
    return pl.pallas_call(
        functools.partial(_stage1_body, nb, seq, carry),
        grid=(G, nt),
        in_specs=in_specs,
        out_specs=out_specs,
        out_shape=out_shape,
        scratch_shapes=[pltpu.VMEM((nb, seq + 8, CONV_DIM), F32), pltpu.VMEM((8, AUG), F32)],
        compiler_params=pltpu.CompilerParams(
            dimension_semantics=("arbitrary", "arbitrary"), vmem_limit_bytes=VMEM_LIMIT),
        name="stage1_carry" if carry else "stage1_batched",
    )(*args)


def _stage3_body(x1_ref, yc_ref, ya_ref, wo_ref, g2_ref, wg_ref, wu_ref, wd_ref, gfin_ref, y_ref):
    x2 = (x1_ref[...]
          + jnp.dot(yc_ref[...], wo_ref[0:CONV_DIM, :], preferred_element_type=F32)
          + jnp.dot(ya_ref[...], wo_ref[CONV_DIM:, :], preferred_element_type=F32))
    y = _ffn(_rms(x2, g2_ref[...]).astype(BF16), wg_ref, wu_ref, wd_ref)
    y_ref[...] = _rms(x2 + 0.5 * y, gfin_ref[...])


def _stage3(x1, yc, ya, wts, *, tm):
    G, T, _ = x1.shape
    const = lambda shape: pl.BlockSpec(shape, lambda g, t: (0,) * len(shape), pipeline_mode=pl.Buffered(1))
    tok = lambda cols: pl.BlockSpec((None, tm, cols), lambda g, t: (g, t, 0))
    names = ("wo", "g2", "wg2", "wu2", "wd2", "gfin")
    return pl.pallas_call(
        _stage3_body,
        grid=(G, T // tm),
        in_specs=[tok(D_MODEL), tok(CONV_DIM), tok(ATT_COLS)] + [const(wts[n].shape) for n in names],
        out_specs=tok(D_MODEL),
        out_shape=jax.ShapeDtypeStruct((G, T, D_MODEL), F32),
        compiler_params=pltpu.CompilerParams(
            dimension_semantics=("arbitrary", "arbitrary"), vmem_limit_bytes=VMEM_LIMIT),
        name="stage3",
    )(x1, yc, ya, *[wts[n] for n in names])


def _ffn_stream_body(with_out_proj, with_final_norm, *refs):
    refs = list(refs)
    n_x = 4 if with_out_proj else 1
    x_refs, refs = refs[:n_x], refs[n_x:]
    g_ref, wg_hbm, wu_hbm, wd_hbm = refs[:4]
    refs = refs[4:]
    gfin_ref = refs.pop(0) if with_final_norm else None
    y_ref, wgb_ref, wub_ref, wdb_ref, x_ref, xn_ref, acc_ref, wg_buf, wu_buf, wd_buf, sem = refs
    c = pl.program_id(0)

    def chunk_copies(k):
        slot = lax.rem(k, WEIGHT_RING)
        off = pl.multiple_of(k * FF_CHUNK, FF_CHUNK)
        return (pltpu.make_async_copy(wg_hbm.at[:, pl.ds(off, FF_CHUNK)], wg_buf.at[slot], sem.at[0, slot]),
                pltpu.make_async_copy(wu_hbm.at[:, pl.ds(off, FF_CHUNK)], wu_buf.at[slot], sem.at[1, slot]),
                pltpu.make_async_copy(wd_hbm.at[pl.ds(off, FF_CHUNK), :], wd_buf.at[slot], sem.at[2, slot]))

    @pl.when(c == 0)
    def _():
        for k in range(WEIGHT_RING - 1):
            for cp in chunk_copies(jnp.int32(k)):
                cp.start()

    @pl.when(c + (WEIGHT_RING - 1) < pl.num_programs(0))
    def _():
        for cp in chunk_copies(c + (WEIGHT_RING - 1)):
            cp.start()

    @pl.when(c == 0)
    def _():
        if with_out_proj:
            x1_ref, yc_ref, ya_ref, wo_ref = x_refs
            x = (x1_ref[...]
                 + jnp.dot(yc_ref[...], wo_ref[0:CONV_DIM, :], preferred_element_type=F32)
                 + jnp.dot(ya_ref[...], wo_ref[CONV_DIM:, :], preferred_element_type=F32))
        else:
            x = x_refs[0][...]
        x_ref[...] = x
        xn_ref[...] = _rms(x, g_ref[...]).astype(BF16)
        acc_ref[...] = jnp.zeros(acc_ref.shape, F32)

    for cp in chunk_copies(c):
        cp.wait()
    slot = lax.rem(c, WEIGHT_RING)
    wg = wg_buf[slot].astype(BF16)
    wu = wu_buf[slot].astype(BF16)
    wd = wd_buf[slot].astype(BF16)
    wgb_ref[...] = wg
    wub_ref[...] = wu
    wdb_ref[...] = wd
    xn = xn_ref[...]
    g = jnp.dot(xn, wg, preferred_element_type=F32)
    u = jnp.dot(xn, wu, preferred_element_type=F32)
    a = (g * jax.nn.sigmoid(g) * u).astype(BF16)
    acc_ref[...] += jnp.dot(a, wd, preferred_element_type=F32)

    @pl.when(c == pl.num_programs(0) - 1)
    def _():
        y = x_ref[...] + 0.5 * acc_ref[...]
        y_ref[...] = _rms(y, gfin_ref[...]) if with_final_norm else y


def _ffn_stream(x_parts, gain, wg, wu, wd, *, wo=None, gfin=None):
    m = x_parts[0].shape[0]
    const = lambda a: pl.BlockSpec(a.shape, lambda c: (0,) * a.ndim, pipeline_mode=pl.Buffered(1))
    col_chunk = pl.BlockSpec((D_MODEL, FF_CHUNK), lambda c: (0, c))
    row_chunk = pl.BlockSpec((FF_CHUNK, D_MODEL), lambda c: (c, 0))
    args = list(x_parts) + ([wo] if wo is not None else []) + [gain]
    in_specs = [const(a) for a in args] + [pl.BlockSpec(memory_space=pl.ANY)] * 3
    args += [wg, wu, wd]
    if gfin is not None:
        in_specs.append(const(gfin))
        args.append(gfin)
    return pl.pallas_call(
        functools.partial(_ffn_stream_body, wo is not None, gfin is not None),
        grid=(N_FF_CHUNKS,),
        in_specs=in_specs,
        out_specs=[pl.BlockSpec((m, D_MODEL), lambda c: (0, 0)), col_chunk, col_chunk, row_chunk],
        out_shape=[jax.ShapeDtypeStruct((m, D_MODEL), F32), jax.ShapeDtypeStruct(wg.shape, BF16),
                   jax.ShapeDtypeStruct(wu.shape, BF16), jax.ShapeDtypeStruct(wd.shape, BF16)],
        scratch_shapes=[pltpu.VMEM((m, D_MODEL), F32), pltpu.VMEM((m, D_MODEL), BF16),
                        pltpu.VMEM((m, D_MODEL), F32),
                        pltpu.VMEM((WEIGHT_RING, D_MODEL, FF_CHUNK), F32),
                        pltpu.VMEM((WEIGHT_RING, D_MODEL, FF_CHUNK), F32),
                        pltpu.VMEM((WEIGHT_RING, FF_CHUNK, D_MODEL), F32),
                        pltpu.SemaphoreType.DMA((3, WEIGHT_RING))],
        compiler_params=pltpu.CompilerParams(dimension_semantics=("arbitrary",), vmem_limit_bytes=VMEM_LIMIT),
        name="ffn_stream_out" if wo is not None else "ffn_stream_in",
    )(*args)


def _memkv_body(mem_ref, gmem_ref, w_ref, gk_ref, mk_ref, mv_ref, mkb_ref, mvb_ref):
    kv = jnp.dot(_rms(mem_ref[...], gmem_ref[...]).astype(BF16), w_ref[...], preferred_element_type=F32)
    for g in range(MEM_HEADS // 2):
        sl = slice(g * PAIR, (g + 1) * PAIR)
        mk = _pair_headnorm(kv[:, sl], gk_ref[:, sl])
        mk_ref[sl, :] = mk.T
        mkb_ref[:, sl] = mk.astype(BF16)
    mv = kv[:, MEM_DIM:]
    mv_ref[...] = mv.T
    mvb_ref[...] = mv.astype(BF16)


def _memkv(mem, wts):
    B = mem.shape[0]
    const = lambda shape: pl.BlockSpec(shape, lambda b: (0,) * len(shape))
    blk = lambda cols: pl.BlockSpec((None, N_MEM, cols), lambda b: (b, 0, 0))
    shp = lambda dt: jax.ShapeDtypeStruct((B, N_MEM, MEM_DIM), dt)
    names = ("gmem", "wmem", "gkm")
    return pl.pallas_call(
        _memkv_body,
        grid=(B,),
        in_specs=[blk(D_MODEL)] + [const(wts[n].shape) for n in names],
        out_specs=[blk(MEM_DIM)] * 4,
        out_shape=[shp(F32), shp(F32), shp(BF16), shp(BF16)],
        compiler_params=pltpu.CompilerParams(dimension_semantics=("arbitrary",)),
        name="memkv",
    )(mem, *[wts[n] for n in names])


def _head_masks():
    lane = lax.broadcasted_iota(jnp.int32, (1, LANES), 1)
    lo = lane < HEAD_DIM
    return lo, _idiv(lane, AUG_GROUP)


def _prompt_att_body(tq, tk, qa_ref, ka_hbm, vb_hbm, qm_ref, mk_ref, mv_ref,
                     o_ref, m_ref, acc_ref, ka_buf, vb_buf, sem):
    b, iq = pl.program_id(0), pl.program_id(1)
    nb, nq = pl.num_programs(0), pl.num_programs(1)
    lo, aug_head = _head_masks()
    first_block = b * lax.shift_right_logical(nq * (nq + 1), 1) + lax.shift_right_logical(iq * (iq + 1), 1)

    def block_copies(batch, j, number):
        slot = lax.rem(number, KV_RING)
        off = j * tq
        rows = pl.ds(off if isinstance(off, int) else pl.multiple_of(off, tq), tq)
        return (pltpu.make_async_copy(ka_hbm.at[batch, rows, :], ka_buf.at[slot], sem.at[0, slot]),
                pltpu.make_async_copy(vb_hbm.at[batch, rows, :], vb_buf.at[slot], sem.at[1, slot]))

    def start(batch, j, number):
        for cp in block_copies(batch, j, number):
            cp.start()

    def wait(batch, j, number):
        for cp in block_copies(batch, j, number):
            cp.wait()

    @pl.when(jnp.logical_and(b == 0, iq == 0))
    def _():
        start(b, iq, first_block)

    m_ref[...] = jnp.full(m_ref.shape, NEG, F32)
    acc_ref[...] = jnp.zeros(acc_ref.shape, F32)

    def step(masked, r0, sub, ka_ref, vb_ref):
        rows = tq - r0
        keys = slice(sub * tk, (sub + 1) * tk)
        if masked:
            visible = (lax.broadcasted_iota(jnp.int32, (tk, tk), 1)
                       <= lax.broadcasted_iota(jnp.int32, (tk, tk), 0))
        q_aug = qa_ref[r0:, FOX_DIM:QK_COLS]
        k_aug = ka_ref[keys, FOX_DIM:QK_COLS]
        ones = jnp.ones((tk, LANES), BF16)
        for h in range(FOX_HEADS):
            sl = slice((h // 2) * PAIR, (h // 2 + 1) * PAIR)
            mine = lo if h % 2 == 0 else jnp.logical_not(lo)
            qh = jnp.concatenate([jnp.where(mine, qa_ref[r0:, sl], 0.0).astype(BF16),
                                  jnp.where(aug_head == h, q_aug, 0.0).astype(BF16)], axis=1)
            kh = jnp.concatenate([ka_ref[keys, sl], k_aug], axis=1)
            s = _dot_nt(qh, kh)
            if masked:
                top = jnp.where(visible, s[:tk], NEG)
                s = top if rows == tk else jnp.concatenate([top, s[tk:]], axis=0)
            blocks = [s[:, c * LANES:(c + 1) * LANES] for c in range(tk // LANES)]
            blk_max = functools.reduce(jnp.maximum, blocks)
            m_prev = m_ref[h, r0:, :]
            m_new = jnp.maximum(m_prev, jnp.max(blk_max, axis=-1, keepdims=True))
            alpha = jnp.exp2(m_prev - m_new)
            pexp = jnp.concatenate([jnp.exp2(b - m_new) for b in blocks], axis=1).astype(BF16)
            pv = jnp.dot(pexp, jnp.concatenate([vb_ref[keys, sl], ones], axis=1), preferred_element_type=F32)
            acc_ref[h, r0:, :] = jnp.concatenate([alpha, alpha], axis=1) * acc_ref[h, r0:, :] + pv
            m_ref[h, r0:, :] = m_new

    def unmasked_block(j, carry):
        number = first_block + j

        @pl.when(j + (KV_RING - 1) <= iq)
        def _():
            start(b, j + (KV_RING - 1), number + (KV_RING - 1))
        wait(b, j, number)
        slot = lax.rem(number, KV_RING)
        for sub in range(tq // tk):
            step(False, 0, sub, ka_buf.at[slot], vb_buf.at[slot])
        return carry

    lax.fori_loop(0, iq, unmasked_block, 0)

    last_q = iq == nq - 1
    next_b = jnp.where(last_q, b + 1, b)
    next_iq = jnp.where(last_q, 0, iq + 1)
    next_first = first_block + iq + 1
    has_next = jnp.logical_not(jnp.logical_and(last_q, b == nb - 1))
    for i in range(KV_RING - 1):
        @pl.when(jnp.logical_and(has_next, next_iq >= i))
        def _(i=i):
            start(next_b, i, next_first + i)

    diag = first_block + iq
    wait(b, iq, diag)
    slot = lax.rem(diag, KV_RING)
    for sub in range(tq // tk):
        step(True, sub * tk, sub, ka_buf.at[slot], vb_buf.at[slot])
    for g in range(FOX_HEADS // 2):
        even = acc_ref[2 * g, :, 0:PAIR] / acc_ref[2 * g, :, PAIR:2 * PAIR]
        odd = acc_ref[2 * g + 1, :, 0:PAIR] / acc_ref[2 * g + 1, :, PAIR:2 * PAIR]
        o_ref[:, g * PAIR:(g + 1) * PAIR] = jnp.where(lo, even, odd).astype(BF16)
    for g in range(MEM_HEADS // 2):
        sl = slice(g * PAIR, (g + 1) * PAIR)
        outs = []
        for mine in (lo, jnp.logical_not(lo)):
            qh = jnp.where(mine, qm_ref[:, sl], 0.0).astype(BF16)
            pexp, lsum = _softmax_rows(_dot_nt(qh, mk_ref[:, sl]) * SCALE)
            outs.append(jnp.dot(pexp.astype(BF16), mv_ref[:, sl], preferred_element_type=F32) / lsum)
        o_ref[:, FOX_DIM + g * PAIR:FOX_DIM + (g + 1) * PAIR] = jnp.where(lo, outs[0], outs[1]).astype(BF16)


def _prompt_attention(qa, ka, vb, qm, mkb, mvb, *, tq, tk):
    B, T, _ = qa.shape
    assert tq % tk == 0 and T % tq == 0
    return pl.pallas_call(
        functools.partial(_prompt_att_body, tq, tk),
        grid=(B, T // tq),
        in_specs=[
            pl.BlockSpec((None, tq, QK_COLS), lambda b, i: (b, i, 0)),
            pl.BlockSpec(memory_space=pl.ANY),
            pl.BlockSpec(memory_space=pl.ANY),
            pl.BlockSpec((None, tq, MEM_DIM), lambda b, i: (b, i, 0)),
            pl.BlockSpec((None, N_MEM, MEM_DIM), lambda b, i: (b, 0, 0)),
            pl.BlockSpec((None, N_MEM, MEM_DIM), lambda b, i: (b, 0, 0)),
        ],
        out_specs=pl.BlockSpec((None, tq, ATT_COLS), lambda b, i: (b, i, 0)),
        out_shape=jax.ShapeDtypeStruct((B, T, ATT_COLS), BF16),
        scratch_shapes=[pltpu.VMEM((FOX_HEADS, tq, LANES), F32), pltpu.VMEM((FOX_HEADS, tq, 2 * PAIR), F32),
                        pltpu.VMEM((KV_RING, tq, QK_COLS), BF16), pltpu.VMEM((KV_RING, tq, FOX_DIM), BF16),
                        pltpu.SemaphoreType.DMA((2, KV_RING))],
        compiler_params=pltpu.CompilerParams(
            dimension_semantics=("arbitrary", "arbitrary"), vmem_limit_bytes=VMEM_LIMIT),
        name="prompt_attention",
    )(qa, ka, vb, qm, mkb, mvb)


def _diag_blocks(x, nh, rows, width):
    head = _idiv(lax.broadcasted_iota(jnp.int32, (1, nh * width), 1), width)
    out = None
    for h in range(nh):
        part = jnp.where(head == h, x[h * rows:(h + 1) * rows, :], 0.0)
        out = part if out is None else out + part
    return out


def _sample_att_body(seq, past, per_step, *refs):
    for i in range(per_step):
        _sample_att_one(seq, past, *[r.at[i] for r in refs])


def _sample_att_one(seq, past, qa_ref, ka_ref, vb_ref, qm_ref, kt_ref, vt_ref, lt_ref, mkt_ref, mvt_ref, o_ref):
    blk = LANES
    nblk = past // blk
    lt = lt_ref[...]
    x = jnp.concatenate([lt[:, b * blk:(b + 1) * blk] for b in range(nblk)], axis=0)
    n = x.shape[0]
    parts = jnp.concatenate(_split3(x), axis=0).astype(BF16)
    src = lax.broadcasted_iota(jnp.int32, (blk, blk), 0)
    dst = lax.broadcasted_iota(jnp.int32, (blk, blk), 1)
    later = jnp.where(src > dst, 1.0, 0.0).astype(BF16)
    loc = jnp.dot(parts, later, preferred_element_type=F32)
    tot = jnp.dot(parts, jnp.ones((blk, blk), BF16), preferred_element_type=F32)
    loc = loc[0:n] + loc[n:2 * n] + loc[2 * n:3 * n]
    tot = tot[0:n] + tot[n:2 * n] + tot[2 * n:3 * n]
    running = jnp.zeros((FOX_HEADS, blk), F32)
    suffix = [None] * nblk
    for b in reversed(range(nblk)):
        rows = slice(b * FOX_HEADS, (b + 1) * FOX_HEADS)
        suffix[b] = loc[rows] + running
        running = running + tot[rows]
    rt = jnp.concatenate(suffix, axis=1) * LOG2E
    bias = jnp.concatenate([jnp.broadcast_to(rt[h:h + 1, :], (seq, past)) for h in range(FOX_HEADS)], axis=0)

    nrow = FOX_HEADS * seq
    lane = lax.broadcasted_iota(jnp.int32, (nrow, QK_COLS), 1)
    lane_head = jnp.where(lane < FOX_DIM, _idiv(lane, HEAD_DIM), _idiv(lane - FOX_DIM, AUG_GROUP))
    row_head = _idiv(lax.broadcasted_iota(jnp.int32, (nrow, QK_COLS), 0), seq)
    qbd = jnp.where(lane_head == row_head, jnp.concatenate([qa_ref[...]] * FOX_HEADS, axis=0), 0.0).astype(BF16)
    aug_row = _imod(lax.broadcasted_iota(jnp.int32, (AUG, past), 0), AUG_GROUP)
    kt = jnp.concatenate([kt_ref[...].astype(BF16), jnp.where(aug_row < N_SPLIT, 1.0, 0.0).astype(BF16)], axis=0)
    s_past = jnp.dot(qbd, kt, preferred_element_type=F32) + bias
    s_new = _dot_nt(qbd, ka_ref[...])
    qi = _imod(lax.broadcasted_iota(jnp.int32, (nrow, seq), 0), seq)
    kj = lax.broadcasted_iota(jnp.int32, (nrow, seq), 1)
    s_new = jnp.where(kj <= qi, s_new, NEG)
    m = jnp.maximum(jnp.max(s_past, axis=-1, keepdims=True), jnp.max(s_new, axis=-1, keepdims=True))
    p_past = jnp.exp2(s_past - m)
    p_new = jnp.exp2(s_new - m)
    lsum = jnp.sum(p_past, axis=-1, keepdims=True) + jnp.sum(p_new, axis=-1, keepdims=True)
    o = (_dot_nt(p_past.astype(BF16), vt_ref[...].astype(BF16))
         + jnp.dot(p_new.astype(BF16), vb_ref[...], preferred_element_type=F32)) / lsum
    o_ref[:, 0:FOX_DIM] = _diag_blocks(o, FOX_HEADS, seq, HEAD_DIM).astype(BF16)

    nrow_m = MEM_HEADS * seq
    lane_m = _idiv(lax.broadcasted_iota(jnp.int32, (nrow_m, MEM_DIM), 1), HEAD_DIM)
    row_m = _idiv(lax.broadcasted_iota(jnp.int32, (nrow_m, MEM_DIM), 0), seq)
    qbd_m = jnp.where(lane_m == row_m, jnp.concatenate([qm_ref[...]] * MEM_HEADS, axis=0), 0.0).astype(BF16)
    pm, lm = _softmax_rows(jnp.dot(qbd_m, mkt_ref[...].astype(BF16), preferred_element_type=F32) * SCALE)
    om = _dot_nt(pm.astype(BF16), mvt_ref[...].astype(BF16)) / lm
    o_ref[:, FOX_DIM:ATT_COLS] = _diag_blocks(om, MEM_HEADS, seq, HEAD_DIM).astype(BF16)


def _sample_attention(qa, ka, vb, qm, pk, pv, plf, mk, mv):
    nbatch, seq, _ = qa.shape
    past = pk.shape[2]
    per_step = SAMPLE_SEQS_PER_STEP
    assert nbatch % per_step == 0
    blk = lambda rows, cols: pl.BlockSpec((per_step, rows, cols), lambda b: (b, 0, 0))
    return pl.pallas_call(
        functools.partial(_sample_att_body, seq, past, per_step),
        grid=(nbatch // per_step,),
        in_specs=[blk(seq, QK_COLS), blk(seq, QK_COLS), blk(seq, FOX_DIM), blk(seq, MEM_DIM),
                  blk(FOX_DIM, past), blk(FOX_DIM, past), blk(FOX_HEADS, past),
                  blk(MEM_DIM, N_MEM), blk(MEM_DIM, N_MEM)],
        out_specs=blk(seq, ATT_COLS),
        out_shape=jax.ShapeDtypeStruct((nbatch, seq, ATT_COLS), BF16),
        compiler_params=pltpu.CompilerParams(
            dimension_semantics=("arbitrary",), vmem_limit_bytes=VMEM_LIMIT),
        name="sample_attention",
    )(qa, ka, vb, qm, pk, pv, plf, mk, mv)


def _prep_weights(norm_ffn1, norm_mix, w_in, b_forget, conv_w, conv_b,
                  q_norm_fox, k_norm_fox, q_norm_mem, k_norm_mem, norm_mem, w_mem_kv, w_out,
                  norm_ffn2, norm_final):
    row = lambda v: v.reshape(1, -1).astype(F32)
    return {
        "g1": row(norm_ffn1), "gmix": row(norm_mix), "win32t": w_in.T,
        "conv_w": conv_w.astype(F32), "conv_b": row(conv_b),
        "gq": row(jnp.tile(q_norm_fox, FOX_HEADS)) * (SCALE * LOG2E), "gk": row(jnp.tile(k_norm_fox, FOX_HEADS)),
        "gqm": row(jnp.tile(q_norm_mem, MEM_HEADS)),
        "bf_rep": row(jnp.repeat(b_forget, AUG_GROUP)),
        "bf": jnp.pad(row(b_forget), ((0, 0), (0, LANES - FOX_HEADS))),
        "gmem": row(norm_mem), "wmem": w_mem_kv.astype(BF16), "gkm": row(jnp.tile(k_norm_mem, MEM_HEADS)),
        "wo": w_out.astype(BF16), "g2": row(norm_ffn2), "gfin": row(norm_final),
    }


def kernel(x_prompt, x_sample, cache_fox_k, cache_fox_v, cache_fox_logf, state_conv, cache_mem_k, cache_mem_v, mem_prompt, norm_ffn1, w_ffn1_gate, w_ffn1_up, w_ffn1_down, norm_mix, w_in, b_forget, conv_w, conv_b, q_norm_fox, k_norm_fox, q_norm_mem, k_norm_mem, norm_mem, w_mem_kv, w_out, norm_ffn2, w_ffn2_gate, w_ffn2_up, w_ffn2_down, norm_final):
    depth = w_in.shape[0]
    assert depth == 1, "single-layer step"
    B, T, _ = x_prompt.shape
    nbs, seq_s, _ = x_sample.shape
    past = cache_fox_k.shape[2]
    l = 0
    wts = _prep_weights(norm_ffn1[l], norm_mix[l], w_in[l], b_forget[l], conv_w[l], conv_b[l], q_norm_fox[l],
                        k_norm_fox[l], q_norm_mem[l], k_norm_mem[l], norm_mem[l], w_mem_kv[l], w_out[l],
                        norm_ffn2[l], norm_final[l])
    tile = TOKEN_TILE
    n_s = nbs * seq_s

    x1s, wts["wg1"], wts["wu1"], wts["wd1"] = _ffn_stream(
        (x_sample.reshape(n_s, D_MODEL),), wts["g1"], w_ffn1_gate[l], w_ffn1_up[l], w_ffn1_down[l])
    ycs, qas, kas, vbs, kfs, vfs, lfs, qms, ncs, wts["win"], wts["wtail"] = _stage1(
        x1s.reshape(1, n_s, D_MODEL), state_conv[l], wts, nb=tile // seq_s, seq=seq_s, carry=False)
    per_seq = lambda a: a.reshape(nbs, seq_s, a.shape[-1])
    feat_major = lambda a: a.reshape(a.shape[0], a.shape[1], -1).transpose(0, 2, 1)
    yas = _sample_attention(
        per_seq(qas), per_seq(kas), per_seq(vbs), per_seq(qms),
        feat_major(cache_fox_k[l]), feat_major(cache_fox_v[l]), feat_major(cache_fox_logf[l]),
        feat_major(cache_mem_k[l]), feat_major(cache_mem_v[l]))
    y_sample, wts["wg2"], wts["wu2"], wts["wd2"] = _ffn_stream(
        (x1s, ycs.reshape(n_s, CONV_DIM), yas.reshape(n_s, ATT_COLS)), wts["g2"],
        w_ffn2_gate[l], w_ffn2_up[l], w_ffn2_down[l], wo=wts["wo"], gfin=wts["gfin"])

    mk_p, mv_p, mkb, mvb = _memkv(mem_prompt, wts)
    x1, yc, qa, ka, vb, kf, vf, lf, qm, nc = _stage1(x_prompt, None, wts, nb=1, seq=tile, carry=True)
    ya = _prompt_attention(qa, ka, vb, qm, mkb, mvb, tq=ATT_QUERY_TILE, tk=ATT_KEY_TILE)
    y_prompt = _stage3(x1, yc, ya, wts, tm=tile)

    heads = lambda a, b, t, nh: a.reshape(1, b, t, nh, HEAD_DIM)
    token_major = lambda a, nh: a.reshape(a.shape[0], nh, -1, a.shape[2]).transpose(0, 3, 1, 2)[None]
    return (y_prompt, y_sample.reshape(nbs, seq_s, D_MODEL),
            token_major(kf, FOX_HEADS), token_major(vf, FOX_HEADS), lf.transpose(0, 2, 1)[None],
            nc.reshape(1, B, CONV_K - 1, CONV_DIM),
            token_major(mk_p, MEM_HEADS), token_major(mv_p, MEM_HEADS),
            heads(kfs, nbs, seq_s, FOX_HEADS), heads(vfs, nbs, seq_s, FOX_HEADS),
            lfs.reshape(1, nbs, seq_s, FOX_HEADS), ncs.reshape(1, nbs, CONV_K - 1, CONV_DIM))
```

```python
import functools

import jax
import jax.numpy as jnp
from jax import lax
from jax.experimental import pallas as pl
from jax.experimental.pallas import tpu as pltpu

F32 = jnp.float32
BF16 = jnp.bfloat16

D_MODEL = 1024
HEAD_DIM = 64
CONV_DIM = 256
CONV_K = 3
FOX_HEADS = 8
FOX_DIM = FOX_HEADS * HEAD_DIM
MEM_HEADS = 4
MEM_DIM = MEM_HEADS * HEAD_DIM
N_MEM = 256
D_FF = 2816
EPS = 1e-6

LANES = 128
PAIR = 2 * HEAD_DIM
FF_CHUNK = 256
N_FF_CHUNKS = D_FF // FF_CHUNK
AUG = LANES
AUG_GROUP = AUG // FOX_HEADS
N_SPLIT = 3
QK_COLS = FOX_DIM + AUG
ATT_COLS = FOX_DIM + MEM_DIM
COL_Q = 3 * CONV_DIM
COL_K = COL_Q + FOX_DIM
COL_V = COL_K + FOX_DIM
COL_F = COL_V + FOX_DIM
COL_QM = COL_F + FOX_HEADS
TAIL_COLS = MEM_DIM + AUG + LANES
NEG = -1e30
SCALE = HEAD_DIM ** -0.5
LOG2E = 1.4426950408889634
VMEM_LIMIT = 56 * 1024 * 1024
TOKEN_TILE = 512
ATT_KEY_TILE = 512
ATT_QUERY_TILE = 1024
SAMPLE_SEQS_PER_STEP = 4
WEIGHT_RING = 3
KV_RING = 3


def _idiv(x, d):
    if d & (d - 1) == 0:
        return lax.shift_right_logical(x, d.bit_length() - 1)
    return x // d


def _imod(x, d):
    if d & (d - 1) == 0:
        return x & (d - 1)
    return x % d


def _rms(x, g):
    ms = jnp.mean(x * x, axis=-1, keepdims=True)
    return x * lax.rsqrt(ms + EPS) * g


def _pair_headnorm(x, g):
    lo = lax.broadcasted_iota(jnp.int32, (1, PAIR), 1) < HEAD_DIM
    x2 = x * x
    s_lo = jnp.sum(jnp.where(lo, x2, 0.0), axis=-1, keepdims=True)
    s_hi = jnp.sum(jnp.where(lo, 0.0, x2), axis=-1, keepdims=True)
    r = jnp.where(lo, lax.rsqrt(s_lo * (1.0 / HEAD_DIM) + EPS),
                  lax.rsqrt(s_hi * (1.0 / HEAD_DIM) + EPS))
    return x * r * g


def _store_cols(ref, sl, x, transposed):
    if transposed:
        ref[sl, :] = x.T
    else:
        ref[:, sl] = x


def _split3(x):
    hi = x.astype(BF16).astype(F32)
    r = x - hi
    mid = r.astype(BF16).astype(F32)
    lo = (r - mid).astype(BF16).astype(F32)
    return hi, mid, lo


def _ffn(xn, wg_ref, wu_ref, wd_ref):
    acc = None
    for c in range(N_FF_CHUNKS):
        sl = slice(c * FF_CHUNK, (c + 1) * FF_CHUNK)
        g = jnp.dot(xn, wg_ref[:, sl], preferred_element_type=F32)
        u = jnp.dot(xn, wu_ref[:, sl], preferred_element_type=F32)
        a = (g * jax.nn.sigmoid(g) * u).astype(BF16)
        d = jnp.dot(a, wd_ref[sl, :], preferred_element_type=F32)
        acc = d if acc is None else acc + d
    return acc


def _log_sigmoid(x):
    return jnp.minimum(x, 0.0) - jnp.log1p(jnp.exp(-jnp.abs(x)))


def _softmax_rows(s):
    m = jnp.max(s, axis=-1, keepdims=True)
    p = jnp.exp(s - m)
    return p, jnp.sum(p, axis=-1, keepdims=True)


def _dot_nt(a, b):
    return lax.dot_general(a, b, (((1,), (1,)), ((), ())), preferred_element_type=F32)


_STAGE1_SHARED_IN = ("gmix", "conv_w", "conv_b", "gq", "gk", "gqm", "bf_rep", "bf")
_STAGE1_SHARED_OUT = ("yc", "qa", "ka", "vb", "kf", "vf", "lf", "qm", "nc")
_STAGE1_IN = {True: ("x", "g1", "wg1", "wu1", "wd1", "win", "wtail") + _STAGE1_SHARED_IN,
              False: ("x1", "state", "win32t") + _STAGE1_SHARED_IN}
_STAGE1_OUT = {True: ("x1",) + _STAGE1_SHARED_OUT, False: _STAGE1_SHARED_OUT + ("win", "wtail")}


def _stage1_body(nb, seq, carry, *refs):
    tm = nb * seq
    names = _STAGE1_IN[carry] + _STAGE1_OUT[carry]
    r = dict(zip(names, refs[:len(names)]))
    cs_ref, cum_ref = refs[len(names):]
    gmix_ref, cw_ref, cb_ref, gq_ref, gk_ref, gqm_ref, bfr_ref, bf_ref = (r[n] for n in _STAGE1_SHARED_IN)
    yc_ref, qa_ref, ka_ref, vb_ref, kf_ref, vf_ref, lf_ref, qm_ref, nc_ref = (r[n] for n in _STAGE1_SHARED_OUT)
    win_ref, wtail_ref = r["win"], r["wtail"]
    if carry:
        x = r["x"][...]
        y = _ffn(_rms(x, r["g1"][...]).astype(BF16), r["wg1"], r["wu1"], r["wd1"])
        x1 = x + 0.5 * y
        r["x1"][...] = x1
    else:
        x1 = r["x1"][...]
        st_ref = r["state"]

        @pl.when(pl.program_id(1) == 0)
        def _():
            w_t = r["win32t"]
            win_ref[...] = w_t[0:COL_F, :].T.astype(BF16)
            f_rows = w_t[COL_F:COL_QM, :]
            f_rep = jnp.concatenate([jnp.broadcast_to(f_rows[hd:hd + 1, :], (AUG_GROUP, D_MODEL))
                                     for hd in range(FOX_HEADS)], axis=0)
            tail_t = jnp.concatenate([w_t[COL_QM:COL_QM + MEM_DIM, :], f_rep, f_rows,
                                      jnp.zeros((LANES - FOX_HEADS, D_MODEL), F32)], axis=0)
            wtail_ref[...] = tail_t.T.astype(BF16)
    h = _rms(x1, gmix_ref[...]).astype(BF16)

    ucb = jnp.dot(h, win_ref[:, 0:COL_Q], preferred_element_type=F32)
    ci = ucb[:, CONV_DIM:2 * CONV_DIM] * ucb[:, 0:CONV_DIM]
    if carry:
        @pl.when(pl.program_id(1) == 0)
        def _():
            cs_ref[:, 0:8, :] = jnp.zeros((nb, 8, CONV_DIM), F32)
    else:
        cs_ref[:, 8 - (CONV_K - 1):8, :] = st_ref[...]
    cs_ref[:, 8:8 + seq, :] = ci.reshape(nb, seq, CONV_DIM)
    conv = cb_ref[...] + cw_ref[CONV_K - 1:CONV_K, :] * ci
    for i in range(CONV_K - 1):
        shifted = cs_ref[:, 8 - (CONV_K - 1) + i:8 - (CONV_K - 1) + i + seq, :]
        conv = conv + cw_ref[i:i + 1, :] * shifted.reshape(tm, CONV_DIM)
    yc_ref[...] = (ucb[:, 2 * CONV_DIM:3 * CONV_DIM] * conv).astype(BF16)
    tail = cs_ref[:, seq:seq + 8, :]
    nc_ref[...] = tail[:, 8 - (CONV_K - 1):, :].reshape(nc_ref.shape)
    if carry:
        cs_ref[:, 0:8, :] = tail

    for name, col, g_ref, out_ref in (("q", COL_Q, gq_ref, qa_ref), ("k", COL_K, gk_ref, ka_ref)):
        pr = jnp.dot(h, win_ref[:, col:col + FOX_DIM], preferred_element_type=F32)
        for g in range(FOX_HEADS // 2):
            sl = slice(g * PAIR, (g + 1) * PAIR)
            xn = _pair_headnorm(pr[:, sl], g_ref[:, sl])
            if name == "k":
                _store_cols(kf_ref, sl, xn, carry)
            out_ref[:, sl] = xn.astype(BF16)
    v = jnp.dot(h, win_ref[:, COL_V:COL_V + FOX_DIM], preferred_element_type=F32)
    for g in range(FOX_HEADS // 2):
        sl = slice(g * PAIR, (g + 1) * PAIR)
        _store_cols(vf_ref, sl, v[:, sl], carry)
    vb_ref[...] = v.astype(BF16)

    tail_cols = jnp.dot(h, wtail_ref[...], preferred_element_type=F32)
    for g in range(MEM_HEADS // 2):
        sl = slice(g * PAIR, (g + 1) * PAIR)
        qm_ref[:, sl] = _pair_headnorm(tail_cols[:, sl], gqm_ref[:, sl]).astype(BF16)
    fl = tail_cols[:, MEM_DIM:]
    lf = _log_sigmoid(fl[:, AUG:2 * AUG] + bf_ref[...])
    if carry:
        lf_ref[...] = lf.T[0:FOX_HEADS, :]
    else:
        lf_ref[...] = lf[:, 0:FOX_HEADS]
    logf = _log_sigmoid(fl[:, 0:AUG] + bfr_ref[...])

    row = lax.broadcasted_iota(jnp.int32, (tm, tm), 0)
    col = lax.broadcasted_iota(jnp.int32, (tm, tm), 1)
    tri = col <= row
    if nb > 1:
        tri = jnp.logical_and(tri, _idiv(row, seq) == _idiv(col, seq))
    tri = jnp.where(tri, 1.0, 0.0).astype(BF16)
    hi, mid, lo = _split3(logf)
    parts = jnp.concatenate([hi.astype(BF16), mid.astype(BF16), lo.astype(BF16)], axis=1)
    cs = jnp.dot(tri, parts, preferred_element_type=F32)
    cum = cs[:, 0:AUG] + cs[:, AUG:2 * AUG] + cs[:, 2 * AUG:3 * AUG]
    if carry:
        @pl.when(pl.program_id(1) == 0)
        def _():
            cum_ref[...] = jnp.zeros(cum_ref.shape, F32)
        cum = cum + cum_ref[0:1, :]
        cum_ref[0:1, :] = cum[tm - 1:tm, :]

    j = _imod(lax.broadcasted_iota(jnp.int32, (1, AUG), 1), AUG_GROUP)
    terms = _split3(cum * LOG2E)
    aq = jnp.where(j < 2 * N_SPLIT, 1.0, 0.0)
    ak = jnp.where(j < N_SPLIT, 1.0, 0.0)
    for i, term in enumerate(terms):
        aq = jnp.where(j == i, term, aq)
        ak = jnp.where(j == N_SPLIT + i, -term, ak)
    qa_ref[:, FOX_DIM:QK_COLS] = aq.astype(BF16)
    ka_ref[:, FOX_DIM:QK_COLS] = ak.astype(BF16)


def _stage1(x3, state, wts, *, nb, seq, carry):
    G, T, _ = x3.shape
    tm = nb * seq
    nt = T // tm
    const = lambda shape: pl.BlockSpec(shape, lambda g, t: (0,) * len(shape), pipeline_mode=pl.Buffered(1))
    tok = lambda cols: pl.BlockSpec((None, tm, cols), lambda g, t: (g, t, 0))
    if carry:
        feat = lambda rows: pl.BlockSpec((None, rows, tm), lambda g, t: (g, 0, t))
        feat_shape = lambda rows: jax.ShapeDtypeStruct((G, rows, T), F32)
    else:
        feat = lambda rows: pl.BlockSpec((None, tm, rows), lambda g, t: (g, t, 0))
        feat_shape = lambda rows: jax.ShapeDtypeStruct((G, T, rows), F32)
    shp = lambda cols, dt: jax.ShapeDtypeStruct((G, T, cols), dt)
    if carry:
        nc_spec = pl.BlockSpec((None, CONV_K - 1, CONV_DIM), lambda g, t: (g, 0, 0))
        nc_shape = jax.ShapeDtypeStruct((G, CONV_K - 1, CONV_DIM), F32)
    else:
        nc_spec = pl.BlockSpec((nb, CONV_K - 1, CONV_DIM), lambda g, t: (t, 0, 0))
        nc_shape = jax.ShapeDtypeStruct((nt * nb, CONV_K - 1, CONV_DIM), F32)
    tiled_in = {"x": (tok(D_MODEL), x3), "x1": (tok(D_MODEL), x3), "state": (nc_spec, state)}
    in_specs, args = [], []
    for n in _STAGE1_IN[carry]:
        spec, arg = tiled_in[n] if n in tiled_in else (const(wts[n].shape), wts[n])
        in_specs.append(spec)
        args.append(arg)
    outs = {"x1": (tok(D_MODEL), shp(D_MODEL, F32)), "yc": (tok(CONV_DIM), shp(CONV_DIM, BF16)),
            "qa": (tok(QK_COLS), shp(QK_COLS, BF16)), "ka": (tok(QK_COLS), shp(QK_COLS, BF16)),
            "vb": (tok(FOX_DIM), shp(FOX_DIM, BF16)), "kf": (feat(FOX_DIM), feat_shape(FOX_DIM)),
            "vf": (feat(FOX_DIM), feat_shape(FOX_DIM)), "lf": (feat(FOX_HEADS), feat_shape(FOX_HEADS)),
            "qm": (tok(MEM_DIM), shp(MEM_DIM, BF16)), "nc": (nc_spec, nc_shape)}
    if not carry:
        for n, w_shape in (("win", (D_MODEL, COL_F)), ("wtail", (D_MODEL, TAIL_COLS))):
            outs[n] = (pl.BlockSpec(w_shape, lambda g, t: (0, 0)), jax.ShapeDtypeStruct(w_shape, BF16))
    out_specs = [outs[n][0] for n in _STAGE1_OUT[carry]]
    out_shape = [outs[n][1] for n in _STAGE1_OUT[carry]]
    return pl.pallas_call(
        functools.partial(_stage1_body, nb, seq, carry),
        grid=(G, nt),
        in_specs=in_specs,
        out_specs=out_specs,
        out_shape=out_shape,
        scratch_shapes=[pltpu.VMEM((nb, seq + 8, CONV_DIM), F32), pltpu.VMEM((8, AUG), F32)],
        compiler_params=pltpu.CompilerParams(
            dimension_semantics=("arbitrary", "arbitrary"), vmem_limit_bytes=VMEM_LIMIT),
        name="stage1_carry" if carry else "stage1_batched",
    )(*args)


def _stage3_body(x1_ref, yc_ref, ya_ref, wo_ref, g2_ref, wg_ref, wu_ref, wd_ref, gfin_ref, y_ref):
    x2 = (x1_ref[...]
          + jnp.dot(yc_ref[...], wo_ref[0:CONV_DIM, :], preferred_element_type=F32)
          + jnp.dot(ya_ref[...], wo_ref[CONV_DIM:, :], preferred_element_type=F32))
    y = _ffn(_rms(x2, g2_ref[...]).astype(BF16), wg_ref, wu_ref, wd_ref)
    y_ref[...] = _rms(x2 + 0.5 * y, gfin_ref[...])


def _stage3(x1, yc, ya, wts, *, tm):
    G, T, _ = x1.shape
    const = lambda shape: pl.BlockSpec(shape, lambda g, t: (0,) * len(shape), pipeline_mode=pl.Buffered(1))
    tok = lambda cols: pl.BlockSpec((None, tm, cols), lambda g, t: (g, t, 0))
    names = ("wo", "g2", "wg2", "wu2", "wd2", "gfin")
    return pl.pallas_call(
        _stage3_body,
        grid=(G, T // tm),
        in_specs=[tok(D_MODEL), tok(CONV_DIM), tok(ATT_COLS)] + [const(wts[n].shape) for n in names],
        out_specs=tok(D_MODEL),
        out_shape=jax.ShapeDtypeStruct((G, T, D_MODEL), F32),
        compiler_params=pltpu.CompilerParams(
            dimension_semantics=("arbitrary", "arbitrary"), vmem_limit_bytes=VMEM_LIMIT),
        name="stage3",
    )(x1, yc, ya, *[wts[n] for n in names])


def _ffn_stream_body(with_out_proj, with_final_norm, *refs):
    refs = list(refs)
    n_x = 4 if with_out_proj else 1
    x_refs, refs = refs[:n_x], refs[n_x:]
    g_ref, wg_hbm, wu_hbm, wd_hbm = refs[:4]
    refs = refs[4:]
    gfin_ref = refs.pop(0) if with_final_norm else None
    y_ref, wgb_ref, wub_ref, wdb_ref, x_ref, xn_ref, acc_ref, wg_buf, wu_buf, wd_buf, sem = refs
    c = pl.program_id(0)

    def chunk_copies(k):
        slot = lax.rem(k, WEIGHT_RING)
        off = pl.multiple_of(k * FF_CHUNK, FF_CHUNK)
        return (pltpu.make_async_copy(wg_hbm.at[:, pl.ds(off, FF_CHUNK)], wg_buf.at[slot], sem.at[0, slot]),
                pltpu.make_async_copy(wu_hbm.at[:, pl.ds(off, FF_CHUNK)], wu_buf.at[slot], sem.at[1, slot]),
                pltpu.make_async_copy(wd_hbm.at[pl.ds(off, FF_CHUNK), :], wd_buf.at[slot], sem.at[2, slot]))

    @pl.when(c == 0)
    def _():
        for k in range(WEIGHT_RING - 1):
            for cp in chunk_copies(jnp.int32(k)):
                cp.start()

    @pl.when(c + (WEIGHT_RING - 1) < pl.num_programs(0))
    def _():
        for cp in chunk_copies(c + (WEIGHT_RING - 1)):
            cp.start()

    @pl.when(c == 0)
    def _():
        if with_out_proj:
            x1_ref, yc_ref, ya_ref, wo_ref = x_refs
            x = (x1_ref[...]
                 + jnp.dot(yc_ref[...], wo_ref[0:CONV_DIM, :], preferred_element_type=F32)
                 + jnp.dot(ya_ref[...], wo_ref[CONV_DIM:, :], preferred_element_type=F32))
        else:
            x = x_refs[0][...]
        x_ref[...] = x
        xn_ref[...] = _rms(x, g_ref[...]).astype(BF16)
        acc_ref[...] = jnp.zeros(acc_ref.shape, F32)

    for cp in chunk_copies(c):
        cp.wait()
    slot = lax.rem(c, WEIGHT_RING)
    wg = wg_buf[slot].astype(BF16)
    wu = wu_buf[slot].astype(BF16)
    wd = wd_buf[slot].astype(BF16)
    wgb_ref[...] = wg
    wub_ref[...] = wu
    wdb_ref[...] = wd
    xn = xn_ref[...]
    g = jnp.dot(xn, wg, preferred_element_type=F32)
    u = jnp.dot(xn, wu, preferred_element_type=F32)
    a = (g * jax.nn.sigmoid(g) * u).astype(BF16)
    acc_ref[...] += jnp.dot(a, wd, preferred_element_type=F32)

    @pl.when(c == pl.num_programs(0) - 1)
    def _():
        y = x_ref[...] + 0.5 * acc_ref[...]
        y_ref[...] = _rms(y, gfin_ref[...]) if with_final_norm else y


def _ffn_stream(x_parts, gain, wg, wu, wd, *, wo=None, gfin=None):
    m = x_parts[0].shape[0]
    const = lambda a: pl.BlockSpec(a.shape, lambda c: (0,) * a.ndim, pipeline_mode=pl.Buffered(1))
    col_chunk = pl.BlockSpec((D_MODEL, FF_CHUNK), lambda c: (0, c))
    row_chunk = pl.BlockSpec((FF_CHUNK, D_MODEL), lambda c: (c, 0))
    args = list(x_parts) + ([wo] if wo is not None else []) + [gain]
    in_specs = [const(a) for a in args] + [pl.BlockSpec(memory_space=pl.ANY)] * 3
    args += [wg, wu, wd]
    if gfin is not None:
        in_specs.append(const(gfin))
        args.append(gfin)
    return pl.pallas_call(
        functools.partial(_ffn_stream_body, wo is not None, gfin is not None),
        grid=(N_FF_CHUNKS,),
        in_specs=in_specs,
        out_specs=[pl.BlockSpec((m, D_MODEL), lambda c: (0, 0)), col_chunk, col_chunk, row_chunk],
        out_shape=[jax.ShapeDtypeStruct((m, D_MODEL), F32), jax.ShapeDtypeStruct(wg.shape, BF16),
                   jax.ShapeDtypeStruct(wu.shape, BF16), jax.ShapeDtypeStruct(wd.shape, BF16)],
        scratch_shapes=[pltpu.VMEM((m, D_MODEL), F32), pltpu.VMEM((m, D_MODEL), BF16),
                        pltpu.VMEM((m, D_MODEL), F32),
                        pltpu.VMEM((WEIGHT_RING, D_MODEL, FF_CHUNK), F32),
                        pltpu.VMEM((WEIGHT_RING, D_MODEL, FF_CHUNK), F32),
                        pltpu.VMEM((WEIGHT_RING, FF_CHUNK, D_MODEL), F32),
                        pltpu.SemaphoreType.DMA((3, WEIGHT_RING))],
        compiler_params=pltpu.CompilerParams(dimension_semantics=("arbitrary",), vmem_limit_bytes=VMEM_LIMIT),
        name="ffn_stream_out" if wo is not None else "ffn_stream_in",
    )(*args)


def _memkv_body(mem_ref, gmem_ref, w_ref, gk_ref, mk_ref, mv_ref, mkb_ref, mvb_ref):
    kv = jnp.dot(_rms(mem_ref[...], gmem_ref[...]).astype(BF16), w_ref[...], preferred_element_type=F32)
    for g in range(MEM_HEADS // 2):
        sl = slice(g * PAIR, (g + 1) * PAIR)
        mk = _pair_headnorm(kv[:, sl], gk_ref[:, sl])
        mk_ref[sl, :] = mk.T
        mkb_ref[:, sl] = mk.astype(BF16)
    mv = kv[:, MEM_DIM:]
    mv_ref[...] = mv.T
    mvb_ref[...] = mv.astype(BF16)


def _memkv(mem, wts):
    B = mem.shape[0]
    const = lambda shape: pl.BlockSpec(shape, lambda b: (0,) * len(shape))
    blk = lambda cols: pl.BlockSpec((None, N_MEM, cols), lambda b: (b, 0, 0))
    shp = lambda dt: jax.ShapeDtypeStruct((B, N_MEM, MEM_DIM), dt)
    names = ("gmem", "wmem", "gkm")
    return pl.pallas_call(
        _memkv_body,
        grid=(B,),
        in_specs=[blk(D_MODEL)] + [const(wts[n].shape) for n in names],
        out_specs=[blk(MEM_DIM)] * 4,
        out_shape=[shp(F32), shp(F32), shp(BF16), shp(BF16)],
        compiler_params=pltpu.CompilerParams(dimension_semantics=("arbitrary",)),
        name="memkv",
    )(mem, *[wts[n] for n in names])


def _head_masks():
    lane = lax.broadcasted_iota(jnp.int32, (1, LANES), 1)
    lo = lane < HEAD_DIM
    return lo, _idiv(lane, AUG_GROUP)


def _prompt_att_body(tq, tk, qa_ref, ka_hbm, vb_hbm, qm_ref, mk_ref, mv_ref,
                     o_ref, m_ref, acc_ref, ka_buf, vb_buf, sem):
    b, iq = pl.program_id(0), pl.program_id(1)
    nb, nq = pl.num_programs(0), pl.num_programs(1)
    lo, aug_head = _head_masks()
    first_block = b * lax.shift_right_logical(nq * (nq + 1), 1) + lax.shift_right_logical(iq * (iq + 1), 1)

    def block_copies(batch, j, number):
        slot = lax.rem(number, KV_RING)
        off = j * tq
        rows = pl.ds(off if isinstance(off, int) else pl.multiple_of(off, tq), tq)
        return (pltpu.make_async_copy(ka_hbm.at[batch, rows, :], ka_buf.at[slot], sem.at[0, slot]),
                pltpu.make_async_copy(vb_hbm.at[batch, rows, :], vb_buf.at[slot], sem.at[1, slot]))

    def start(batch, j, number):
        for cp in block_copies(batch, j, number):
            cp.start()

    def wait(batch, j, number):
        for cp in block_copies(batch, j, number):
            cp.wait()

    @pl.when(jnp.logical_and(b == 0, iq == 0))
    def _():
        start(b, iq, first_block)
        acc_ref[...] = jnp.zeros(acc_ref.shape, F32)

    m_ref[...] = jnp.full(m_ref.shape, NEG, F32)

    def step(masked, r0, sub, ka_ref, vb_ref):
        rows = tq - r0
        keys = slice(sub * tk, (sub + 1) * tk)
        if masked:
            visible = (lax.broadcasted_iota(jnp.int32, (tk, tk), 1)
                       <= lax.broadcasted_iota(jnp.int32, (tk, tk), 0))
        q_aug = qa_ref[r0:, FOX_DIM:QK_COLS]
        k_aug = ka_ref[keys, FOX_DIM:QK_COLS]
        ones = jnp.ones((tk, LANES), BF16)
        for h in range(FOX_HEADS):
            sl = slice((h // 2) * PAIR, (h // 2 + 1) * PAIR)
            mine = lo if h % 2 == 0 else jnp.logical_not(lo)
            qh = jnp.concatenate([jnp.where(mine, qa_ref[r0:, sl], 0.0).astype(BF16),
                                  jnp.where(aug_head == h, q_aug, 0.0).astype(BF16)], axis=1)
            kh = jnp.concatenate([ka_ref[keys, sl], k_aug], axis=1)
            s = _dot_nt(qh, kh)
            if masked:
                top = jnp.where(visible, s[:tk], NEG)
                s = top if rows == tk else jnp.concatenate([top, s[tk:]], axis=0)
            blocks = [s[:, c * LANES:(c + 1) * LANES] for c in range(tk // LANES)]
            blk_max = functools.reduce(jnp.maximum, blocks)
            m_prev = m_ref[h, r0:, :]
            m_new = jnp.maximum(m_prev, jnp.max(blk_max, axis=-1, keepdims=True))
            alpha = jnp.exp2(m_prev - m_new)
            pexp = jnp.concatenate([jnp.exp2(b - m_new) for b in blocks], axis=1).astype(BF16)
            pv = jnp.dot(pexp, jnp.concatenate([vb_ref[keys, sl], ones], axis=1), preferred_element_type=F32)
            acc_ref[h, r0:, :] = jnp.concatenate([alpha, alpha], axis=1) * acc_ref[h, r0:, :] + pv
            m_ref[h, r0:, :] = m_new

    def unmasked_block(j, carry):
        number = first_block + j

        @pl.when(j + (KV_RING - 1) <= iq)
        def _():
            start(b, j + (KV_RING - 1), number + (KV_RING - 1))
        wait(b, j, number)
        slot = lax.rem(number, KV_RING)
        for sub in range(tq // tk):
            step(False, 0, sub, ka_buf.at[slot], vb_buf.at[slot])
        return carry

    lax.fori_loop(0, iq, unmasked_block, 0)

    last_q = iq == nq - 1
    next_b = jnp.where(last_q, b + 1, b)
    next_iq = jnp.where(last_q, 0, iq + 1)
    next_first = first_block + iq + 1
    has_next = jnp.logical_not(jnp.logical_and(last_q, b == nb - 1))
    for i in range(KV_RING - 1):
        @pl.when(jnp.logical_and(has_next, next_iq >= i))
        def _(i=i):
            start(next_b, i, next_first + i)

    diag = first_block + iq
    wait(b, iq, diag)
    slot = lax.rem(diag, KV_RING)
    for sub in range(tq // tk):
        step(True, sub * tk, sub, ka_buf.at[slot], vb_buf.at[slot])
    for g in range(FOX_HEADS // 2):
        even = acc_ref[2 * g, :, 0:PAIR] / acc_ref[2 * g, :, PAIR:2 * PAIR]
        odd = acc_ref[2 * g + 1, :, 0:PAIR] / acc_ref[2 * g + 1, :, PAIR:2 * PAIR]
        o_ref[:, g * PAIR:(g + 1) * PAIR] = jnp.where(lo, even, odd).astype(BF16)
    for g in range(MEM_HEADS // 2):
        sl = slice(g * PAIR, (g + 1) * PAIR)
        outs = []
        for mine in (lo, jnp.logical_not(lo)):
            qh = jnp.where(mine, qm_ref[:, sl], 0.0).astype(BF16)
            pexp, lsum = _softmax_rows(_dot_nt(qh, mk_ref[:, sl]) * SCALE)
            outs.append(jnp.dot(pexp.astype(BF16), mv_ref[:, sl], preferred_element_type=F32) / lsum)
        o_ref[:, FOX_DIM + g * PAIR:FOX_DIM + (g + 1) * PAIR] = jnp.where(lo, outs[0], outs[1]).astype(BF16)


def _prompt_attention(qa, ka, vb, qm, mkb, mvb, *, tq, tk):
    B, T, _ = qa.shape
    assert tq % tk == 0 and T % tq == 0
    return pl.pallas_call(
        functools.partial(_prompt_att_body, tq, tk),
        grid=(B, T // tq),
        in_specs=[
            pl.BlockSpec((None, tq, QK_COLS), lambda b, i: (b, i, 0)),
            pl.BlockSpec(memory_space=pl.ANY),
            pl.BlockSpec(memory_space=pl.ANY),
            pl.BlockSpec((None, tq, MEM_DIM), lambda b, i: (b, i, 0)),
            pl.BlockSpec((None, N_MEM, MEM_DIM), lambda b, i: (b, 0, 0)),
            pl.BlockSpec((None, N_MEM, MEM_DIM), lambda b, i: (b, 0, 0)),
        ],
        out_specs=pl.BlockSpec((None, tq, ATT_COLS), lambda b, i: (b, i, 0)),
        out_shape=jax.ShapeDtypeStruct((B, T, ATT_COLS), BF16),
        scratch_shapes=[pltpu.VMEM((FOX_HEADS, tq, LANES), F32), pltpu.VMEM((FOX_HEADS, tq, 2 * PAIR), F32),
                        pltpu.VMEM((KV_RING, tq, QK_COLS), BF16), pltpu.VMEM((KV_RING, tq, FOX_DIM), BF16),
                        pltpu.SemaphoreType.DMA((2, KV_RING))],
        compiler_params=pltpu.CompilerParams(
            dimension_semantics=("arbitrary", "arbitrary"), vmem_limit_bytes=VMEM_LIMIT),
        name="prompt_attention",
    )(qa, ka, vb, qm, mkb, mvb)


def _diag_blocks(x, nh, rows, width):
    head = _idiv(lax.broadcasted_iota(jnp.int32, (1, nh * width), 1), width)
    out = None
    for h in range(nh):
        part = jnp.where(head == h, x[h * rows:(h + 1) * rows, :], 0.0)
        out = part if out is None else out + part
    return out


def _sample_att_body(seq, past, per_step, *refs):
    for i in range(per_step):
        _sample_att_one(seq, past, *[r.at[i] for r in refs])


def _sample_att_one(seq, past, qa_ref, ka_ref, vb_ref, qm_ref, kt_ref, vt_ref, lt_ref, mkt_ref, mvt_ref, o_ref):
    blk = LANES
    nblk = past // blk
    lt = lt_ref[...]
    x = jnp.concatenate([lt[:, b * blk:(b + 1) * blk] for b in range(nblk)], axis=0)
    n = x.shape[0]
    parts = jnp.concatenate(_split3(x), axis=0).astype(BF16)
    src = lax.broadcasted_iota(jnp.int32, (blk, blk), 0)
    dst = lax.broadcasted_iota(jnp.int32, (blk, blk), 1)
    later = jnp.where(src > dst, 1.0, 0.0).astype(BF16)
    loc = jnp.dot(parts, later, preferred_element_type=F32)
    tot = jnp.dot(parts, jnp.ones((blk, blk), BF16), preferred_element_type=F32)
    loc = loc[0:n] + loc[n:2 * n] + loc[2 * n:3 * n]
    tot = tot[0:n] + tot[n:2 * n] + tot[2 * n:3 * n]
    running = jnp.zeros((FOX_HEADS, blk), F32)
    suffix = [None] * nblk
    for b in reversed(range(nblk)):
        rows = slice(b * FOX_HEADS, (b + 1) * FOX_HEADS)
        suffix[b] = loc[rows] + running
        running = running + tot[rows]
    rt = jnp.concatenate(suffix, axis=1) * LOG2E
    bias = jnp.concatenate([jnp.broadcast_to(rt[h:h + 1, :], (seq, past)) for h in range(FOX_HEADS)], axis=0)

    nrow = FOX_HEADS * seq
    lane = lax.broadcasted_iota(jnp.int32, (nrow, QK_COLS), 1)
    lane_head = jnp.where(lane < FOX_DIM, _idiv(lane, HEAD_DIM), _idiv(lane - FOX_DIM, AUG_GROUP))
    row_head = _idiv(lax.broadcasted_iota(jnp.int32, (nrow, QK_COLS), 0), seq)
    qbd = jnp.where(lane_head == row_head, jnp.concatenate([qa_ref[...]] * FOX_HEADS, axis=0), 0.0).astype(BF16)
    aug_row = _imod(lax.broadcasted_iota(jnp.int32, (AUG, past), 0), AUG_GROUP)
    kt = jnp.concatenate([kt_ref[...].astype(BF16), jnp.where(aug_row < N_SPLIT, 1.0, 0.0).astype(BF16)], axis=0)
    s_past = jnp.dot(qbd, kt, preferred_element_type=F32) + bias
    s_new = _dot_nt(qbd, ka_ref[...])
    qi = _imod(lax.broadcasted_iota(jnp.int32, (nrow, seq), 0), seq)
    kj = lax.broadcasted_iota(jnp.int32, (nrow, seq), 1)
    s_new = jnp.where(kj <= qi, s_new, NEG)
    m = jnp.maximum(jnp.max(s_past, axis=-1, keepdims=True), jnp.max(s_new, axis=-1, keepdims=True))
    p_past = jnp.exp2(s_past - m)
    p_new = jnp.exp2(s_new - m)
    lsum = jnp.sum(p_past, axis=-1, keepdims=True) + jnp.sum(p_new, axis=-1, keepdims=True)
    o = (_dot_nt(p_past.astype(BF16), vt_ref[...].astype(BF16))
         + jnp.dot(p_new.astype(BF16), vb_ref[...], preferred_element_type=F32)) / lsum
    o_ref[:, 0:FOX_DIM] = _diag_blocks(o, FOX_HEADS, seq, HEAD_DIM).astype(BF16)

    nrow_m = MEM_HEADS * seq
    lane_m = _idiv(lax.broadcasted_iota(jnp.int32, (nrow_m, MEM_DIM), 1), HEAD_DIM)
    row_m = _idiv(lax.broadcasted_iota(jnp.int32, (nrow_m, MEM_DIM), 0), seq)
    qbd_m = jnp.where(lane_m == row_m, jnp.concatenate([qm_ref[...]] * MEM_HEADS, axis=0), 0.0).astype(BF16)
    pm, lm = _softmax_rows(jnp.dot(qbd_m, mkt_ref[...].astype(BF16), preferred_element_type=F32) * SCALE)
    om = _dot_nt(pm.astype(BF16), mvt_ref[...].astype(BF16)) / lm
    o_ref[:, FOX_DIM:ATT_COLS] = _diag_blocks(om, MEM_HEADS, seq, HEAD_DIM).astype(BF16)


def _sample_attention(qa, ka, vb, qm, pk, pv, plf, mk, mv):
    nbatch, seq, _ = qa.shape
    past = pk.shape[2]
    per_step = SAMPLE_SEQS_PER_STEP
    assert nbatch % per_step == 0
    blk = lambda rows, cols: pl.BlockSpec((per_step, rows, cols), lambda b: (b, 0, 0))
    return pl.pallas_call(
        functools.partial(_sample_att_body, seq, past, per_step),
        grid=(nbatch // per_step,),
        in_specs=[blk(seq, QK_COLS), blk(seq, QK_COLS), blk(seq, FOX_DIM), blk(seq, MEM_DIM),
                  blk(FOX_DIM, past), blk(FOX_DIM, past), blk(FOX_HEADS, past),
                  blk(MEM_DIM, N_MEM), blk(MEM_DIM, N_MEM)],
        out_specs=blk(seq, ATT_COLS),
        out_shape=jax.ShapeDtypeStruct((nbatch, seq, ATT_COLS), BF16),
        compiler_params=pltpu.CompilerParams(
            dimension_semantics=("arbitrary",), vmem_limit_bytes=VMEM_LIMIT),
        name="sample_attention",
    )(qa, ka, vb, qm, pk, pv, plf, mk, mv)


def _prep_weights(norm_ffn1, norm_mix, w_in, b_forget, conv_w, conv_b,
                  q_norm_fox, k_norm_fox, q_norm_mem, k_norm_mem, norm_mem, w_mem_kv, w_out,
                  norm_ffn2, norm_final):
    row = lambda v: v.reshape(1, -1).astype(F32)
    return {
        "g1": row(norm_ffn1), "gmix": row(norm_mix), "win32t": w_in.T,
        "conv_w": conv_w.astype(F32), "conv_b": row(conv_b),
        "gq": row(jnp.tile(q_norm_fox, FOX_HEADS)) * (SCALE * LOG2E), "gk": row(jnp.tile(k_norm_fox, FOX_HEADS)),
        "gqm": row(jnp.tile(q_norm_mem, MEM_HEADS)),
        "bf_rep": row(jnp.repeat(b_forget, AUG_GROUP)),
        "bf": jnp.pad(row(b_forget), ((0, 0), (0, LANES - FOX_HEADS))),
        "gmem": row(norm_mem), "wmem": w_mem_kv.astype(BF16), "gkm": row(jnp.tile(k_norm_mem, MEM_HEADS)),
        "wo": w_out.astype(BF16), "g2": row(norm_ffn2), "gfin": row(norm_final),
    }


def kernel(x_prompt, x_sample, cache_fox_k, cache_fox_v, cache_fox_logf, state_conv, cache_mem_k, cache_mem_v, mem_prompt, norm_ffn1, w_ffn1_gate, w_ffn1_up, w_ffn1_down, norm_mix, w_in, b_forget, conv_w, conv_b, q_norm_fox, k_norm_fox, q_norm_mem, k_norm_mem, norm_mem, w_mem_kv, w_out, norm_ffn2, w_ffn2_gate, w_ffn2_up, w_ffn2_down, norm_final):
    depth = w_in.shape[0]
    assert depth == 1, "single-layer step"
    B, T, _ = x_prompt.shape
    nbs, seq_s, _ = x_sample.shape
    past = cache_fox_k.shape[2]
    l = 0
    wts = _prep_weights(norm_ffn1[l], norm_mix[l], w_in[l], b_forget[l], conv_w[l], conv_b[l], q_norm_fox[l],
                        k_norm_fox[l], q_norm_mem[l], k_norm_mem[l], norm_mem[l], w_mem_kv[l], w_out[l],
                        norm_ffn2[l], norm_final[l])
    tile = TOKEN_TILE
    n_s = nbs * seq_s

    x1s, wts["wg1"], wts["wu1"], wts["wd1"] = _ffn_stream(
        (x_sample.reshape(n_s, D_MODEL),), wts["g1"], w_ffn1_gate[l], w_ffn1_up[l], w_ffn1_down[l])
    ycs, qas, kas, vbs, kfs, vfs, lfs, qms, ncs, wts["win"], wts["wtail"] = _stage1(
        x1s.reshape(1, n_s, D_MODEL), state_conv[l], wts, nb=tile // seq_s, seq=seq_s, carry=False)
    per_seq = lambda a: a.reshape(nbs, seq_s, a.shape[-1])
    feat_major = lambda a: a.reshape(a.shape[0], a.shape[1], -1).transpose(0, 2, 1)
    yas = _sample_attention(
        per_seq(qas), per_seq(kas), per_seq(vbs), per_seq(qms),
        feat_major(cache_fox_k[l]), feat_major(cache_fox_v[l]), feat_major(cache_fox_logf[l]),
        feat_major(cache_mem_k[l]), feat_major(cache_mem_v[l]))
    y_sample, wts["wg2"], wts["wu2"], wts["wd2"] = _ffn_stream(
        (x1s, ycs.reshape(n_s, CONV_DIM), yas.reshape(n_s, ATT_COLS)), wts["g2"],
        w_ffn2_gate[l], w_ffn2_up[l], w_ffn2_down[l], wo=wts["wo"], gfin=wts["gfin"])

    mk_p, mv_p, mkb, mvb = _memkv(mem_prompt, wts)
    x1, yc, qa, ka, vb, kf, vf, lf, qm, nc = _stage1(x_prompt, None, wts, nb=1, seq=tile, carry=True)
    ya = _prompt_attention(qa, ka, vb, qm, mkb, mvb, tq=ATT_QUERY_TILE, tk=ATT_KEY_TILE)
    y_prompt = _stage3(x1, yc, ya, wts, tm=tile)

    heads = lambda a, b, t, nh: a.reshape(1, b, t, nh, HEAD_DIM)
    token_major = lambda a, nh: a.reshape(a.shape[0], nh, -1, a.shape[2]).transpose(0, 3, 1, 2)[None]
    return (y_prompt, y_sample.reshape(nbs, seq_s, D_MODEL),
            token_major(kf, FOX_HEADS), token_major(vf, FOX_HEADS), lf.transpose(0, 2, 1)[None],
            nc.reshape(1, B, CONV_K - 1, CONV_DIM),
            token_major(mk_p, MEM_HEADS), token_major(mv_p, MEM_HEADS),
            heads(kfs, nbs, seq_s, FOX_HEADS), heads(vfs, nbs, seq_s, FOX_HEADS),
            lfs.reshape(1, nbs, seq_s, FOX_HEADS), ncs.reshape(1, nbs, CONV_K - 1, CONV_DIM))
```

```python
import functools

import jax
import jax.numpy as jnp
from jax import lax
from jax.experimental import pallas as pl
from jax.experimental.pallas import tpu as pltpu

F32 = jnp.float32
BF16 = jnp.bfloat16

D_MODEL = 1024
HEAD_DIM = 64
CONV_DIM = 256
CONV_K = 3
FOX_HEADS = 8
FOX_DIM = FOX_HEADS * HEAD_DIM
MEM_HEADS = 4
MEM_DIM = MEM_HEADS * HEAD_DIM
N_MEM = 256
D_FF = 2816
EPS = 1e-6

LANES = 128
PAIR = 2 * HEAD_DIM
FF_CHUNK = 256
N_FF_CHUNKS = D_FF // FF_CHUNK
AUG = LANES
AUG_GROUP = AUG // FOX_HEADS
N_SPLIT = 3
QK_COLS = FOX_DIM + AUG
ATT_COLS = FOX_DIM + MEM_DIM
COL_Q = 3 * CONV_DIM
COL_K = COL_Q + FOX_DIM
COL_V = COL_K + FOX_DIM
COL_F = COL_V + FOX_DIM
COL_QM = COL_F + FOX_HEADS
TAIL_COLS = MEM_DIM + AUG + LANES
NEG = -1e30
SCALE = HEAD_DIM ** -0.5
LOG2E = 1.4426950408889634
VMEM_LIMIT = 56 * 1024 * 1024
TOKEN_TILE = 512
ATT_KEY_TILE = 512
ATT_QUERY_TILE = 1024
SAMPLE_SEQS_PER_STEP = 4
WEIGHT_RING = 3
KV_RING = 3


def _idiv(x, d):
    if d & (d - 1) == 0:
        return lax.shift_right_logical(x, d.bit_length() - 1)
    return x // d


def _imod(x, d):
    if d & (d - 1) == 0:
        return x & (d - 1)
    return x % d


def _rms(x, g):
    ms = jnp.mean(x * x, axis=-1, keepdims=True)
    return x * lax.rsqrt(ms + EPS) * g


def _pair_headnorm(x, g):
    lo = lax.broadcasted_iota(jnp.int32, (1, PAIR), 1) < HEAD_DIM
    x2 = x * x
    s_lo = jnp.sum(jnp.where(lo, x2, 0.0), axis=-1, keepdims=True)
    s_hi = jnp.sum(jnp.where(lo, 0.0, x2), axis=-1, keepdims=True)
    r = jnp.where(lo, lax.rsqrt(s_lo * (1.0 / HEAD_DIM) + EPS),
                  lax.rsqrt(s_hi * (1.0 / HEAD_DIM) + EPS))
    return x * r * g


def _store_cols(ref, sl, x, transposed):
    if transposed:
        ref[sl, :] = x.T
    else:
        ref[:, sl] = x


def _split3(x):
    hi = x.astype(BF16).astype(F32)
    r = x - hi
    mid = r.astype(BF16).astype(F32)
    lo = (r - mid).astype(BF16).astype(F32)
    return hi, mid, lo


def _ffn(xn, wg_ref, wu_ref, wd_ref):
    acc = None
    for c in range(N_FF_CHUNKS):
        sl = slice(c * FF_CHUNK, (c + 1) * FF_CHUNK)
        g = jnp.dot(xn, wg_ref[:, sl], preferred_element_type=F32)
        u = jnp.dot(xn, wu_ref[:, sl], preferred_element_type=F32)
        a = (g * jax.nn.sigmoid(g) * u).astype(BF16)
        d = jnp.dot(a, wd_ref[sl, :], preferred_element_type=F32)
        acc = d if acc is None else acc + d
    return acc


def _log_sigmoid(x):
    return jnp.minimum(x, 0.0) - jnp.log1p(jnp.exp(-jnp.abs(x)))


def _softmax_rows(s):
    m = jnp.max(s, axis=-1, keepdims=True)
    p = jnp.exp(s - m)
    return p, jnp.sum(p, axis=-1, keepdims=True)


def _dot_nt(a, b):
    return lax.dot_general(a, b, (((1,), (1,)), ((), ())), preferred_element_type=F32)


_STAGE1_SHARED_IN = ("gmix", "conv_w", "conv_b", "gq", "gk", "gqm", "bf_rep", "bf")
_STAGE1_SHARED_OUT = ("yc", "qa", "ka", "vb", "kf", "vf", "lf", "qm", "nc")
_STAGE1_IN = {True: ("x", "g1", "wg1", "wu1", "wd1", "win", "wtail") + _STAGE1_SHARED_IN,
              False: ("x1", "state", "win32t") + _STAGE1_SHARED_IN}
_STAGE1_OUT = {True: ("x1",) + _STAGE1_SHARED_OUT, False: _STAGE1_SHARED_OUT + ("win", "wtail")}


def _stage1_body(nb, seq, carry, *refs):
    tm = nb * seq
    names = _STAGE1_IN[carry] + _STAGE1_OUT[carry]
    r = dict(zip(names, refs[:len(names)]))
    cs_ref, cum_ref = refs[len(names):]
    gmix_ref, cw_ref, cb_ref, gq_ref, gk_ref, gqm_ref, bfr_ref, bf_ref = (r[n] for n in _STAGE1_SHARED_IN)
    yc_ref, qa_ref, ka_ref, vb_ref, kf_ref, vf_ref, lf_ref, qm_ref, nc_ref = (r[n] for n in _STAGE1_SHARED_OUT)
    win_ref, wtail_ref = r["win"], r["wtail"]
    if carry:
        x = r["x"][...]
        y = _ffn(_rms(x, r["g1"][...]).astype(BF16), r["wg1"], r["wu1"], r["wd1"])
        x1 = x + 0.5 * y
        r["x1"][...] = x1
    else:
        x1 = r["x1"][...]
        st_ref = r["state"]

        @pl.when(pl.program_id(1) == 0)
        def _():
            w_t = r["win32t"]
            win_ref[...] = w_t[0:COL_F, :].T.astype(BF16)
            f_rows = w_t[COL_F:COL_QM, :]
            f_rep = jnp.concatenate([jnp.broadcast_to(f_rows[hd:hd + 1, :], (AUG_GROUP, D_MODEL))
                                     for hd in range(FOX_HEADS)], axis=0)
            tail_t = jnp.concatenate([w_t[COL_QM:COL_QM + MEM_DIM, :], f_rep, f_rows,
                                      jnp.zeros((LANES - FOX_HEADS, D_MODEL), F32)], axis=0)
            wtail_ref[...] = tail_t.T.astype(BF16)
    h = _rms(x1, gmix_ref[...]).astype(BF16)

    ucb = jnp.dot(h, win_ref[:, 0:COL_Q], preferred_element_type=F32)
    ci = ucb[:, CONV_DIM:2 * CONV_DIM] * ucb[:, 0:CONV_DIM]
    if carry:
        @pl.when(pl.program_id(1) == 0)
        def _():
            cs_ref[:, 0:8, :] = jnp.zeros((nb, 8, CONV_DIM), F32)
    else:
        cs_ref[:, 8 - (CONV_K - 1):8, :] = st_ref[...]
    cs_ref[:, 8:8 + seq, :] = ci.reshape(nb, seq, CONV_DIM)
    conv = cb_ref[...] + cw_ref[CONV_K - 1:CONV_K, :] * ci
    for i in range(CONV_K - 1):
        shifted = cs_ref[:, 8 - (CONV_K - 1) + i:8 - (CONV_K - 1) + i + seq, :]
        conv = conv + cw_ref[i:i + 1, :] * shifted.reshape(tm, CONV_DIM)
    yc_ref[...] = (ucb[:, 2 * CONV_DIM:3 * CONV_DIM] * conv).astype(BF16)
    tail = cs_ref[:, seq:seq + 8, :]
    nc_ref[...] = tail[:, 8 - (CONV_K - 1):, :].reshape(nc_ref.shape)
    if carry:
        cs_ref[:, 0:8, :] = tail

    for name, col, g_ref, out_ref in (("q", COL_Q, gq_ref, qa_ref), ("k", COL_K, gk_ref, ka_ref)):
        pr = jnp.dot(h, win_ref[:, col:col + FOX_DIM], preferred_element_type=F32)
        for g in range(FOX_HEADS // 2):
            sl = slice(g * PAIR, (g + 1) * PAIR)
            xn = _pair_headnorm(pr[:, sl], g_ref[:, sl])
            if name == "k":
                _store_cols(kf_ref, sl, xn, carry)
            out_ref[:, sl] = xn.astype(BF16)
    v = jnp.dot(h, win_ref[:, COL_V:COL_V + FOX_DIM], preferred_element_type=F32)
    for g in range(FOX_HEADS // 2):
        sl = slice(g * PAIR, (g + 1) * PAIR)
        _store_cols(vf_ref, sl, v[:, sl], carry)
    vb_ref[...] = v.astype(BF16)

    tail_cols = jnp.dot(h, wtail_ref[...], preferred_element_type=F32)
    for g in range(MEM_HEADS // 2):
        sl = slice(g * PAIR, (g + 1) * PAIR)
        qm_ref[:, sl] = _pair_headnorm(tail_cols[:, sl], gqm_ref[:, sl]).astype(BF16)
    fl = tail_cols[:, MEM_DIM:]
    lf = _log_sigmoid(fl[:, AUG:2 * AUG] + bf_ref[...])
    if carry:
        lf_ref[...] = lf.T[0:FOX_HEADS, :]
    else:
        lf_ref[...] = lf[:, 0:FOX_HEADS]
    logf = _log_sigmoid(fl[:, 0:AUG] + bfr_ref[...])

    row = lax.broadcasted_iota(jnp.int32, (tm, tm), 0)
    col = lax.broadcasted_iota(jnp.int32, (tm, tm), 1)
    tri = col <= row
    if nb > 1:
        tri = jnp.logical_and(tri, _idiv(row, seq) == _idiv(col, seq))
    tri = jnp.where(tri, 1.0, 0.0).astype(BF16)
    hi, mid, lo = _split3(logf)
    parts = jnp.concatenate([hi.astype(BF16), mid.astype(BF16), lo.astype(BF16)], axis=1)
    cs = jnp.dot(tri, parts, preferred_element_type=F32)
    cum = cs[:, 0:AUG] + cs[:, AUG:2 * AUG] + cs[:, 2 * AUG:3 * AUG]
    if carry:
        @pl.when(pl.program_id(1) == 0)
        def _():
            cum_ref[...] = jnp.zeros(cum_ref.shape, F32)
        cum = cum + cum_ref[0:1, :]
        cum_ref[0:1, :] = cum[tm - 1:tm, :]

    j = _imod(lax.broadcasted_iota(jnp.int32, (1, AUG), 1), AUG_GROUP)
    terms = _split3(cum * LOG2E)
    aq = jnp.where(j < 2 * N_SPLIT, 1.0, 0.0)
    ak = jnp.where(j < N_SPLIT, 1.0, 0.0)
    for i, term in enumerate(terms):
        aq = jnp.where(j == i, term, aq)
        ak = jnp.where(j == N_SPLIT + i, -term, ak)
    qa_ref[:, FOX_DIM:QK_COLS] = aq.astype(BF16)
    ka_ref[:, FOX_DIM:QK_COLS] = ak.astype(BF16)


def _stage1(x3, state, wts, *, nb, seq, carry):
    G, T, _ = x3.shape
    tm = nb * seq
    nt = T // tm
    const = lambda shape: pl.BlockSpec(shape, lambda g, t: (0,) * len(shape), pipeline_mode=pl.Buffered(1))
    tok = lambda cols: pl.BlockSpec((None, tm, cols), lambda g, t: (g, t, 0))
    if carry:
        feat = lambda rows: pl.BlockSpec((None, rows, tm), lambda g, t: (g, 0, t))
        feat_shape = lambda rows: jax.ShapeDtypeStruct((G, rows, T), F32)
    else:
        feat = lambda rows: pl.BlockSpec((None, tm, rows), lambda g, t: (g, t, 0))
        feat_shape = lambda rows: jax.ShapeDtypeStruct((G, T, rows), F32)
    shp = lambda cols, dt: jax.ShapeDtypeStruct((G, T, cols), dt)
    if carry:
        nc_spec = pl.BlockSpec((None, CONV_K - 1, CONV_DIM), lambda g, t: (g, 0, 0))
        nc_shape = jax.ShapeDtypeStruct((G, CONV_K - 1, CONV_DIM), F32)
    else:
        nc_spec = pl.BlockSpec((nb, CONV_K - 1, CONV_DIM), lambda g, t: (t, 0, 0))
        nc_shape = jax.ShapeDtypeStruct((nt * nb, CONV_K - 1, CONV_DIM), F32)
    tiled_in = {"x": (tok(D_MODEL), x3), "x1": (tok(D_MODEL), x3), "state": (nc_spec, state)}
    in_specs, args = [], []
    for n in _STAGE1_IN[carry]:
        spec, arg = tiled_in[n] if n in tiled_in else (const(wts[n].shape), wts[n])
        in_specs.append(spec)
        args.append(arg)
    outs = {"x1": (tok(D_MODEL), shp(D_MODEL, F32)), "yc": (tok(CONV_DIM), shp(CONV_DIM, BF16)),
            "qa": (tok(QK_COLS), shp(QK_COLS, BF16)), "ka": (tok(QK_COLS), shp(QK_COLS, BF16)),
            "vb": (tok(FOX_DIM), shp(FOX_DIM, BF16)), "kf": (feat(FOX_DIM), feat_shape(FOX_DIM)),
            "vf": (feat(FOX_DIM), feat_shape(FOX_DIM)), "lf": (feat(FOX_HEADS), feat_shape(FOX_HEADS)),
            "qm": (tok(MEM_DIM), shp(MEM_DIM, BF16)), "nc": (nc_spec, nc_shape)}
    if not carry:
        for n, w_shape in (("win", (D_MODEL, COL_F)), ("wtail", (D_MODEL, TAIL_COLS))):
            outs[n] = (pl.BlockSpec(w_shape, lambda g, t: (0, 0)), jax.ShapeDtypeStruct(w_shape, BF16))
    out_specs = [outs[n][0] for n in _STAGE1_OUT[carry]]
    out_shape = [outs[n][1] for n in _STAGE1_OUT[carry]]
    return pl.pallas_call(
        functools.partial(_stage1_body, nb, seq, carry),
        grid=(G, nt),
        in_specs=in_specs,
        out_specs=out_specs,
        out_shape=out_shape,
        scratch_shapes=[pltpu.VMEM((nb, seq + 8, CONV_DIM), F32), pltpu.VMEM((8, AUG), F32)],
        compiler_params=pltpu.CompilerParams(
            dimension_semantics=("arbitrary", "arbitrary"), vmem_limit_bytes=VMEM_LIMIT),
        name="stage1_carry" if carry else "stage1_batched",
    )(*args)


def _stage3_body(x1_ref, yc_ref, ya_ref, wo_ref, g2_ref, wg_ref, wu_ref, wd_ref, gfin_ref, y_ref):
    x2 = (x1_ref[...]
          + jnp.dot(yc_ref[...], wo_ref[0:CONV_DIM, :], preferred_element_type=F32)
          + jnp.dot(ya_ref[...], wo_ref[CONV_DIM:, :], preferred_element_type=F32))
    y = _ffn(_rms(x2, g2_ref[...]).astype(BF16), wg_ref, wu_ref, wd_ref)
    y_ref[...] = _rms(x2 + 0.5 * y, gfin_ref[...])


def _stage3(x1, yc, ya, wts, *, tm):
    G, T, _ = x1.shape
    const = lambda shape: pl.BlockSpec(shape, lambda g, t: (0,) * len(shape), pipeline_mode=pl.Buffered(1))
    tok = lambda cols: pl.BlockSpec((None, tm, cols), lambda g, t: (g, t, 0))
    names = ("wo", "g2", "wg2", "wu2", "wd2", "gfin")
    return pl.pallas_call(
        _stage3_body,
        grid=(G, T // tm),
        in_specs=[tok(D_MODEL), tok(CONV_DIM), tok(ATT_COLS)] + [const(wts[n].shape) for n in names],
        out_specs=tok(D_MODEL),
        out_shape=jax.ShapeDtypeStruct((G, T, D_MODEL), F32),
        compiler_params=pltpu.CompilerParams(
            dimension_semantics=("arbitrary", "arbitrary"), vmem_limit_bytes=VMEM_LIMIT),
        name="stage3",
    )(x1, yc, ya, *[wts[n] for n in names])


def _ffn_stream_body(with_out_proj, with_final_norm, *refs):
    refs = list(refs)
    n_x = 4 if with_out_proj else 1
    x_refs, refs = refs[:n_x], refs[n_x:]
    g_ref, wg_hbm, wu_hbm, wd_hbm = refs[:4]
    refs = refs[4:]
    gfin_ref = refs.pop(0) if with_final_norm else None
    y_ref, wgb_ref, wub_ref, wdb_ref, x_ref, xn_ref, acc_ref, wg_buf, wu_buf, wd_buf, sem = refs
    c = pl.program_id(0)

    def chunk_copies(k):
        slot = lax.rem(k, WEIGHT_RING)
        off = pl.multiple_of(k * FF_CHUNK, FF_CHUNK)
        return (pltpu.make_async_copy(wg_hbm.at[:, pl.ds(off, FF_CHUNK)], wg_buf.at[slot], sem.at[0, slot]),
                pltpu.make_async_copy(wu_hbm.at[:, pl.ds(off, FF_CHUNK)], wu_buf.at[slot], sem.at[1, slot]),
                pltpu.make_async_copy(wd_hbm.at[pl.ds(off, FF_CHUNK), :], wd_buf.at[slot], sem.at[2, slot]))

    @pl.when(c == 0)
    def _():
        for k in range(WEIGHT_RING - 1):
            for cp in chunk_copies(jnp.int32(k)):
                cp.start()

    @pl.when(c + (WEIGHT_RING - 1) < pl.num_programs(0))
    def _():
        for cp in chunk_copies(c + (WEIGHT_RING - 1)):
            cp.start()

    @pl.when(c == 0)
    def _():
        if with_out_proj:
            x1_ref, yc_ref, ya_ref, wo_ref = x_refs
            x = (x1_ref[...]
                 + jnp.dot(yc_ref[...], wo_ref[0:CONV_DIM, :], preferred_element_type=F32)
                 + jnp.dot(ya_ref[...], wo_ref[CONV_DIM:, :], preferred_element_type=F32))
        else:
            x = x_refs[0][...]
        x_ref[...] = x
        xn_ref[...] = _rms(x, g_ref[...]).astype(BF16)
        acc_ref[...] = jnp.zeros(acc_ref.shape, F32)

    for cp in chunk_copies(c):
        cp.wait()
    slot = lax.rem(c, WEIGHT_RING)
    wg = wg_buf[slot].astype(BF16)
    wu = wu_buf[slot].astype(BF16)
    wd = wd_buf[slot].astype(BF16)
    wgb_ref[...] = wg
    wub_ref[...] = wu
    wdb_ref[...] = wd
    xn = xn_ref[...]
    g = jnp.dot(xn, wg, preferred_element_type=F32)
    u = jnp.dot(xn, wu, preferred_element_type=F32)
    a = (g * jax.nn.sigmoid(g) * u).astype(BF16)
    acc_ref[...] += jnp.dot(a, wd, preferred_element_type=F32)

    @pl.when(c == pl.num_programs(0) - 1)
    def _():
        y = x_ref[...] + 0.5 * acc_ref[...]
        y_ref[...] = _rms(y, gfin_ref[...]) if with_final_norm else y


def _ffn_stream(x_parts, gain, wg, wu, wd, *, wo=None, gfin=None):
    m = x_parts[0].shape[0]
    const = lambda a: pl.BlockSpec(a.shape, lambda c: (0,) * a.ndim, pipeline_mode=pl.Buffered(1))
    col_chunk = pl.BlockSpec((D_MODEL, FF_CHUNK), lambda c: (0, c))
    row_chunk = pl.BlockSpec((FF_CHUNK, D_MODEL), lambda c: (c, 0))
    args = list(x_parts) + ([wo] if wo is not None else []) + [gain]
    in_specs = [const(a) for a in args] + [pl.BlockSpec(memory_space=pl.ANY)] * 3
    args += [wg, wu, wd]
    if gfin is not None:
        in_specs.append(const(gfin))
        args.append(gfin)
    return pl.pallas_call(
        functools.partial(_ffn_stream_body, wo is not None, gfin is not None),
        grid=(N_FF_CHUNKS,),
        in_specs=in_specs,
        out_specs=[pl.BlockSpec((m, D_MODEL), lambda c: (0, 0)), col_chunk, col_chunk, row_chunk],
        out_shape=[jax.ShapeDtypeStruct((m, D_MODEL), F32), jax.ShapeDtypeStruct(wg.shape, BF16),
                   jax.ShapeDtypeStruct(wu.shape, BF16), jax.ShapeDtypeStruct(wd.shape, BF16)],
        scratch_shapes=[pltpu.VMEM((m, D_MODEL), F32), pltpu.VMEM((m, D_MODEL), BF16),
                        pltpu.VMEM((m, D_MODEL), F32),
                        pltpu.VMEM((WEIGHT_RING, D_MODEL, FF_CHUNK), F32),
                        pltpu.VMEM((WEIGHT_RING, D_MODEL, FF_CHUNK), F32),
                        pltpu.VMEM((WEIGHT_RING, FF_CHUNK, D_MODEL), F32),
                        pltpu.SemaphoreType.DMA((3, WEIGHT_RING))],
        compiler_params=pltpu.CompilerParams(dimension_semantics=("arbitrary",), vmem_limit_bytes=VMEM_LIMIT),
        name="ffn_stream_out" if wo is not None else "ffn_stream_in",
    )(*args)


def _memkv_body(mem_ref, gmem_ref, w_ref, gk_ref, mk_ref, mv_ref, mkb_ref, mvb_ref):
    kv = jnp.dot(_rms(mem_ref[...], gmem_ref[...]).astype(BF16), w_ref[...], preferred_element_type=F32)
    for g in range(MEM_HEADS // 2):
        sl = slice(g * PAIR, (g + 1) * PAIR)
        mk = _pair_headnorm(kv[:, sl], gk_ref[:, sl])
        mk_ref[sl, :] = mk.T
        mkb_ref[:, sl] = mk.astype(BF16)
    mv = kv[:, MEM_DIM:]
    mv_ref[...] = mv.T
    mvb_ref[...] = mv.astype(BF16)


def _memkv(mem, wts):
    B = mem.shape[0]
    const = lambda shape: pl.BlockSpec(shape, lambda b: (0,) * len(shape))
    blk = lambda cols: pl.BlockSpec((None, N_MEM, cols), lambda b: (b, 0, 0))
    shp = lambda dt: jax.ShapeDtypeStruct((B, N_MEM, MEM_DIM), dt)
    names = ("gmem", "wmem", "gkm")
    return pl.pallas_call(
        _memkv_body,
        grid=(B,),
        in_specs=[blk(D_MODEL)] + [const(wts[n].shape) for n in names],
        out_specs=[blk(MEM_DIM)] * 4,
        out_shape=[shp(F32), shp(F32), shp(BF16), shp(BF16)],
        compiler_params=pltpu.CompilerParams(dimension_semantics=("arbitrary",)),
        name="memkv",
    )(mem, *[wts[n] for n in names])


def _head_masks():
    lane = lax.broadcasted_iota(jnp.int32, (1, LANES), 1)
    lo = lane < HEAD_DIM
    return lo, _idiv(lane, AUG_GROUP)


def _prompt_att_body(tq, tk, qa_ref, ka_hbm, vb_hbm, qm_ref, mk_ref, mv_ref,
                     o_ref, m_ref, acc_ref, ka_buf, vb_buf, sem):
    b, iq = pl.program_id(0), pl.program_id(1)
    nb, nq = pl.num_programs(0), pl.num_programs(1)
    lo, aug_head = _head_masks()
    first_block = b * lax.shift_right_logical(nq * (nq + 1), 1) + lax.shift_right_logical(iq * (iq + 1), 1)

    def block_copies(batch, j, number):
        slot = lax.rem(number, KV_RING)
        off = j * tq
        rows = pl.ds(off if isinstance(off, int) else pl.multiple_of(off, tq), tq)
        return (pltpu.make_async_copy(ka_hbm.at[batch, rows, :], ka_buf.at[slot], sem.at[0, slot]),
                pltpu.make_async_copy(vb_hbm.at[batch, rows, :], vb_buf.at[slot], sem.at[1, slot]))

    def start(batch, j, number):
        for cp in block_copies(batch, j, number):
            cp.start()

    def wait(batch, j, number):
        for cp in block_copies(batch, j, number):
            cp.wait()

    @pl.when(jnp.logical_and(b == 0, iq == 0))
    def _():
        start(b, iq, first_block)
        m_ref[...] = jnp.full(m_ref.shape, NEG, F32)
        acc_ref[...] = jnp.zeros(acc_ref.shape, F32)

    def step(masked, r0, sub, ka_ref, vb_ref):
        rows = tq - r0
        keys = slice(sub * tk, (sub + 1) * tk)
        if masked:
            visible = (lax.broadcasted_iota(jnp.int32, (tk, tk), 1)
                       <= lax.broadcasted_iota(jnp.int32, (tk, tk), 0))
        q_aug = qa_ref[r0:, FOX_DIM:QK_COLS]
        k_aug = ka_ref[keys, FOX_DIM:QK_COLS]
        ones = jnp.ones((tk, LANES), BF16)
        for h in range(FOX_HEADS):
            sl = slice((h // 2) * PAIR, (h // 2 + 1) * PAIR)
            mine = lo if h % 2 == 0 else jnp.logical_not(lo)
            qh = jnp.concatenate([jnp.where(mine, qa_ref[r0:, sl], 0.0).astype(BF16),
                                  jnp.where(aug_head == h, q_aug, 0.0).astype(BF16)], axis=1)
            kh = jnp.concatenate([ka_ref[keys, sl], k_aug], axis=1)
            s = _dot_nt(qh, kh)
            if masked:
                top = jnp.where(visible, s[:tk], NEG)
                s = top if rows == tk else jnp.concatenate([top, s[tk:]], axis=0)
            blocks = [s[:, c * LANES:(c + 1) * LANES] for c in range(tk // LANES)]
            blk_max = functools.reduce(jnp.maximum, blocks)
            m_prev = m_ref[h, r0:, :]
            m_new = jnp.maximum(m_prev, jnp.max(blk_max, axis=-1, keepdims=True))
            alpha = jnp.exp2(m_prev - m_new)
            pexp = jnp.concatenate([jnp.exp2(b - m_new) for b in blocks], axis=1).astype(BF16)
            pv = jnp.dot(pexp, jnp.concatenate([vb_ref[keys, sl], ones], axis=1), preferred_element_type=F32)
            acc_ref[h, r0:, :] = jnp.concatenate([alpha, alpha], axis=1) * acc_ref[h, r0:, :] + pv
            m_ref[h, r0:, :] = m_new

    def unmasked_block(j, carry):
        number = first_block + j

        @pl.when(j + (KV_RING - 1) <= iq)
        def _():
            start(b, j + (KV_RING - 1), number + (KV_RING - 1))
        wait(b, j, number)
        slot = lax.rem(number, KV_RING)
        for sub in range(tq // tk):
            step(False, 0, sub, ka_buf.at[slot], vb_buf.at[slot])
        return carry

    lax.fori_loop(0, iq, unmasked_block, 0)

    last_q = iq == nq - 1
    next_b = jnp.where(last_q, b + 1, b)
    next_iq = jnp.where(last_q, 0, iq + 1)
    next_first = first_block + iq + 1
    has_next = jnp.logical_not(jnp.logical_and(last_q, b == nb - 1))
    for i in range(KV_RING - 1):
        @pl.when(jnp.logical_and(has_next, next_iq >= i))
        def _(i=i):
            start(next_b, i, next_first + i)

    diag = first_block + iq
    wait(b, iq, diag)
    slot = lax.rem(diag, KV_RING)
    for sub in range(tq // tk):
        step(True, sub * tk, sub, ka_buf.at[slot], vb_buf.at[slot])
    for g in range(FOX_HEADS // 2):
        even = acc_ref[2 * g, :, 0:PAIR] / acc_ref[2 * g, :, PAIR:2 * PAIR]
        odd = acc_ref[2 * g + 1, :, 0:PAIR] / acc_ref[2 * g + 1, :, PAIR:2 * PAIR]
        o_ref[:, g * PAIR:(g + 1) * PAIR] = jnp.where(lo, even, odd).astype(BF16)
        for h in (2 * g, 2 * g + 1):
            m_ref[h] = jnp.full((tq, LANES), NEG, F32)
            acc_ref[h] = jnp.zeros((tq, 2 * PAIR), F32)
    for g in range(MEM_HEADS // 2):
        sl = slice(g * PAIR, (g + 1) * PAIR)
        outs = []
        for mine in (lo, jnp.logical_not(lo)):
            qh = jnp.where(mine, qm_ref[:, sl], 0.0).astype(BF16)
            pexp, lsum = _softmax_rows(_dot_nt(qh, mk_ref[:, sl]) * SCALE)
            outs.append(jnp.dot(pexp.astype(BF16), mv_ref[:, sl], preferred_element_type=F32) / lsum)
        o_ref[:, FOX_DIM + g * PAIR:FOX_DIM + (g + 1) * PAIR] = jnp.where(lo, outs[0], outs[1]).astype(BF16)


def _prompt_attention(qa, ka, vb, qm, mkb, mvb, *, tq, tk):
    B, T, _ = qa.shape
    assert tq % tk == 0 and T % tq == 0
    return pl.pallas_call(
        functools.partial(_prompt_att_body, tq, tk),
        grid=(B, T // tq),
        in_specs=[
            pl.BlockSpec((None, tq, QK_COLS), lambda b, i: (b, i, 0)),
            pl.BlockSpec(memory_space=pl.ANY),
            pl.BlockSpec(memory_space=pl.ANY),
            pl.BlockSpec((None, tq, MEM_DIM), lambda b, i: (b, i, 0)),
            pl.BlockSpec((None, N_MEM, MEM_DIM), lambda b, i: (b, 0, 0)),
            pl.BlockSpec((None, N_MEM, MEM_DIM), lambda b, i: (b, 0, 0)),
        ],
        out_specs=pl.BlockSpec((None, tq, ATT_COLS), lambda b, i: (b, i, 0)),
        out_shape=jax.ShapeDtypeStruct((B, T, ATT_COLS), BF16),
        scratch_shapes=[pltpu.VMEM((FOX_HEADS, tq, LANES), F32), pltpu.VMEM((FOX_HEADS, tq, 2 * PAIR), F32),
                        pltpu.VMEM((KV_RING, tq, QK_COLS), BF16), pltpu.VMEM((KV_RING, tq, FOX_DIM), BF16),
                        pltpu.SemaphoreType.DMA((2, KV_RING))],
        compiler_params=pltpu.CompilerParams(
            dimension_semantics=("arbitrary", "arbitrary"), vmem_limit_bytes=VMEM_LIMIT),
        name="prompt_attention",
    )(qa, ka, vb, qm, mkb, mvb)


def _diag_blocks(x, nh, rows, width):
    head = _idiv(lax.broadcasted_iota(jnp.int32, (1, nh * width), 1), width)
    out = None
    for h in range(nh):
        part = jnp.where(head == h, x[h * rows:(h + 1) * rows, :], 0.0)
        out = part if out is None else out + part
    return out


def _sample_att_body(seq, past, per_step, *refs):
    for i in range(per_step):
        _sample_att_one(seq, past, *[r.at[i] for r in refs])


def _sample_att_one(seq, past, qa_ref, ka_ref, vb_ref, qm_ref, kt_ref, vt_ref, lt_ref, mkt_ref, mvt_ref, o_ref):
    blk = LANES
    nblk = past // blk
    lt = lt_ref[...]
    x = jnp.concatenate([lt[:, b * blk:(b + 1) * blk] for b in range(nblk)], axis=0)
    n = x.shape[0]
    parts = jnp.concatenate(_split3(x), axis=0).astype(BF16)
    src = lax.broadcasted_iota(jnp.int32, (blk, blk), 0)
    dst = lax.broadcasted_iota(jnp.int32, (blk, blk), 1)
    later = jnp.where(src > dst, 1.0, 0.0).astype(BF16)
    loc = jnp.dot(parts, later, preferred_element_type=F32)
    tot = jnp.dot(parts, jnp.ones((blk, blk), BF16), preferred_element_type=F32)
    loc = loc[0:n] + loc[n:2 * n] + loc[2 * n:3 * n]
    tot = tot[0:n] + tot[n:2 * n] + tot[2 * n:3 * n]
    running = jnp.zeros((FOX_HEADS, blk), F32)
    suffix = [None] * nblk
    for b in reversed(range(nblk)):
        rows = slice(b * FOX_HEADS, (b + 1) * FOX_HEADS)
        suffix[b] = loc[rows] + running
        running = running + tot[rows]
    rt = jnp.concatenate(suffix, axis=1) * LOG2E
    bias = jnp.concatenate([jnp.broadcast_to(rt[h:h + 1, :], (seq, past)) for h in range(FOX_HEADS)], axis=0)

    nrow = FOX_HEADS * seq
    lane = lax.broadcasted_iota(jnp.int32, (nrow, QK_COLS), 1)
    lane_head = jnp.where(lane < FOX_DIM, _idiv(lane, HEAD_DIM), _idiv(lane - FOX_DIM, AUG_GROUP))
    row_head = _idiv(lax.broadcasted_iota(jnp.int32, (nrow, QK_COLS), 0), seq)
    qbd = jnp.where(lane_head == row_head, jnp.concatenate([qa_ref[...]] * FOX_HEADS, axis=0), 0.0).astype(BF16)
    aug_row = _imod(lax.broadcasted_iota(jnp.int32, (AUG, past), 0), AUG_GROUP)
    kt = jnp.concatenate([kt_ref[...].astype(BF16), jnp.where(aug_row < N_SPLIT, 1.0, 0.0).astype(BF16)], axis=0)
    s_past = jnp.dot(qbd, kt, preferred_element_type=F32) + bias
    s_new = _dot_nt(qbd, ka_ref[...])
    qi = _imod(lax.broadcasted_iota(jnp.int32, (nrow, seq), 0), seq)
    kj = lax.broadcasted_iota(jnp.int32, (nrow, seq), 1)
    s_new = jnp.where(kj <= qi, s_new, NEG)
    m = jnp.maximum(jnp.max(s_past, axis=-1, keepdims=True), jnp.max(s_new, axis=-1, keepdims=True))
    p_past = jnp.exp2(s_past - m)
    p_new = jnp.exp2(s_new - m)
    lsum = jnp.sum(p_past, axis=-1, keepdims=True) + jnp.sum(p_new, axis=-1, keepdims=True)
    o = (_dot_nt(p_past.astype(BF16), vt_ref[...].astype(BF16))
         + jnp.dot(p_new.astype(BF16), vb_ref[...], preferred_element_type=F32)) / lsum
    o_ref[:, 0:FOX_DIM] = _diag_blocks(o, FOX_HEADS, seq, HEAD_DIM).astype(BF16)

    nrow_m = MEM_HEADS * seq
    lane_m = _idiv(lax.broadcasted_iota(jnp.int32, (nrow_m, MEM_DIM), 1), HEAD_DIM)
    row_m = _idiv(lax.broadcasted_iota(jnp.int32, (nrow_m, MEM_DIM), 0), seq)
    qbd_m = jnp.where(lane_m == row_m, jnp.concatenate([qm_ref[...]] * MEM_HEADS, axis=0), 0.0).astype(BF16)
    pm, lm = _softmax_rows(jnp.dot(qbd_m, mkt_ref[...].astype(BF16), preferred_element_type=F32) * SCALE)
    om = _dot_nt(pm.astype(BF16), mvt_ref[...].astype(BF16)) / lm
    o_ref[:, FOX_DIM:ATT_COLS] = _diag_blocks(om, MEM_HEADS, seq, HEAD_DIM).astype(BF16)


def _sample_attention(qa, ka, vb, qm, pk, pv, plf, mk, mv):
    nbatch, seq, _ = qa.shape
    past = pk.shape[2]
    per_step = SAMPLE_SEQS_PER_STEP
    assert nbatch % per_step == 0
    blk = lambda rows, cols: pl.BlockSpec((per_step, rows, cols), lambda b: (b, 0, 0))
    return pl.pallas_call(
        functools.partial(_sample_att_body, seq, past, per_step),
        grid=(nbatch // per_step,),
        in_specs=[blk(seq, QK_COLS), blk(seq, QK_COLS), blk(seq, FOX_DIM), blk(seq, MEM_DIM),
                  blk(FOX_DIM, past), blk(FOX_DIM, past), blk(FOX_HEADS, past),
                  blk(MEM_DIM, N_MEM), blk(MEM_DIM, N_MEM)],
        out_specs=blk(seq, ATT_COLS),
        out_shape=jax.ShapeDtypeStruct((nbatch, seq, ATT_COLS), BF16),
        compiler_params=pltpu.CompilerParams(
            dimension_semantics=("arbitrary",), vmem_limit_bytes=VMEM_LIMIT),
        name="sample_attention",
    )(qa, ka, vb, qm, pk, pv, plf, mk, mv)


def _prep_weights(norm_ffn1, norm_mix, w_in, b_forget, conv_w, conv_b,
                  q_norm_fox, k_norm_fox, q_norm_mem, k_norm_mem, norm_mem, w_mem_kv, w_out,
                  norm_ffn2, norm_final):
    row = lambda v: v.reshape(1, -1).astype(F32)
    return {
        "g1": row(norm_ffn1), "gmix": row(norm_mix), "win32t": w_in.T,
        "conv_w": conv_w.astype(F32), "conv_b": row(conv_b),
        "gq": row(jnp.tile(q_norm_fox, FOX_HEADS)) * (SCALE * LOG2E), "gk": row(jnp.tile(k_norm_fox, FOX_HEADS)),
        "gqm": row(jnp.tile(q_norm_mem, MEM_HEADS)),
        "bf_rep": row(jnp.repeat(b_forget, AUG_GROUP)),
        "bf": jnp.pad(row(b_forget), ((0, 0), (0, LANES - FOX_HEADS))),
        "gmem": row(norm_mem), "wmem": w_mem_kv.astype(BF16), "gkm": row(jnp.tile(k_norm_mem, MEM_HEADS)),
        "wo": w_out.astype(BF16), "g2": row(norm_ffn2), "gfin": row(norm_final),
    }


def kernel(x_prompt, x_sample, cache_fox_k, cache_fox_v, cache_fox_logf, state_conv, cache_mem_k, cache_mem_v, mem_prompt, norm_ffn1, w_ffn1_gate, w_ffn1_up, w_ffn1_down, norm_mix, w_in, b_forget, conv_w, conv_b, q_norm_fox, k_norm_fox, q_norm_mem, k_norm_mem, norm_mem, w_mem_kv, w_out, norm_ffn2, w_ffn2_gate, w_ffn2_up, w_ffn2_down, norm_final):
    depth = w_in.shape[0]
    assert depth == 1, "single-layer step"
    B, T, _ = x_prompt.shape
    nbs, seq_s, _ = x_sample.shape
    past = cache_fox_k.shape[2]
    l = 0
    wts = _prep_weights(norm_ffn1[l], norm_mix[l], w_in[l], b_forget[l], conv_w[l], conv_b[l], q_norm_fox[l],
                        k_norm_fox[l], q_norm_mem[l], k_norm_mem[l], norm_mem[l], w_mem_kv[l], w_out[l],
                        norm_ffn2[l], norm_final[l])
    tile = TOKEN_TILE
    n_s = nbs * seq_s

    x1s, wts["wg1"], wts["wu1"], wts["wd1"] = _ffn_stream(
        (x_sample.reshape(n_s, D_MODEL),), wts["g1"], w_ffn1_gate[l], w_ffn1_up[l], w_ffn1_down[l])
    ycs, qas, kas, vbs, kfs, vfs, lfs, qms, ncs, wts["win"], wts["wtail"] = _stage1(
        x1s.reshape(1, n_s, D_MODEL), state_conv[l], wts, nb=tile // seq_s, seq=seq_s, carry=False)
    per_seq = lambda a: a.reshape(nbs, seq_s, a.shape[-1])
    feat_major = lambda a: a.reshape(a.shape[0], a.shape[1], -1).transpose(0, 2, 1)
    yas = _sample_attention(
        per_seq(qas), per_seq(kas), per_seq(vbs), per_seq(qms),
        feat_major(cache_fox_k[l]), feat_major(cache_fox_v[l]), feat_major(cache_fox_logf[l]),
        feat_major(cache_mem_k[l]), feat_major(cache_mem_v[l]))
    y_sample, wts["wg2"], wts["wu2"], wts["wd2"] = _ffn_stream(
        (x1s, ycs.reshape(n_s, CONV_DIM), yas.reshape(n_s, ATT_COLS)), wts["g2"],
        w_ffn2_gate[l], w_ffn2_up[l], w_ffn2_down[l], wo=wts["wo"], gfin=wts["gfin"])

    mk_p, mv_p, mkb, mvb = _memkv(mem_prompt, wts)
    x1, yc, qa, ka, vb, kf, vf, lf, qm, nc = _stage1(x_prompt, None, wts, nb=1, seq=tile, carry=True)
    ya = _prompt_attention(qa, ka, vb, qm, mkb, mvb, tq=ATT_QUERY_TILE, tk=ATT_KEY_TILE)
    y_prompt = _stage3(x1, yc, ya, wts, tm=tile)

    heads = lambda a, b, t, nh: a.reshape(1, b, t, nh, HEAD_DIM)
    token_major = lambda a, nh: a.reshape(a.shape[0], nh, -1, a.shape[2]).transpose(0, 3, 1, 2)[None]
    return (y_prompt, y_sample.reshape(nbs, seq_s, D_MODEL),
            token_major(kf, FOX_HEADS), token_major(vf, FOX_HEADS), lf.transpose(0, 2, 1)[None],
            nc.reshape(1, B, CONV_K - 1, CONV_DIM),
            token_major(mk_p, MEM_HEADS), token_major(mv_p, MEM_HEADS),
            heads(kfs, nbs, seq_s, FOX_HEADS), heads(vfs, nbs, seq_s, FOX_HEADS),
            lfs.reshape(1, nbs, seq_s, FOX_HEADS), ncs.reshape(1, nbs, CONV_K - 1, CONV_DIM))
```

```python
import functools

import jax
import jax.numpy as jnp
from jax import lax
from jax.experimental import pallas as pl
from jax.experimental.pallas import tpu as pltpu

F32 = jnp.float32
BF16 = jnp.bfloat16

D_MODEL = 1024
HEAD_DIM = 64
CONV_DIM = 256
CONV_K = 3
FOX_HEADS = 8
FOX_DIM = FOX_HEADS * HEAD_DIM
MEM_HEADS = 4
MEM_DIM = MEM_HEADS * HEAD_DIM
N_MEM = 256
D_FF = 2816
EPS = 1e-6

LANES = 128
PAIR = 2 * HEAD_DIM
FF_CHUNK = 256
N_FF_CHUNKS = D_FF // FF_CHUNK
AUG = LANES
AUG_GROUP = AUG // FOX_HEADS
N_SPLIT = 3
QK_COLS = FOX_DIM + AUG
ATT_COLS = FOX_DIM + MEM_DIM
COL_Q = 3 * CONV_DIM
COL_K = COL_Q + FOX_DIM
COL_V = COL_K + FOX_DIM
COL_F = COL_V + FOX_DIM
COL_QM = COL_F + FOX_HEADS
TAIL_COLS = MEM_DIM + AUG + LANES
NEG = -1e30
SCALE = HEAD_DIM ** -0.5
LOG2E = 1.4426950408889634
VMEM_LIMIT = 56 * 1024 * 1024
TOKEN_TILE = 512
ATT_KEY_TILE = 512
ATT_QUERY_TILE = 1024
SAMPLE_SEQS_PER_STEP = 4
WEIGHT_RING = 3
KV_RING = 3


def _idiv(x, d):
    if d & (d - 1) == 0:
        return lax.shift_right_logical(x, d.bit_length() - 1)
    return x // d


def _imod(x, d):
    if d & (d - 1) == 0:
        return x & (d - 1)
    return x % d


def _rms(x, g):
    ms = jnp.mean(x * x, axis=-1, keepdims=True)
    return x * lax.rsqrt(ms + EPS) * g


def _pair_headnorm(x, g):
    lo = lax.broadcasted_iota(jnp.int32, (1, PAIR), 1) < HEAD_DIM
    x2 = x * x
    s_lo = jnp.sum(jnp.where(lo, x2, 0.0), axis=-1, keepdims=True)
    s_hi = jnp.sum(jnp.where(lo, 0.0, x2), axis=-1, keepdims=True)
    r = jnp.where(lo, lax.rsqrt(s_lo * (1.0 / HEAD_DIM) + EPS),
                  lax.rsqrt(s_hi * (1.0 / HEAD_DIM) + EPS))
    return x * r * g


def _store_cols(ref, sl, x, transposed):
    if transposed:
        ref[sl, :] = x.T
    else:
        ref[:, sl] = x


def _split3(x):
    hi = x.astype(BF16).astype(F32)
    r = x - hi
    mid = r.astype(BF16).astype(F32)
    lo = (r - mid).astype(BF16).astype(F32)
    return hi, mid, lo


def _ffn(xn, wg_ref, wu_ref, wd_ref):
    acc = None
    for c in range(N_FF_CHUNKS):
        sl = slice(c * FF_CHUNK, (c + 1) * FF_CHUNK)
        g = jnp.dot(xn, wg_ref[:, sl], preferred_element_type=F32)
        u = jnp.dot(xn, wu_ref[:, sl], preferred_element_type=F32)
        a = (g * jax.nn.sigmoid(g) * u).astype(BF16)
        d = jnp.dot(a, wd_ref[sl, :], preferred_element_type=F32)
        acc = d if acc is None else acc + d
    return acc


def _log_sigmoid(x):
    return jnp.minimum(x, 0.0) - jnp.log1p(jnp.exp(-jnp.abs(x)))


def _softmax_rows(s):
    m = jnp.max(s, axis=-1, keepdims=True)
    p = jnp.exp(s - m)
    return p, jnp.sum(p, axis=-1, keepdims=True)


def _dot_nt(a, b):
    return lax.dot_general(a, b, (((1,), (1,)), ((), ())), preferred_element_type=F32)


_STAGE1_SHARED_IN = ("gmix", "conv_w", "conv_b", "gq", "gk", "gqm", "bf_rep", "bf")
_STAGE1_SHARED_OUT = ("yc", "qa", "ka", "vb", "kf", "vf", "lf", "qm", "nc")
_STAGE1_IN = {True: ("x", "g1", "wg1", "wu1", "wd1", "win", "wtail") + _STAGE1_SHARED_IN,
              False: ("x1", "state", "win32t") + _STAGE1_SHARED_IN}
_STAGE1_OUT = {True: ("x1",) + _STAGE1_SHARED_OUT, False: _STAGE1_SHARED_OUT + ("win", "wtail")}


def _stage1_body(nb, seq, carry, *refs):
    tm = nb * seq
    names = _STAGE1_IN[carry] + _STAGE1_OUT[carry]
    r = dict(zip(names, refs[:len(names)]))
    cs_ref, cum_ref = refs[len(names):]
    gmix_ref, cw_ref, cb_ref, gq_ref, gk_ref, gqm_ref, bfr_ref, bf_ref = (r[n] for n in _STAGE1_SHARED_IN)
    yc_ref, qa_ref, ka_ref, vb_ref, kf_ref, vf_ref, lf_ref, qm_ref, nc_ref = (r[n] for n in _STAGE1_SHARED_OUT)
    win_ref, wtail_ref = r["win"], r["wtail"]
    if carry:
        x = r["x"][...]
        y = _ffn(_rms(x, r["g1"][...]).astype(BF16), r["wg1"], r["wu1"], r["wd1"])
        x1 = x + 0.5 * y
        r["x1"][...] = x1
    else:
        x1 = r["x1"][...]
        st_ref = r["state"]

        @pl.when(pl.program_id(1) == 0)
        def _():
            w_t = r["win32t"]
            win_ref[...] = w_t[0:COL_F, :].T.astype(BF16)
            f_rows = w_t[COL_F:COL_QM, :]
            f_rep = jnp.concatenate([jnp.broadcast_to(f_rows[hd:hd + 1, :], (AUG_GROUP, D_MODEL))
                                     for hd in range(FOX_HEADS)], axis=0)
            tail_t = jnp.concatenate([w_t[COL_QM:COL_QM + MEM_DIM, :], f_rep, f_rows,
                                      jnp.zeros((LANES - FOX_HEADS, D_MODEL), F32)], axis=0)
            wtail_ref[...] = tail_t.T.astype(BF16)
    h = _rms(x1, gmix_ref[...]).astype(BF16)

    ucb = jnp.dot(h, win_ref[:, 0:COL_Q], preferred_element_type=F32)
    ci = ucb[:, CONV_DIM:2 * CONV_DIM] * ucb[:, 0:CONV_DIM]
    if carry:
        @pl.when(pl.program_id(1) == 0)
        def _():
            cs_ref[:, 0:8, :] = jnp.zeros((nb, 8, CONV_DIM), F32)
    else:
        cs_ref[:, 8 - (CONV_K - 1):8, :] = st_ref[...]
    cs_ref[:, 8:8 + seq, :] = ci.reshape(nb, seq, CONV_DIM)
    conv = cb_ref[...] + cw_ref[CONV_K - 1:CONV_K, :] * ci
    for i in range(CONV_K - 1):
        shifted = cs_ref[:, 8 - (CONV_K - 1) + i:8 - (CONV_K - 1) + i + seq, :]
        conv = conv + cw_ref[i:i + 1, :] * shifted.reshape(tm, CONV_DIM)
    yc_ref[...] = (ucb[:, 2 * CONV_DIM:3 * CONV_DIM] * conv).astype(BF16)
    tail = cs_ref[:, seq:seq + 8, :]
    nc_ref[...] = tail[:, 8 - (CONV_K - 1):, :].reshape(nc_ref.shape)
    if carry:
        cs_ref[:, 0:8, :] = tail

    for name, col, g_ref, out_ref in (("q", COL_Q, gq_ref, qa_ref), ("k", COL_K, gk_ref, ka_ref)):
        pr = jnp.dot(h, win_ref[:, col:col + FOX_DIM], preferred_element_type=F32)
        for g in range(FOX_HEADS // 2):
            sl = slice(g * PAIR, (g + 1) * PAIR)
            xn = _pair_headnorm(pr[:, sl], g_ref[:, sl])
            if name == "k":
                _store_cols(kf_ref, sl, xn, carry)
            out_ref[:, sl] = xn.astype(BF16)
    v = jnp.dot(h, win_ref[:, COL_V:COL_V + FOX_DIM], preferred_element_type=F32)
    for g in range(FOX_HEADS // 2):
        sl = slice(g * PAIR, (g + 1) * PAIR)
        _store_cols(vf_ref, sl, v[:, sl], carry)
    vb_ref[...] = v.astype(BF16)

    tail_cols = jnp.dot(h, wtail_ref[...], preferred_element_type=F32)
    for g in range(MEM_HEADS // 2):
        sl = slice(g * PAIR, (g + 1) * PAIR)
        qm_ref[:, sl] = _pair_headnorm(tail_cols[:, sl], gqm_ref[:, sl]).astype(BF16)
    fl = tail_cols[:, MEM_DIM:]
    lf = _log_sigmoid(fl[:, AUG:2 * AUG] + bf_ref[...])
    if carry:
        lf_ref[...] = lf.T[0:FOX_HEADS, :]
    else:
        lf_ref[...] = lf[:, 0:FOX_HEADS]
    logf = _log_sigmoid(fl[:, 0:AUG] + bfr_ref[...])

    row = lax.broadcasted_iota(jnp.int32, (tm, tm), 0)
    col = lax.broadcasted_iota(jnp.int32, (tm, tm), 1)
    tri = col <= row
    if nb > 1:
        tri = jnp.logical_and(tri, _idiv(row, seq) == _idiv(col, seq))
    tri = jnp.where(tri, 1.0, 0.0).astype(BF16)
    hi, mid, lo = _split3(logf)
    parts = jnp.concatenate([hi.astype(BF16), mid.astype(BF16), lo.astype(BF16)], axis=1)
    cs = jnp.dot(tri, parts, preferred_element_type=F32)
    cum = cs[:, 0:AUG] + cs[:, AUG:2 * AUG] + cs[:, 2 * AUG:3 * AUG]
    if carry:
        @pl.when(pl.program_id(1) == 0)
        def _():
            cum_ref[...] = jnp.zeros(cum_ref.shape, F32)
        cum = cum + cum_ref[0:1, :]
        cum_ref[0:1, :] = cum[tm - 1:tm, :]

    j = _imod(lax.broadcasted_iota(jnp.int32, (1, AUG), 1), AUG_GROUP)
    terms = _split3(cum * LOG2E)
    aq = jnp.where(j < 2 * N_SPLIT, 1.0, 0.0)
    ak = jnp.where(j < N_SPLIT, 1.0, 0.0)
    for i, term in enumerate(terms):
        aq = jnp.where(j == i, term, aq)
        ak = jnp.where(j == N_SPLIT + i, -term, ak)
    qa_ref[:, FOX_DIM:QK_COLS] = aq.astype(BF16)
    ka_ref[:, FOX_DIM:QK_COLS] = ak.astype(BF16)


def _stage1(x3, state, wts, *, nb, seq, carry):
    G, T, _ = x3.shape
    tm = nb * seq
    nt = T // tm
    const = lambda shape: pl.BlockSpec(shape, lambda g, t: (0,) * len(shape), pipeline_mode=pl.Buffered(1))
    tok = lambda cols: pl.BlockSpec((None, tm, cols), lambda g, t: (g, t, 0))
    if carry:
        feat = lambda rows: pl.BlockSpec((None, rows, tm), lambda g, t: (g, 0, t))
        feat_shape = lambda rows: jax.ShapeDtypeStruct((G, rows, T), F32)
    else:
        feat = lambda rows: pl.BlockSpec((None, tm, rows), lambda g, t: (g, t, 0))
        feat_shape = lambda rows: jax.ShapeDtypeStruct((G, T, rows), F32)
    shp = lambda cols, dt: jax.ShapeDtypeStruct((G, T, cols), dt)
    if carry:
        nc_spec = pl.BlockSpec((None, CONV_K - 1, CONV_DIM), lambda g, t: (g, 0, 0))
        nc_shape = jax.ShapeDtypeStruct((G, CONV_K - 1, CONV_DIM), F32)
    else:
        nc_spec = pl.BlockSpec((nb, CONV_K - 1, CONV_DIM), lambda g, t: (t, 0, 0))
        nc_shape = jax.ShapeDtypeStruct((nt * nb, CONV_K - 1, CONV_DIM), F32)
    tiled_in = {"x": (tok(D_MODEL), x3), "x1": (tok(D_MODEL), x3), "state": (nc_spec, state)}
    in_specs, args = [], []
    for n in _STAGE1_IN[carry]:
        spec, arg = tiled_in[n] if n in tiled_in else (const(wts[n].shape), wts[n])
        in_specs.append(spec)
        args.append(arg)
    outs = {"x1": (tok(D_MODEL), shp(D_MODEL, F32)), "yc": (tok(CONV_DIM), shp(CONV_DIM, BF16)),
            "qa": (tok(QK_COLS), shp(QK_COLS, BF16)), "ka": (tok(QK_COLS), shp(QK_COLS, BF16)),
            "vb": (tok(FOX_DIM), shp(FOX_DIM, BF16)), "kf": (feat(FOX_DIM), feat_shape(FOX_DIM)),
            "vf": (feat(FOX_DIM), feat_shape(FOX_DIM)), "lf": (feat(FOX_HEADS), feat_shape(FOX_HEADS)),
            "qm": (tok(MEM_DIM), shp(MEM_DIM, BF16)), "nc": (nc_spec, nc_shape)}
    if not carry:
        for n, w_shape in (("win", (D_MODEL, COL_F)), ("wtail", (D_MODEL, TAIL_COLS))):
            outs[n] = (pl.BlockSpec(w_shape, lambda g, t: (0, 0)), jax.ShapeDtypeStruct(w_shape, BF16))
    out_specs = [outs[n][0] for n in _STAGE1_OUT[carry]]
    out_shape = [outs[n][1] for n in _STAGE1_OUT[carry]]
    return pl.pallas_call(
        functools.partial(_stage1_body, nb, seq, carry),
        grid=(G, nt),
        in_specs=in_specs,
        out_specs=out_specs,
        out_shape=out_shape,
        scratch_shapes=[pltpu.VMEM((nb, seq + 8, CONV_DIM), F32), pltpu.VMEM((8, AUG), F32)],
        compiler_params=pltpu.CompilerParams(
            dimension_semantics=("arbitrary", "arbitrary"), vmem_limit_bytes=VMEM_LIMIT),
        name="stage1_carry" if carry else "stage1_batched",
    )(*args)


def _stage3_body(x1_ref, yc_ref, ya_ref, wo_ref, g2_ref, wg_ref, wu_ref, wd_ref, gfin_ref, y_ref):
    x2 = (x1_ref[...]
          + jnp.dot(yc_ref[...], wo_ref[0:CONV_DIM, :], preferred_element_type=F32)
          + jnp.dot(ya_ref[...], wo_ref[CONV_DIM:, :], preferred_element_type=F32))
    y = _ffn(_rms(x2, g2_ref[...]).astype(BF16), wg_ref, wu_ref, wd_ref)
    y_ref[...] = _rms(x2 + 0.5 * y, gfin_ref[...])


def _stage3(x1, yc, ya, wts, *, tm):
    G, T, _ = x1.shape
    const = lambda shape: pl.BlockSpec(shape, lambda g, t: (0,) * len(shape), pipeline_mode=pl.Buffered(1))
    tok = lambda cols: pl.BlockSpec((None, tm, cols), lambda g, t: (g, t, 0))
    names = ("wo", "g2", "wg2", "wu2", "wd2", "gfin")
    return pl.pallas_call(
        _stage3_body,
        grid=(G, T // tm),
        in_specs=[tok(D_MODEL), tok(CONV_DIM), tok(ATT_COLS)] + [const(wts[n].shape) for n in names],
        out_specs=tok(D_MODEL),
        out_shape=jax.ShapeDtypeStruct((G, T, D_MODEL), F32),
        compiler_params=pltpu.CompilerParams(
            dimension_semantics=("arbitrary", "arbitrary"), vmem_limit_bytes=VMEM_LIMIT),
        name="stage3",
    )(x1, yc, ya, *[wts[n] for n in names])


def _ffn_stream_body(with_out_proj, with_final_norm, *refs):
    refs = list(refs)
    n_x = 4 if with_out_proj else 1
    x_refs, refs = refs[:n_x], refs[n_x:]
    g_ref, wg_hbm, wu_hbm, wd_hbm = refs[:4]
    refs = refs[4:]
    gfin_ref = refs.pop(0) if with_final_norm else None
    y_ref, wgb_ref, wub_ref, wdb_ref, x_ref, xn_ref, acc_ref, wg_buf, wu_buf, wd_buf, sem = refs
    c = pl.program_id(0)

    def chunk_copies(k):
        slot = lax.rem(k, WEIGHT_RING)
        off = pl.multiple_of(k * FF_CHUNK, FF_CHUNK)
        return (pltpu.make_async_copy(wg_hbm.at[:, pl.ds(off, FF_CHUNK)], wg_buf.at[slot], sem.at[0, slot]),
                pltpu.make_async_copy(wu_hbm.at[:, pl.ds(off, FF_CHUNK)], wu_buf.at[slot], sem.at[1, slot]),
                pltpu.make_async_copy(wd_hbm.at[pl.ds(off, FF_CHUNK), :], wd_buf.at[slot], sem.at[2, slot]))

    @pl.when(c == 0)
    def _():
        for k in range(WEIGHT_RING - 1):
            for cp in chunk_copies(jnp.int32(k)):
                cp.start()

    @pl.when(c + (WEIGHT_RING - 1) < pl.num_programs(0))
    def _():
        for cp in chunk_copies(c + (WEIGHT_RING - 1)):
            cp.start()

    @pl.when(c == 0)
    def _():
        if with_out_proj:
            x1_ref, yc_ref, ya_ref, wo_ref = x_refs
            x = (x1_ref[...]
                 + jnp.dot(yc_ref[...], wo_ref[0:CONV_DIM, :], preferred_element_type=F32)
                 + jnp.dot(ya_ref[...], wo_ref[CONV_DIM:, :], preferred_element_type=F32))
        else:
            x = x_refs[0][...]
        x_ref[...] = x
        xn_ref[...] = _rms(x, g_ref[...]).astype(BF16)
        acc_ref[...] = jnp.zeros(acc_ref.shape, F32)

    for cp in chunk_copies(c):
        cp.wait()
    slot = lax.rem(c, WEIGHT_RING)
    wg = wg_buf[slot].astype(BF16)
    wu = wu_buf[slot].astype(BF16)
    wd = wd_buf[slot].astype(BF16)
    wgb_ref[...] = wg
    wub_ref[...] = wu
    wdb_ref[...] = wd
    xn = xn_ref[...]
    g = jnp.dot(xn, wg, preferred_element_type=F32)
    u = jnp.dot(xn, wu, preferred_element_type=F32)
    a = (g * jax.nn.sigmoid(g) * u).astype(BF16)
    acc_ref[...] += jnp.dot(a, wd, preferred_element_type=F32)

    @pl.when(c == pl.num_programs(0) - 1)
    def _():
        y = x_ref[...] + 0.5 * acc_ref[...]
        y_ref[...] = _rms(y, gfin_ref[...]) if with_final_norm else y


def _ffn_stream(x_parts, gain, wg, wu, wd, *, wo=None, gfin=None):
    m = x_parts[0].shape[0]
    const = lambda a: pl.BlockSpec(a.shape, lambda c: (0,) * a.ndim, pipeline_mode=pl.Buffered(1))
    col_chunk = pl.BlockSpec((D_MODEL, FF_CHUNK), lambda c: (0, c))
    row_chunk = pl.BlockSpec((FF_CHUNK, D_MODEL), lambda c: (c, 0))
    args = list(x_parts) + ([wo] if wo is not None else []) + [gain]
    in_specs = [const(a) for a in args] + [pl.BlockSpec(memory_space=pl.ANY)] * 3
    args += [wg, wu, wd]
    if gfin is not None:
        in_specs.append(const(gfin))
        args.append(gfin)
    return pl.pallas_call(
        functools.partial(_ffn_stream_body, wo is not None, gfin is not None),
        grid=(N_FF_CHUNKS,),
        in_specs=in_specs,
        out_specs=[pl.BlockSpec((m, D_MODEL), lambda c: (0, 0)), col_chunk, col_chunk, row_chunk],
        out_shape=[jax.ShapeDtypeStruct((m, D_MODEL), F32), jax.ShapeDtypeStruct(wg.shape, BF16),
                   jax.ShapeDtypeStruct(wu.shape, BF16), jax.ShapeDtypeStruct(wd.shape, BF16)],
        scratch_shapes=[pltpu.VMEM((m, D_MODEL), F32), pltpu.VMEM((m, D_MODEL), BF16),
                        pltpu.VMEM((m, D_MODEL), F32),
                        pltpu.VMEM((WEIGHT_RING, D_MODEL, FF_CHUNK), F32),
                        pltpu.VMEM((WEIGHT_RING, D_MODEL, FF_CHUNK), F32),
                        pltpu.VMEM((WEIGHT_RING, FF_CHUNK, D_MODEL), F32),
                        pltpu.SemaphoreType.DMA((3, WEIGHT_RING))],
        compiler_params=pltpu.CompilerParams(dimension_semantics=("arbitrary",), vmem_limit_bytes=VMEM_LIMIT),
        name="ffn_stream_out" if wo is not None else "ffn_stream_in",
    )(*args)


def _memkv_body(mem_ref, gmem_ref, w_ref, gk_ref, mk_ref, mv_ref, mkb_ref, mvb_ref):
    kv = jnp.dot(_rms(mem_ref[...], gmem_ref[...]).astype(BF16), w_ref[...], preferred_element_type=F32)
    for g in range(MEM_HEADS // 2):
        sl = slice(g * PAIR, (g + 1) * PAIR)
        mk = _pair_headnorm(kv[:, sl], gk_ref[:, sl])
        mk_ref[sl, :] = mk.T
        mkb_ref[:, sl] = mk.astype(BF16)
    mv = kv[:, MEM_DIM:]
    mv_ref[...] = mv.T
    mvb_ref[...] = mv.astype(BF16)


def _memkv(mem, wts):
    B = mem.shape[0]
    const = lambda shape: pl.BlockSpec(shape, lambda b: (0,) * len(shape))
    blk = lambda cols: pl.BlockSpec((None, N_MEM, cols), lambda b: (b, 0, 0))
    shp = lambda dt: jax.ShapeDtypeStruct((B, N_MEM, MEM_DIM), dt)
    names = ("gmem", "wmem", "gkm")
    return pl.pallas_call(
        _memkv_body,
        grid=(B,),
        in_specs=[blk(D_MODEL)] + [const(wts[n].shape) for n in names],
        out_specs=[blk(MEM_DIM)] * 4,
        out_shape=[shp(F32), shp(F32), shp(BF16), shp(BF16)],
        compiler_params=pltpu.CompilerParams(dimension_semantics=("arbitrary",)),
        name="memkv",
    )(mem, *[wts[n] for n in names])


def _head_masks():
    lane = lax.broadcasted_iota(jnp.int32, (1, LANES), 1)
    lo = lane < HEAD_DIM
    return lo, _idiv(lane, AUG_GROUP)


def _prompt_att_body(tq, tk, qa_ref, ka_hbm, vb_hbm, qm_ref, mk_ref, mv_ref,
                     o_ref, m_ref, acc_ref, ka_buf, vb_buf, sem):
    b, iq = pl.program_id(0), pl.program_id(1)
    nb, nq = pl.num_programs(0), pl.num_programs(1)
    lo, aug_head = _head_masks()
    first_block = b * lax.shift_right_logical(nq * (nq + 1), 1) + lax.shift_right_logical(iq * (iq + 1), 1)

    def block_copies(batch, j, number):
        slot = lax.rem(number, KV_RING)
        off = j * tq
        rows = pl.ds(off if isinstance(off, int) else pl.multiple_of(off, tq), tq)
        return (pltpu.make_async_copy(ka_hbm.at[batch, rows, :], ka_buf.at[slot], sem.at[0, slot]),
                pltpu.make_async_copy(vb_hbm.at[batch, rows, :], vb_buf.at[slot], sem.at[1, slot]))

    def start(batch, j, number):
        for cp in block_copies(batch, j, number):
            cp.start()

    def wait(batch, j, number):
        for cp in block_copies(batch, j, number):
            cp.wait()

    @pl.when(jnp.logical_and(b == 0, iq == 0))
    def _():
        start(b, iq, first_block)
        m_ref[...] = jnp.full(m_ref.shape, NEG, F32)
        acc_ref[...] = jnp.zeros(acc_ref.shape, F32)

    def step(masked, r0, sub, ka_ref, vb_ref):
        rows = tq - r0
        keys = slice(sub * tk, (sub + 1) * tk)
        if masked:
            visible = (lax.broadcasted_iota(jnp.int32, (tk, tk), 1)
                       <= lax.broadcasted_iota(jnp.int32, (tk, tk), 0))
        q_aug = qa_ref[r0:, FOX_DIM:QK_COLS]
        k_aug = ka_ref[keys, FOX_DIM:QK_COLS]
        ones = jnp.ones((tk, LANES), BF16)
        for h in range(FOX_HEADS):
            sl = slice((h // 2) * PAIR, (h // 2 + 1) * PAIR)
            mine = lo if h % 2 == 0 else jnp.logical_not(lo)
            qh = jnp.concatenate([jnp.where(mine, qa_ref[r0:, sl], 0.0).astype(BF16),
                                  jnp.where(aug_head == h, q_aug, 0.0).astype(BF16)], axis=1)
            kh = jnp.concatenate([ka_ref[keys, sl], k_aug], axis=1)
            s = _dot_nt(qh, kh)
            if masked:
                top = jnp.where(visible, s[:tk], NEG)
                s = top if rows == tk else jnp.concatenate([top, s[tk:]], axis=0)
            blocks = [s[:, c * LANES:(c + 1) * LANES] for c in range(tk // LANES)]
            blk_max = functools.reduce(jnp.maximum, blocks)
            m_prev = m_ref[h, r0:, :]
            m_new = jnp.maximum(m_prev, jnp.max(blk_max, axis=-1, keepdims=True))
            alpha = jnp.exp2(m_prev - m_new)
            pexp = jnp.concatenate([jnp.exp2(b - m_new) for b in blocks], axis=1).astype(BF16)
            pv = jnp.dot(pexp, jnp.concatenate([vb_ref[keys, sl], ones], axis=1), preferred_element_type=F32)
            untouched = m_prev <= NEG
            acc_ref[h, r0:, :] = jnp.where(jnp.concatenate([untouched, untouched], axis=1), pv,
                                           jnp.concatenate([alpha, alpha], axis=1) * acc_ref[h, r0:, :] + pv)
            m_ref[h, r0:, :] = m_new

    def unmasked_block(j, carry):
        number = first_block + j

        @pl.when(j + (KV_RING - 1) <= iq)
        def _():
            start(b, j + (KV_RING - 1), number + (KV_RING - 1))
        wait(b, j, number)
        slot = lax.rem(number, KV_RING)
        for sub in range(tq // tk):
            step(False, 0, sub, ka_buf.at[slot], vb_buf.at[slot])
        return carry

    lax.fori_loop(0, iq, unmasked_block, 0)

    last_q = iq == nq - 1
    next_b = jnp.where(last_q, b + 1, b)
    next_iq = jnp.where(last_q, 0, iq + 1)
    next_first = first_block + iq + 1
    has_next = jnp.logical_not(jnp.logical_and(last_q, b == nb - 1))
    for i in range(KV_RING - 1):
        @pl.when(jnp.logical_and(has_next, next_iq >= i))
        def _(i=i):
            start(next_b, i, next_first + i)

    diag = first_block + iq
    wait(b, iq, diag)
    slot = lax.rem(diag, KV_RING)
    for sub in range(tq // tk):
        step(True, sub * tk, sub, ka_buf.at[slot], vb_buf.at[slot])
    for g in range(FOX_HEADS // 2):
        even = acc_ref[2 * g, :, 0:PAIR] / acc_ref[2 * g, :, PAIR:2 * PAIR]
        odd = acc_ref[2 * g + 1, :, 0:PAIR] / acc_ref[2 * g + 1, :, PAIR:2 * PAIR]
        o_ref[:, g * PAIR:(g + 1) * PAIR] = jnp.where(lo, even, odd).astype(BF16)
        for h in (2 * g, 2 * g + 1):
            m_ref[h] = jnp.full((tq, LANES), NEG, F32)
    for g in range(MEM_HEADS // 2):
        sl = slice(g * PAIR, (g + 1) * PAIR)
        outs = []
        for mine in (lo, jnp.logical_not(lo)):
            qh = jnp.where(mine, qm_ref[:, sl], 0.0).astype(BF16)
            pexp, lsum = _softmax_rows(_dot_nt(qh, mk_ref[:, sl]) * SCALE)
            outs.append(jnp.dot(pexp.astype(BF16), mv_ref[:, sl], preferred_element_type=F32) / lsum)
        o_ref[:, FOX_DIM + g * PAIR:FOX_DIM + (g + 1) * PAIR] = jnp.where(lo, outs[0], outs[1]).astype(BF16)


def _prompt_attention(qa, ka, vb, qm, mkb, mvb, *, tq, tk):
    B, T, _ = qa.shape
    assert tq % tk == 0 and T % tq == 0
    return pl.pallas_call(
        functools.partial(_prompt_att_body, tq, tk),
        grid=(B, T // tq),
        in_specs=[
            pl.BlockSpec((None, tq, QK_COLS), lambda b, i: (b, i, 0)),
            pl.BlockSpec(memory_space=pl.ANY),
            pl.BlockSpec(memory_space=pl.ANY),
            pl.BlockSpec((None, tq, MEM_DIM), lambda b, i: (b, i, 0)),
            pl.BlockSpec((None, N_MEM, MEM_DIM), lambda b, i: (b, 0, 0)),
            pl.BlockSpec((None, N_MEM, MEM_DIM), lambda b, i: (b, 0, 0)),
        ],
        out_specs=pl.BlockSpec((None, tq, ATT_COLS), lambda b, i: (b, i, 0)),
        out_shape=jax.ShapeDtypeStruct((B, T, ATT_COLS), BF16),
        scratch_shapes=[pltpu.VMEM((FOX_HEADS, tq, LANES), F32), pltpu.VMEM((FOX_HEADS, tq, 2 * PAIR), F32),
                        pltpu.VMEM((KV_RING, tq, QK_COLS), BF16), pltpu.VMEM((KV_RING, tq, FOX_DIM), BF16),
                        pltpu.SemaphoreType.DMA((2, KV_RING))],
        compiler_params=pltpu.CompilerParams(
            dimension_semantics=("arbitrary", "arbitrary"), vmem_limit_bytes=VMEM_LIMIT),
        name="prompt_attention",
    )(qa, ka, vb, qm, mkb, mvb)


def _diag_blocks(x, nh, rows, width):
    head = _idiv(lax.broadcasted_iota(jnp.int32, (1, nh * width), 1), width)
    out = None
    for h in range(nh):
        part = jnp.where(head == h, x[h * rows:(h + 1) * rows, :], 0.0)
        out = part if out is None else out + part
    return out


def _sample_att_body(seq, past, per_step, *refs):
    for i in range(per_step):
        _sample_att_one(seq, past, *[r.at[i] for r in refs])


def _sample_att_one(seq, past, qa_ref, ka_ref, vb_ref, qm_ref, kt_ref, vt_ref, lt_ref, mkt_ref, mvt_ref, o_ref):
    blk = LANES
    nblk = past // blk
    lt = lt_ref[...]
    x = jnp.concatenate([lt[:, b * blk:(b + 1) * blk] for b in range(nblk)], axis=0)
    n = x.shape[0]
    parts = jnp.concatenate(_split3(x), axis=0).astype(BF16)
    src = lax.broadcasted_iota(jnp.int32, (blk, blk), 0)
    dst = lax.broadcasted_iota(jnp.int32, (blk, blk), 1)
    later = jnp.where(src > dst, 1.0, 0.0).astype(BF16)
    loc = jnp.dot(parts, later, preferred_element_type=F32)
    tot = jnp.dot(parts, jnp.ones((blk, blk), BF16), preferred_element_type=F32)
    loc = loc[0:n] + loc[n:2 * n] + loc[2 * n:3 * n]
    tot = tot[0:n] + tot[n:2 * n] + tot[2 * n:3 * n]
    running = jnp.zeros((FOX_HEADS, blk), F32)
    suffix = [None] * nblk
    for b in reversed(range(nblk)):
        rows = slice(b * FOX_HEADS, (b + 1) * FOX_HEADS)
        suffix[b] = loc[rows] + running
        running = running + tot[rows]
    rt = jnp.concatenate(suffix, axis=1) * LOG2E
    bias = jnp.concatenate([jnp.broadcast_to(rt[h:h + 1, :], (seq, past)) for h in range(FOX_HEADS)], axis=0)

    nrow = FOX_HEADS * seq
    lane = lax.broadcasted_iota(jnp.int32, (nrow, QK_COLS), 1)
    lane_head = jnp.where(lane < FOX_DIM, _idiv(lane, HEAD_DIM), _idiv(lane - FOX_DIM, AUG_GROUP))
    row_head = _idiv(lax.broadcasted_iota(jnp.int32, (nrow, QK_COLS), 0), seq)
    qbd = jnp.where(lane_head == row_head, jnp.concatenate([qa_ref[...]] * FOX_HEADS, axis=0), 0.0).astype(BF16)
    aug_row = _imod(lax.broadcasted_iota(jnp.int32, (AUG, past), 0), AUG_GROUP)
    kt = jnp.concatenate([kt_ref[...].astype(BF16), jnp.where(aug_row < N_SPLIT, 1.0, 0.0).astype(BF16)], axis=0)
    s_past = jnp.dot(qbd, kt, preferred_element_type=F32) + bias
    s_new = _dot_nt(qbd, ka_ref[...])
    qi = _imod(lax.broadcasted_iota(jnp.int32, (nrow, seq), 0), seq)
    kj = lax.broadcasted_iota(jnp.int32, (nrow, seq), 1)
    s_new = jnp.where(kj <= qi, s_new, NEG)
    m = jnp.maximum(jnp.max(s_past, axis=-1, keepdims=True), jnp.max(s_new, axis=-1, keepdims=True))
    p_past = jnp.exp2(s_past - m)
    p_new = jnp.exp2(s_new - m)
    lsum = jnp.sum(p_past, axis=-1, keepdims=True) + jnp.sum(p_new, axis=-1, keepdims=True)
    o = (_dot_nt(p_past.astype(BF16), vt_ref[...].astype(BF16))
         + jnp.dot(p_new.astype(BF16), vb_ref[...], preferred_element_type=F32)) / lsum
    o_ref[:, 0:FOX_DIM] = _diag_blocks(o, FOX_HEADS, seq, HEAD_DIM).astype(BF16)

    nrow_m = MEM_HEADS * seq
    lane_m = _idiv(lax.broadcasted_iota(jnp.int32, (nrow_m, MEM_DIM), 1), HEAD_DIM)
    row_m = _idiv(lax.broadcasted_iota(jnp.int32, (nrow_m, MEM_DIM), 0), seq)
    qbd_m = jnp.where(lane_m == row_m, jnp.concatenate([qm_ref[...]] * MEM_HEADS, axis=0), 0.0).astype(BF16)
    pm, lm = _softmax_rows(jnp.dot(qbd_m, mkt_ref[...].astype(BF16), preferred_element_type=F32) * SCALE)
    om = _dot_nt(pm.astype(BF16), mvt_ref[...].astype(BF16)) / lm
    o_ref[:, FOX_DIM:ATT_COLS] = _diag_blocks(om, MEM_HEADS, seq, HEAD_DIM).astype(BF16)


def _sample_attention(qa, ka, vb, qm, pk, pv, plf, mk, mv):
    nbatch, seq, _ = qa.shape
    past = pk.shape[2]
    per_step = SAMPLE_SEQS_PER_STEP
    assert nbatch % per_step == 0
    blk = lambda rows, cols: pl.BlockSpec((per_step, rows, cols), lambda b: (b, 0, 0))
    return pl.pallas_call(
        functools.partial(_sample_att_body, seq, past, per_step),
        grid=(nbatch // per_step,),
        in_specs=[blk(seq, QK_COLS), blk(seq, QK_COLS), blk(seq, FOX_DIM), blk(seq, MEM_DIM),
                  blk(FOX_DIM, past), blk(FOX_DIM, past), blk(FOX_HEADS, past),
                  blk(MEM_DIM, N_MEM), blk(MEM_DIM, N_MEM)],
        out_specs=blk(seq, ATT_COLS),
        out_shape=jax.ShapeDtypeStruct((nbatch, seq, ATT_COLS), BF16),
        compiler_params=pltpu.CompilerParams(
            dimension_semantics=("arbitrary",), vmem_limit_bytes=VMEM_LIMIT),
        name="sample_attention",
    )(qa, ka, vb, qm, pk, pv, plf, mk, mv)


def _prep_weights(norm_ffn1, norm_mix, w_in, b_forget, conv_w, conv_b,
                  q_norm_fox, k_norm_fox, q_norm_mem, k_norm_mem, norm_mem, w_mem_kv, w_out,
                  norm_ffn2, norm_final):
    row = lambda v: v.reshape(1, -1).astype(F32)
    return {
        "g1": row(norm_ffn1), "gmix": row(norm_mix), "win32t": w_in.T,
        "conv_w": conv_w.astype(F32), "conv_b": row(conv_b),
        "gq": row(jnp.tile(q_norm_fox, FOX_HEADS)) * (SCALE * LOG2E), "gk": row(jnp.tile(k_norm_fox, FOX_HEADS)),
        "gqm": row(jnp.tile(q_norm_mem, MEM_HEADS)),
        "bf_rep": row(jnp.repeat(b_forget, AUG_GROUP)),
        "bf": jnp.pad(row(b_forget), ((0, 0), (0, LANES - FOX_HEADS))),
        "gmem": row(norm_mem), "wmem": w_mem_kv.astype(BF16), "gkm": row(jnp.tile(k_norm_mem, MEM_HEADS)),
        "wo": w_out.astype(BF16), "g2": row(norm_ffn2), "gfin": row(norm_final),
    }


def kernel(x_prompt, x_sample, cache_fox_k, cache_fox_v, cache_fox_logf, state_conv, cache_mem_k, cache_mem_v, mem_prompt, norm_ffn1, w_ffn1_gate, w_ffn1_up, w_ffn1_down, norm_mix, w_in, b_forget, conv_w, conv_b, q_norm_fox, k_norm_fox, q_norm_mem, k_norm_mem, norm_mem, w_mem_kv, w_out, norm_ffn2, w_ffn2_gate, w_ffn2_up, w_ffn2_down, norm_final):
    depth = w_in.shape[0]
    assert depth == 1, "single-layer step"
    B, T, _ = x_prompt.shape
    nbs, seq_s, _ = x_sample.shape
    past = cache_fox_k.shape[2]
    l = 0
    wts = _prep_weights(norm_ffn1[l], norm_mix[l], w_in[l], b_forget[l], conv_w[l], conv_b[l], q_norm_fox[l],
                        k_norm_fox[l], q_norm_mem[l], k_norm_mem[l], norm_mem[l], w_mem_kv[l], w_out[l],
                        norm_ffn2[l], norm_final[l])
    tile = TOKEN_TILE
    n_s = nbs * seq_s

    x1s, wts["wg1"], wts["wu1"], wts["wd1"] = _ffn_stream(
        (x_sample.reshape(n_s, D_MODEL),), wts["g1"], w_ffn1_gate[l], w_ffn1_up[l], w_ffn1_down[l])
    ycs, qas, kas, vbs, kfs, vfs, lfs, qms, ncs, wts["win"], wts["wtail"] = _stage1(
        x1s.reshape(1, n_s, D_MODEL), state_conv[l], wts, nb=tile // seq_s, seq=seq_s, carry=False)
    per_seq = lambda a: a.reshape(nbs, seq_s, a.shape[-1])
    feat_major = lambda a: a.reshape(a.shape[0], a.shape[1], -1).transpose(0, 2, 1)
    yas = _sample_attention(
        per_seq(qas), per_seq(kas), per_seq(vbs), per_seq(qms),
        feat_major(cache_fox_k[l]), feat_major(cache_fox_v[l]), feat_major(cache_fox_logf[l]),
        feat_major(cache_mem_k[l]), feat_major(cache_mem_v[l]))
    y_sample, wts["wg2"], wts["wu2"], wts["wd2"] = _ffn_stream(
        (x1s, ycs.reshape(n_s, CONV_DIM), yas.reshape(n_s, ATT_COLS)), wts["g2"],
        w_ffn2_gate[l], w_ffn2_up[l], w_ffn2_down[l], wo=wts["wo"], gfin=wts["gfin"])

    mk_p, mv_p, mkb, mvb = _memkv(mem_prompt, wts)
    x1, yc, qa, ka, vb, kf, vf, lf, qm, nc = _stage1(x_prompt, None, wts, nb=1, seq=tile, carry=True)
    ya = _prompt_attention(qa, ka, vb, qm, mkb, mvb, tq=ATT_QUERY_TILE, tk=ATT_KEY_TILE)
    y_prompt = _stage3(x1, yc, ya, wts, tm=tile)

    heads = lambda a, b, t, nh: a.reshape(1, b, t, nh, HEAD_DIM)
    token_major = lambda a, nh: a.reshape(a.shape[0], nh, -1, a.shape[2]).transpose(0, 3, 1, 2)[None]
    return (y_prompt, y_sample.reshape(nbs, seq_s, D_MODEL),
            token_major(kf, FOX_HEADS), token_major(vf, FOX_HEADS), lf.transpose(0, 2, 1)[None],
            nc.reshape(1, B, CONV_K - 1, CONV_DIM),
            token_major(mk_p, MEM_HEADS), token_major(mv_p, MEM_HEADS),
            heads(kfs, nbs, seq_s, FOX_HEADS), heads(vfs, nbs, seq_s, FOX_HEADS),
            lfs.reshape(1, nbs, seq_s, FOX_HEADS), ncs.reshape(1, nbs, CONV_K - 1, CONV_DIM))
```
